```python
import math
import jax, jax.numpy as jnp
from jax import lax
import numpy as np

D_MODEL = 1024
BATCH = 4
SEQ = 4096
DEPTH = 1

HEAD_DIM = 64
N_HEADS_DIL = 8
N_HEADS_NSA = 8
N_KV_NSA = 2
NSA_REP = N_HEADS_NSA // N_KV_NSA
MIX_WIDTH = (N_HEADS_DIL + N_HEADS_NSA) * HEAD_DIM
DIL_PAIRS = ((128, 1), (512, 4), (2048, 16))
ROPE_THETA = 500000.0
ROT_DIM = HEAD_DIM // 4
BLOCK_Q = 128
CMP_LEN = 32
CMP_STRIDE = 16
CMP_HIDDEN = 128
SEL_BLOCK = 64
SEL_TOP = 16
NSA_WINDOW = 512
N_EXPERTS = 256
TOP_K = 8
N_GROUPS = 8
TOPK_GROUPS = 4
EXPERT_FF = 256
SHARED_FF = 256
ROUTED_SCALE = 2.5
MOE_BLOCK = 128
EPS = 1e-6

QKV_DIL = N_HEADS_DIL * HEAD_DIM
Q_NSA = N_HEADS_NSA * HEAD_DIM
KV_NSA = N_KV_NSA * HEAD_DIM
IN_SPLITS = (QKV_DIL, QKV_DIL, QKV_DIL, Q_NSA, KV_NSA, KV_NSA, KV_NSA, KV_NSA, KV_NSA, KV_NSA, 3 * N_HEADS_NSA)
IN_COLS = sum(IN_SPLITS)

kernel_name = "hybrid_dilated_nsa_moe_adaln"


def rms_norm(x, g):
    xf = x.astype(jnp.float32)
    y = xf * lax.rsqrt(jnp.mean(xf * xf, axis=-1, keepdims=True) + EPS)
    return (y * g.astype(jnp.float32)).astype(x.dtype)


def partial_rope(x, pos):
    half = ROT_DIM // 2
    inv_freq = ROPE_THETA ** (-jnp.arange(half, dtype=jnp.float32) / half)
    ang = pos.astype(jnp.float32)[:, None] * inv_freq[None, :]
    cos = jnp.cos(ang)[:, None, :]
    sin = jnp.sin(ang)[:, None, :]
    xf = x.astype(jnp.float32)
    x1, x2, rest = xf[..., :half], xf[..., half:ROT_DIM], xf[..., ROT_DIM:]
    return jnp.concatenate([x1 * cos - x2 * sin, x2 * cos + x1 * sin, rest], axis=-1).astype(x.dtype)


def banded_attention(q, k, v, max_dist):
    N, L, Hk, R, hd = q.shape
    blk = BLOCK_Q
    nb = -(-L // blk)
    Lp = nb * blk
    n_prev = -(-max_dist // blk)
    pad_end = Lp - L
    qp = jnp.pad(q, ((0, 0), (0, pad_end), (0, 0), (0, 0), (0, 0)))
    kp = jnp.pad(k, ((0, 0), (n_prev * blk, pad_end), (0, 0), (0, 0)))
    vp = jnp.pad(v, ((0, 0), (n_prev * blk, pad_end), (0, 0), (0, 0)))
    qb = qp.reshape(N, nb, blk, Hk, R, hd)
    kb = kp.reshape(N, nb + n_prev, blk, Hk, hd)
    vb = vp.reshape(N, nb + n_prev, blk, Hk, hd)
    kwin = jnp.concatenate([kb[:, j:j + nb] for j in range(n_prev + 1)], axis=2)
    vwin = jnp.concatenate([vb[:, j:j + nb] for j in range(n_prev + 1)], axis=2)
    qpos = jnp.arange(Lp).reshape(nb, blk)
    kpos = (jnp.arange((nb + n_prev) * blk) - n_prev * blk).reshape(nb + n_prev, blk)
    kposw = jnp.concatenate([kpos[j:j + nb] for j in range(n_prev + 1)], axis=1)
    dist = qpos[:, :, None] - kposw[:, None, :]
    mask = (dist >= 0) & (dist <= max_dist) & (kposw[:, None, :] >= 0)
    s = jnp.einsum('nbqhrd,nbkhd->nbhrqk', qb, kwin, preferred_element_type=jnp.float32) / math.sqrt(hd)
    s = jnp.where(mask[None, :, None, None], s, -jnp.inf)
    m = jnp.max(s, axis=-1, keepdims=True)
    p = jnp.exp(s - m)
    l = jnp.sum(p, axis=-1, keepdims=True)
    o = jnp.einsum('nbhrqk,nbkhd->nbqhrd', (p / l).astype(v.dtype), vwin)
    lse = (m + jnp.log(l))[..., 0]
    o = o.reshape(N, Lp, Hk, R, hd)[:, :L]
    lse = lse.transpose(0, 1, 4, 2, 3).reshape(N, Lp, Hk, R)[:, :L]
    return o, lse


def dilated_attention(q, k, v):
    B, S, H, hd = q.shape
    outs, lses = [], []
    for window, dil in DIL_PAIRS:
        L = S // dil

        def to_strided(t):
            return t.reshape(B, L, dil, H, hd).transpose(0, 2, 1, 3, 4).reshape(B * dil, L, H, hd)

        o, lse = banded_attention(to_strided(q)[:, :, :, None, :], to_strided(k), to_strided(v), window // dil)
        o = o[:, :, :, 0, :].reshape(B, dil, L, H, hd).transpose(0, 2, 1, 3, 4).reshape(B, S, H, hd)
        lse = lse[..., 0].reshape(B, dil, L, H).transpose(0, 2, 1, 3).reshape(B, S, H)
        outs.append(o)
        lses.append(lse)
    w = jax.nn.softmax(jnp.stack(lses, axis=0), axis=0)
    out = jnp.einsum('gbsh,gbshd->bshd', w.astype(q.dtype), jnp.stack(outs, axis=0))
    return out.reshape(B, S, H * hd)


def compress_blocks(t, pe, w1, w2):
    B, S, G, hd = t.shape
    n_c = (S - CMP_LEN) // CMP_STRIDE + 1
    idx = jnp.arange(n_c)[:, None] * CMP_STRIDE + jnp.arange(CMP_LEN)[None, :]
    blocks = t[:, idx] + pe[None, None, :, None, :]
    flat = blocks.transpose(0, 1, 3, 2, 4).reshape(B, n_c, G, CMP_LEN * hd)
    return jax.nn.gelu(flat @ w1) @ w2


def nsa_attention(q, q_rope, k_cmp, v_cmp, k_slc, v_slc, k_win, v_win, gate_logits, pos):
    B, S, G, R, hd = q.shape
    scale = 1.0 / math.sqrt(hd)
    n_c = k_cmp.shape[1]
    s = jnp.einsum('bsgrd,bcgd->bgrsc', q, k_cmp, preferred_element_type=jnp.float32) * scale
    cend = jnp.arange(n_c) * CMP_STRIDE + CMP_LEN - 1
    valid = cend[None, :] <= pos[:, None]
    s = jnp.where(valid, s, -jnp.inf)
    m = jnp.max(s, axis=-1, keepdims=True)
    m = jnp.where(jnp.isfinite(m), m, 0.0)
    p = jnp.exp(s - m)
    l = jnp.sum(p, axis=-1, keepdims=True)
    p = p / jnp.where(l > 0, l, 1.0)
    o_cmp = jnp.einsum('bgrsc,bcgd->bsgrd', p.astype(v_cmp.dtype), v_cmp)
    n_sel = S // SEL_BLOCK
    top = min(SEL_TOP, n_sel)
    cs = jnp.arange(n_c) * CMP_STRIDE
    ss = jnp.arange(n_sel) * SEL_BLOCK
    ov = jnp.clip(jnp.minimum(cs[:, None] + CMP_LEN, ss[None, :] + SEL_BLOCK) - jnp.maximum(cs[:, None], ss[None, :]), 0, None)
    ov = ov.astype(jnp.float32) / CMP_STRIDE
    p_slc = jnp.einsum('bgrsc,cj->bgsj', p, ov)
    blk_ids = jnp.arange(n_sel)[None, :]
    cur = (pos // SEL_BLOCK)[:, None]
    forced = (blk_ids == 0) | (blk_ids == cur) | (blk_ids == cur - 1)
    score = jnp.where(forced, jnp.inf, jnp.where(blk_ids <= cur, p_slc, -jnp.inf))
    _, sel_idx = lax.top_k(score, top)
    ks = k_slc.reshape(B, n_sel, SEL_BLOCK, G, hd).transpose(0, 3, 1, 2, 4)
    vs = v_slc.reshape(B, n_sel, SEL_BLOCK, G, hd).transpose(0, 3, 1, 2, 4)
    nq = S // BLOCK_Q
    qc = q_rope.reshape(B, nq, BLOCK_Q, G, R, hd).transpose(1, 0, 3, 4, 2, 5)
    ic = sel_idx.reshape(B, G, nq, BLOCK_Q, top).transpose(2, 0, 1, 3, 4)
    pc = pos.reshape(nq, BLOCK_Q)
    gather = jax.vmap(jax.vmap(lambda kb, ib: kb[ib]))

    def sel_block(args):
        qb, ib, pb = args
        kg = gather(ks, ib).reshape(B, G, BLOCK_Q, top * SEL_BLOCK, hd)
        vg = gather(vs, ib).reshape(B, G, BLOCK_Q, top * SEL_BLOCK, hd)
        kpos = (ib[..., None] * SEL_BLOCK + jnp.arange(SEL_BLOCK)).reshape(B, G, BLOCK_Q, top * SEL_BLOCK)
        sb = jnp.einsum('bgrqd,bgqkd->bgrqk', qb, kg, preferred_element_type=jnp.float32) * scale
        sb = jnp.where((kpos <= pb[None, None, :, None])[:, :, None], sb, -jnp.inf)
        pb_ = jax.nn.softmax(sb, axis=-1)
        return jnp.einsum('bgrqk,bgqkd->bgrqd', pb_.astype(vg.dtype), vg)

    o_slc = lax.map(sel_block, (qc, ic, pc)).transpose(1, 0, 4, 2, 3, 5).reshape(B, S, G, R, hd)
    o_win, _ = banded_attention(q_rope, k_win, v_win, NSA_WINDOW - 1)
    g = jax.nn.sigmoid(gate_logits.astype(jnp.float32)).reshape(B, S, G, R, 3)
    o = g[..., 0:1] * o_cmp + g[..., 1:2] * o_slc + g[..., 2:3] * o_win
    return o.astype(q.dtype).reshape(B, S, G * R * hd)


def moe_ffn(h, w_router, router_bias, w_gate, w_up, w_down, ws_gate, ws_up, ws_down):
    B, S, D = h.shape
    T = B * S
    xf = h.reshape(T, D)
    aff = jax.nn.sigmoid(jnp.dot(xf, w_router, preferred_element_type=jnp.float32))
    biased = aff + router_bias.astype(jnp.float32)
    grp = biased.reshape(T, N_GROUPS, N_EXPERTS // N_GROUPS)
    grp_score = lax.top_k(grp, 2)[0].sum(-1)
    _, gidx = lax.top_k(grp_score, TOPK_GROUPS)
    gmask = jax.nn.one_hot(gidx, N_GROUPS, dtype=jnp.float32).sum(-2) > 0
    emask = jnp.repeat(gmask, N_EXPERTS // N_GROUPS, axis=-1)
    _, eidx = lax.top_k(jnp.where(emask, biased, -jnp.inf), TOP_K)
    sel = jnp.take_along_axis(aff, eidx, axis=-1)
    gates = sel / jnp.sum(sel, axis=-1, keepdims=True) * ROUTED_SCALE
    A = T * TOP_K
    flat_e = eidx.reshape(A)
    flat_t = jnp.repeat(jnp.arange(T, dtype=jnp.int32), TOP_K)
    flat_w = gates.reshape(A)
    order = jnp.argsort(flat_e)
    se, st, sw = flat_e[order], flat_t[order], flat_w[order]
    counts = jnp.bincount(flat_e, length=N_EXPERTS)
    padded = (counts + MOE_BLOCK - 1) // MOE_BLOCK * MOE_BLOCK
    pend = jnp.cumsum(padded)
    pstart = pend - padded
    ustart = jnp.cumsum(counts) - counts
    dest = pstart[se] + jnp.arange(A) - ustart[se]
    P = A + N_EXPERTS * MOE_BLOCK
    nb = P // MOE_BLOCK
    tok_buf = jnp.full((P,), T, dtype=jnp.int32).at[dest].set(st)
    w_buf = jnp.zeros((P,), jnp.float32).at[dest].set(sw)
    blk_e = jnp.minimum(jnp.searchsorted(pend, jnp.arange(nb) * MOE_BLOCK, side='right'), N_EXPERTS - 1)
    x_pad = jnp.concatenate([xf, jnp.zeros((1, D), xf.dtype)], axis=0)

    def expert_block(args):
        e, toks, wts = args
        xb = x_pad[toks]
        y = (jax.nn.silu(xb @ w_gate[e]) * (xb @ w_up[e])) @ w_down[e]
        return y * wts[:, None].astype(y.dtype)

    y = lax.map(expert_block, (blk_e, tok_buf.reshape(nb, MOE_BLOCK), w_buf.reshape(nb, MOE_BLOCK)))
    routed = jax.ops.segment_sum(y.reshape(P, D), tok_buf, num_segments=T + 1)[:T]
    shared = (jax.nn.silu(xf @ ws_gate) * (xf @ ws_up)) @ ws_down
    return (routed + shared).reshape(B, S, D)


def setup_inputs(seed: int = 0) -> dict:
    key = jax.random.key(seed)
    ks = jax.random.split(key, 32)
    f32 = jnp.float32
    D, L = D_MODEL, DEPTH

    def nrm(k, shape, scale):
        return jax.random.normal(k, shape, f32) * scale

    def gain(k, shape):
        return 1.0 + 0.02 * jax.random.normal(k, shape, f32)

    return {
        "x": nrm(ks[0], (BATCH, SEQ, D), 1.0),
        "c": nrm(ks[1], (BATCH, D), 1.0),
        "w_ada": nrm(ks[2], (L, D, 6 * D), 0.5 * D ** -0.5),
        "b_ada": nrm(ks[3], (L, 6 * D), 0.02),
        "g_norm_mix": gain(ks[4], (L, D)),
        "g_norm_ffn": gain(ks[5], (L, D)),
        "w_in": nrm(ks[6], (L, D, IN_COLS), D ** -0.5),
        "g_q_dil": gain(ks[7], (L, HEAD_DIM)),
        "g_k_dil": gain(ks[8], (L, HEAD_DIM)),
        "g_q_nsa": gain(ks[9], (L, HEAD_DIM)),
        "g_k_cmp": gain(ks[10], (L, HEAD_DIM)),
        "g_k_slc": gain(ks[11], (L, HEAD_DIM)),
        "g_k_win": gain(ks[12], (L, HEAD_DIM)),
        "cmp_pe_k": nrm(ks[13], (L, CMP_LEN, HEAD_DIM), 0.1),
        "cmp_w1_k": nrm(ks[14], (L, CMP_LEN * HEAD_DIM, CMP_HIDDEN), (CMP_LEN * HEAD_DIM) ** -0.5),
        "cmp_w2_k": nrm(ks[15], (L, CMP_HIDDEN, HEAD_DIM), CMP_HIDDEN ** -0.5),
        "cmp_pe_v": nrm(ks[16], (L, CMP_LEN, HEAD_DIM), 0.1),
        "cmp_w1_v": nrm(ks[17], (L, CMP_LEN * HEAD_DIM, CMP_HIDDEN), (CMP_LEN * HEAD_DIM) ** -0.5),
        "cmp_w2_v": nrm(ks[18], (L, CMP_HIDDEN, HEAD_DIM), CMP_HIDDEN ** -0.5),
        "w_out": nrm(ks[19], (L, MIX_WIDTH, D), MIX_WIDTH ** -0.5),
        "w_router": nrm(ks[20], (L, D, N_EXPERTS), D ** -0.5),
        "router_bias": nrm(ks[21], (L, N_EXPERTS), 0.01),
        "w_gate": nrm(ks[22], (L, N_EXPERTS, D, EXPERT_FF), D ** -0.5),
        "w_up": nrm(ks[23], (L, N_EXPERTS, D, EXPERT_FF), D ** -0.5),
        "w_down": nrm(ks[24], (L, N_EXPERTS, EXPERT_FF, D), EXPERT_FF ** -0.5),
        "ws_gate": nrm(ks[25], (L, D, SHARED_FF), D ** -0.5),
        "ws_up": nrm(ks[26], (L, D, SHARED_FF), D ** -0.5),
        "ws_down": nrm(ks[27], (L, SHARED_FF, D), SHARED_FF ** -0.5),
    }


def reference(x, c, w_ada, b_ada, g_norm_mix, g_norm_ffn, w_in, g_q_dil, g_k_dil, g_q_nsa, g_k_cmp, g_k_slc, g_k_win,
              cmp_pe_k, cmp_w1_k, cmp_w2_k, cmp_pe_v, cmp_w1_v, cmp_w2_v, w_out, w_router, router_bias,
              w_gate, w_up, w_down, ws_gate, ws_up, ws_down):
    B, S, D = x.shape
    hd = HEAD_DIM
    G, R = N_KV_NSA, NSA_REP
    pos = jnp.arange(S)
    split_at = np.cumsum(IN_SPLITS)[:-1].tolist()
    for l in range(DEPTH):
        mod = jnp.dot(jax.nn.silu(c), w_ada[l]) + b_ada[l]
        sh1, sc1, gt1, sh2, sc2, gt2 = jnp.split(mod[:, None, :], 6, axis=-1)
        h = rms_norm(x, g_norm_mix[l]) * (1.0 + sc1) + sh1
        proj = h @ w_in[l]
        qa, ka, va, qb, kc, vc, ksl, vsl, kw, vw, gb = jnp.split(proj, split_at, axis=-1)
        qa = partial_rope(rms_norm(qa.reshape(B, S, N_HEADS_DIL, hd), g_q_dil[l]), pos)
        ka = partial_rope(rms_norm(ka.reshape(B, S, N_HEADS_DIL, hd), g_k_dil[l]), pos)
        o_a = dilated_attention(qa, ka, va.reshape(B, S, N_HEADS_DIL, hd))
        qn = rms_norm(qb.reshape(B, S, N_HEADS_NSA, hd), g_q_nsa[l])
        q_rope = partial_rope(qn, pos).reshape(B, S, G, R, hd)
        qn = qn.reshape(B, S, G, R, hd)
        k_cmp = rms_norm(compress_blocks(kc.reshape(B, S, G, hd), cmp_pe_k[l], cmp_w1_k[l], cmp_w2_k[l]), g_k_cmp[l])
        v_cmp = compress_blocks(vc.reshape(B, S, G, hd), cmp_pe_v[l], cmp_w1_v[l], cmp_w2_v[l])
        k_slc = partial_rope(rms_norm(ksl.reshape(B, S, G, hd), g_k_slc[l]), pos)
        k_win = partial_rope(rms_norm(kw.reshape(B, S, G, hd), g_k_win[l]), pos)
        o_b = nsa_attention(qn, q_rope, k_cmp, v_cmp, k_slc, vsl.reshape(B, S, G, hd), k_win,
                            vw.reshape(B, S, G, hd), gb, pos)
        mix = jnp.concatenate([o_a, o_b.astype(o_a.dtype)], axis=-1) @ w_out[l]
        x = x + gt1 * mix
        h = rms_norm(x, g_norm_ffn[l]) * (1.0 + sc2) + sh2
        x = x + gt2 * moe_ffn(h, w_router[l], router_bias[l], w_gate[l], w_up[l], w_down[l],
                              ws_gate[l], ws_up[l], ws_down[l])
    return x
```

```python
import functools
import math

import numpy as np
import jax
import jax.numpy as jnp
from jax import lax
from jax.experimental import pallas as pl
from jax.experimental.pallas import tpu as pltpu

F32 = jnp.float32
BF16 = jnp.bfloat16
I32 = jnp.int32

HD = 64
LANES = 128
NH_DIL = 8
NH_NSA = 8
NKV_NSA = 2
NSA_REP = NH_NSA // NKV_NSA
DIL_PAIRS = ((128, 1), (512, 4), (2048, 16))
ROPE_THETA = 500000.0
ROT_DIM = HD // 4
ROT_HALF = ROT_DIM // 2
CMP_LEN = 32
CMP_STRIDE = 16
CMP_HIDDEN = 128
SEL_BLOCK = 64
SEL_SHIFT = 6
SEL_TOP = 16
NSA_WINDOW = 512
N_EXPERTS = 256
TOP_K = 8
N_GROUPS = 8
GROUP_SIZE = N_EXPERTS // N_GROUPS
TOPK_GROUPS = 4
EXPERT_FF = 256
SHARED_FF = 256
ROUTED_SCALE = 2.5
EPS = 1e-6
QK_SCALE = 1.0 / math.sqrt(HD)

NEG = -1e30
BIG = 3e38

C_QA, C_KA, C_VA, C_QB = 0, 512, 1024, 1536
C_KC, C_VC, C_KSL, C_VSL, C_KW, C_VW, C_GB = 2048, 2176, 2304, 2432, 2560, 2688, 2816
IN_COLS_PADDED = 2944

MOE_BLOCK = 128
VMEM_LIMIT = 48 * 1024 * 1024


def _cparams(sem, vmem=VMEM_LIMIT):
    return pltpu.CompilerParams(dimension_semantics=sem, vmem_limit_bytes=vmem)


def _sigmoid(v):
    return 1.0 / (1.0 + jnp.exp(-v))


def _silu(v):
    return v * _sigmoid(v)


def _dot(a, b):
    return jnp.dot(a, b, preferred_element_type=F32)


def _dot_nt(a, b):
    return lax.dot_general(a, b, (((1,), (1,)), ((), ())), preferred_element_type=F32)


def _lane_lo(shape):
    return lax.broadcasted_iota(I32, shape, len(shape) - 1) < HD


def _head_sums(v, ones_bd):
    hi = v.astype(BF16)
    lo = (v - hi.astype(F32)).astype(BF16)
    return _dot(hi, ones_bd) + _dot(lo, ones_bd)


def _head_rmsnorm(y, ones_bd, gain):
    ms = _head_sums(y * y, ones_bd) * (1.0 / HD)
    return y * lax.rsqrt(ms + EPS) * gain


def _rope(y, cos, sin_a, sin_b):
    return y * cos + pltpu.roll(y, LANES - ROT_HALF, 1) * sin_a + pltpu.roll(y, ROT_HALF, 1) * sin_b


def _ada_kernel(c_ref, w_ref, b_ref, o_ref):
    a = _silu(c_ref[...]).astype(BF16)
    o_ref[...] = _dot(a, w_ref[...].astype(BF16)) + b_ref[...]


def _ada_mod(c_pad, w_ada, b_ada):
    rows, d = c_pad.shape
    n = w_ada.shape[1]
    tn = 1536 if n % 1536 == 0 else n
    return pl.pallas_call(
        _ada_kernel,
        grid=(n // tn,),
        in_specs=[pl.BlockSpec((rows, d), lambda j: (0, 0)),
                  pl.BlockSpec((d, tn), lambda j: (0, j)),
                  pl.BlockSpec((1, tn), lambda j: (0, j))],
        out_specs=pl.BlockSpec((rows, tn), lambda j: (0, j)),
        out_shape=jax.ShapeDtypeStruct((rows, n), F32),
        compiler_params=_cparams(("arbitrary",)),
        name="ada_mod",
    )(c_pad, w_ada, b_ada)


def _inproj_kernel(x_ref, g_ref, sc_ref, sh_ref, w_ref, gcol_ref, cos_ref, sa_ref, sb_ref, ones_ref,
                   qa_ref, ka_ref, va_ref, qn_ref, qr_ref, kc_ref, vc_ref, ksl_ref, vsl_ref, kw_ref, vw_ref,
                   gate_ref):
    x = x_ref[0]
    ms = jnp.mean(x * x, axis=-1, keepdims=True)
    h = (x * lax.rsqrt(ms + EPS)) * g_ref[...]
    h = h * (1.0 + sc_ref[0]) + sh_ref[0]
    hb = h.astype(BF16)
    ones_bd = ones_ref[...]
    cos, sa, sb = cos_ref[...], sa_ref[...], sb_ref[...]
    lo = _lane_lo(cos.shape)

    def proj(c0, width):
        return _dot(hb, w_ref[:, c0:c0 + width])

    def normed(tile, c0):
        return _head_rmsnorm(tile, ones_bd, gcol_ref[:, c0:c0 + LANES])

    acc = proj(C_QA, 512)
    for p in range(4):
        y = _rope(normed(acc[:, p * LANES:(p + 1) * LANES], C_QA + p * LANES), cos, sa, sb) * QK_SCALE
        qa_ref[0, :, (2 * p) * LANES:(2 * p + 1) * LANES] = jnp.where(lo, y, 0.0).astype(BF16)
        qa_ref[0, :, (2 * p + 1) * LANES:(2 * p + 2) * LANES] = jnp.where(lo, 0.0, y).astype(BF16)
    acc = proj(C_KA, 512)
    for p in range(4):
        y = _rope(normed(acc[:, p * LANES:(p + 1) * LANES], C_KA + p * LANES), cos, sa, sb)
        ka_ref[0, :, p * LANES:(p + 1) * LANES] = y.astype(BF16)
    va_ref[0] = proj(C_VA, 512).astype(BF16)

    acc = proj(C_QB, 512)
    for p in range(4):
        yn = normed(acc[:, p * LANES:(p + 1) * LANES], C_QB + p * LANES)
        yr = _rope(yn, cos, sa, sb)
        for y, ref in ((yn * QK_SCALE, qn_ref), (yr * QK_SCALE, qr_ref)):
            ysw = pltpu.roll(y, HD, 1)
            for half in range(2):
                head = 2 * p + half
                grp = head // NSA_REP
                src = y if grp == half else ysw
                keep = lo if grp == 0 else jnp.logical_not(lo)
                ref[0, :, head * LANES:(head + 1) * LANES] = jnp.where(keep, src, 0.0).astype(BF16)

    acc = proj(C_KC, 512)
    kc_ref[0] = acc[:, 0:LANES]
    vc_ref[0] = acc[:, LANES:2 * LANES]
    ksl_ref[0] = _rope(normed(acc[:, 2 * LANES:3 * LANES], C_KSL), cos, sa, sb).astype(BF16)
    vsl_ref[0] = acc[:, 3 * LANES:4 * LANES].astype(BF16)
    acc = proj(C_KW, 384)
    kw_ref[0] = _rope(normed(acc[:, 0:LANES], C_KW), cos, sa, sb).astype(BF16)
    vw_ref[0] = acc[:, LANES:2 * LANES].astype(BF16)
    gate_ref[0] = _sigmoid(acc[:, 2 * LANES:3 * LANES])


def _in_projection(x, g_mix, sc1, sh1, w_all, gcol, cos_t, sa_t, sb_t, ones_bd, tm):
    B, S, D = x.shape
    nc = w_all.shape[1]
    row = lambda b, i: (b, i, 0)
    bvec = pl.BlockSpec((1, 1, D), lambda b, i: (b, 0, 0))
    tab = pl.BlockSpec((tm, LANES), lambda b, i: (i, 0))
    const2 = lambda shape: pl.BlockSpec(shape, lambda b, i: (0, 0))
    widths_dt = [(1024, BF16), (512, BF16), (512, BF16), (1024, BF16), (1024, BF16), (LANES, F32), (LANES, F32),
                 (LANES, BF16), (LANES, BF16), (LANES, BF16), (LANES, BF16), (LANES, F32)]
    return pl.pallas_call(
        _inproj_kernel,
        grid=(B, S // tm),
        in_specs=[pl.BlockSpec((1, tm, D), row), const2((1, D)), bvec, bvec, const2((D, nc)), const2((1, nc)),
                  tab, tab, tab, const2((LANES, LANES))],
        out_specs=[pl.BlockSpec((1, tm, w), row) for w, _ in widths_dt],
        out_shape=[jax.ShapeDtypeStruct((B, S, w), dt) for w, dt in widths_dt],
        compiler_params=_cparams(("parallel", "parallel")),
        name="in_projection",
    )(x, g_mix, sc1, sh1, w_all, gcol, cos_t, sa_t, sb_t, ones_bd)


def _pair_tile(lo, t0, half0, t1, half1):
    a = t0 if half0 == 0 else pltpu.roll(t0, HD, 1)
    b = t1 if half1 == 1 else pltpu.roll(t1, HD, 1)
    return jnp.where(lo, a, b)


def _band_kernel(*refs, tq, n_prev, max_dist, n_heads, kv_tile, kv_half, mode):
    nk = n_prev + 1
    q_ref = refs[0]
    k_refs = refs[1:1 + nk]
    v_refs = refs[1 + nk:1 + 2 * nk]
    rest = refs[1 + 2 * nk:]
    i = pl.program_id(2)

    row = lax.broadcasted_iota(I32, (tq, tq), 0)
    col = lax.broadcasted_iota(I32, (tq, tq), 1)
    diff = row - col
    biases = []
    for j in range(nk):
        d = diff + (n_prev - j) * tq
        ok = jnp.where(d >= 0, jnp.where(d <= max_dist, 0.0, NEG), NEG)
        biases.append(jnp.where(i >= n_prev - j, ok, NEG))
    lo = _lane_lo((tq, LANES))

    outs, lses = [], []
    for h in range(n_heads):
        qh = q_ref[0, :, h * LANES:(h + 1) * LANES]
        kt = kv_tile(h)
        s = [_dot_nt(qh, k_refs[j][0, :, kt * LANES:(kt + 1) * LANES]) + biases[j] for j in range(nk)]
        m = s[0].max(axis=1, keepdims=True)
        for j in range(1, nk):
            m = jnp.maximum(m, s[j].max(axis=1, keepdims=True))
        p = [jnp.exp(sj - m) for sj in s]
        l = p[0].sum(axis=1, keepdims=True)
        for j in range(1, nk):
            l = l + p[j].sum(axis=1, keepdims=True)
        o = _dot(p[0].astype(BF16), v_refs[0][0, :, kt * LANES:(kt + 1) * LANES])
        for j in range(1, nk):
            o = o + _dot(p[j].astype(BF16), v_refs[j][0, :, kt * LANES:(kt + 1) * LANES])
        outs.append(o / l)
        lses.append(m + jnp.log(l))

    for p_ in range(n_heads // 2):
        h0, h1 = 2 * p_, 2 * p_ + 1
        sl = slice(p_ * LANES, (p_ + 1) * LANES)
        o_pair = _pair_tile(lo, outs[h0], kv_half(h0), outs[h1], kv_half(h1))
        if mode == "gated":
            prev_ref, gate_ref, out_ref = rest
            c0, c1 = 3 * h0 + 2, 3 * h1 + 2
            g = jnp.where(lo, gate_ref[0, :, c0:c0 + 1], gate_ref[0, :, c1:c1 + 1])
            out_ref[0, :, sl] = (prev_ref[0, :, sl] + g * o_pair).astype(out_ref.dtype)
            continue
        lse_pair = jnp.where(lo, lses[h0], lses[h1])
        if mode == "first":
            o_out, lse_out = rest
            o_out[0, :, sl] = o_pair
            lse_out[0, :, sl] = lse_pair
            continue
        po_ref, pl_ref = rest[0], rest[1]
        lp = pl_ref[0, :, sl]
        mx = jnp.maximum(lp, lse_pair)
        wp = jnp.exp(lp - mx)
        wn = jnp.exp(lse_pair - mx)
        den = wp + wn
        merged = (po_ref[0, :, sl] * wp + o_pair * wn) / den
        if mode == "mid":
            rest[2][0, :, sl] = merged
            rest[3][0, :, sl] = mx + jnp.log(den)
        else:
            rest[2][0, :, sl] = merged.astype(rest[2].dtype)


def _band_attention(q, k, v, *, dil, tq, n_prev, max_dist, kv_tile, kv_half, mode, extra=(), out_dtype=F32):
    B, S, qw = q.shape
    kvw = k.shape[2]
    L = S // dil
    nk = n_prev + 1
    ow = NH_DIL * HD
    view = lambda a: a.reshape(B, L, dil * a.shape[2])
    cur = lambda b, r, i: (b, i, r)

    def prev_map(j):
        return lambda b, r, i: (b, jnp.maximum(i - (n_prev - j), 0), r)

    in_specs = [pl.BlockSpec((1, tq, qw), cur)]
    in_specs += [pl.BlockSpec((1, tq, kvw), prev_map(j)) for j in range(nk)] * 2
    args = [view(q)] + [view(k)] * nk + [view(v)] * nk
    for a in extra:
        in_specs.append(pl.BlockSpec((1, tq, a.shape[2]), cur))
        args.append(view(a))
    o_spec = pl.BlockSpec((1, tq, ow), cur)
    if mode in ("first", "mid"):
        out_specs = [o_spec, o_spec]
        out_shape = [jax.ShapeDtypeStruct((B, L, dil * ow), F32)] * 2
    else:
        out_specs = o_spec
        out_shape = jax.ShapeDtypeStruct((B, L, dil * ow), out_dtype)
    kern = functools.partial(_band_kernel, tq=tq, n_prev=n_prev, max_dist=max_dist, n_heads=NH_DIL,
                             kv_tile=kv_tile, kv_half=kv_half, mode=mode)
    res = pl.pallas_call(
        kern, grid=(B, dil, L // tq), in_specs=in_specs, out_specs=out_specs, out_shape=out_shape,
        compiler_params=_cparams(("parallel", "parallel", "parallel")),
        name=f"band_attention_d{dil}_{mode}",
    )(*args)
    if mode in ("first", "mid"):
        return res[0].reshape(B, S, ow), res[1].reshape(B, S, ow)
    return res.reshape(B, S, ow)


def _compress_kernel(kc_ref, vc_ref, pak_ref, pbk_ref, wak_ref, wbk_ref, w2k_ref, gk_ref, ones_ref,
                     pav_ref, pbv_ref, wav_ref, wbv_ref, w2v_ref, ko_ref, vo_ref):
    ncb = kc_ref.shape[1]

    def branch(t_ref, pa, pb, wa, wb, w2):
        t = t_ref[0]
        ua = _dot((t + pa[...]).astype(BF16), wa[...])
        ub = _dot((t + pb[...]).astype(BF16), wb[...])
        pre = ua + pltpu.roll(ub, ncb - 1, 0)
        return _dot(jax.nn.gelu(pre).astype(BF16), w2[...])

    live = lax.broadcasted_iota(I32, (ncb, LANES), 0) < ncb - 1
    kcmp = _head_rmsnorm(branch(kc_ref, pak_ref, pbk_ref, wak_ref, wbk_ref, w2k_ref), ones_ref[...], gk_ref[...])
    ko_ref[0] = jnp.where(live, kcmp, 0.0).astype(BF16)
    vo_ref[0] = jnp.where(live, branch(vc_ref, pav_ref, pbv_ref, wav_ref, wbv_ref, w2v_ref), 0.0).astype(BF16)


def _compress(kc, vc, wk, wv, gk2, ones_bd):
    B, S, _ = kc.shape
    ncb = S // CMP_STRIDE
    cw = CMP_STRIDE * LANES
    view = lambda a: a.reshape(B, ncb, cw)
    blk = pl.BlockSpec((1, ncb, cw), lambda b: (b, 0, 0))
    full = lambda a: pl.BlockSpec(a.shape, lambda b: (0,) * a.ndim)
    consts = [*wk, gk2, ones_bd, *wv]
    o_spec = pl.BlockSpec((1, ncb, LANES), lambda b: (b, 0, 0))
    return pl.pallas_call(
        _compress_kernel, grid=(B,),
        in_specs=[blk, blk] + [full(a) for a in consts],
        out_specs=[o_spec, o_spec],
        out_shape=[jax.ShapeDtypeStruct((B, ncb, LANES), BF16)] * 2,
        compiler_params=_cparams(("parallel",)),
        name="nsa_compress",
    )(view(kc), view(vc), *consts)


def _cmpsel_kernel(qn_ref, kc_ref, vc_ref, ovl_ref, ovr_ref, gate_ref, o_ref, sel_ref, *, tq, n_sel, top):
    ncb = kc_ref.shape[1]
    i = pl.program_id(1)
    pos0 = i * tq
    qpos = pos0 + lax.broadcasted_iota(I32, (tq, ncb), 0)
    cend = lax.broadcasted_iota(I32, (tq, ncb), 1) * CMP_STRIDE + (CMP_LEN - 1)
    valid = cend <= qpos
    lo = _lane_lo((tq, LANES))
    kc = kc_ref[0]
    vc = vc_ref[0]

    outs = []
    psum = [None] * NKV_NSA
    for h in range(NH_NSA):
        g = h // NSA_REP
        s = _dot_nt(qn_ref[0, :, h * LANES:(h + 1) * LANES], kc)
        m = jnp.where(valid, s, NEG).max(axis=1, keepdims=True)
        m = jnp.where(m > 0.5 * NEG, m, 0.0)
        p = jnp.where(valid, jnp.exp(s - m), 0.0)
        l = p.sum(axis=1, keepdims=True)
        p = p / jnp.where(l > 0.0, l, 1.0)
        outs.append(_dot(p.astype(BF16), vc))
        psum[g] = p if psum[g] is None else psum[g] + p

    for p_ in range(NH_NSA // 2):
        h0, h1 = 2 * p_, 2 * p_ + 1
        o_pair = _pair_tile(lo, outs[h0], h0 // NSA_REP, outs[h1], h1 // NSA_REP)
        g = jnp.where(lo, gate_ref[0, :, 3 * h0:3 * h0 + 1], gate_ref[0, :, 3 * h1:3 * h1 + 1])
        o_ref[0, :, p_ * LANES:(p_ + 1) * LANES] = g * o_pair

    def hilo_dot(a, w):
        hi = a.astype(BF16)
        return _dot(hi, w) + _dot((a - hi.astype(F32)).astype(BF16), w)

    pslc = hilo_dot(psum[0], ovl_ref[...]) + hilo_dot(psum[1], ovr_ref[...])
    pslc_t = pslc.T
    jblk = lax.broadcasted_iota(I32, (HD, tq), 0)
    cur = (pos0 + lax.broadcasted_iota(I32, (HD, tq), 1)) >> SEL_SHIFT
    forced = (jblk == 0) | (jblk == cur) | (jblk == cur - 1)
    sel_t = []
    for g in range(NKV_NSA):
        sc = jnp.where(forced, BIG, jnp.where(jblk <= cur, pslc_t[g * HD:(g + 1) * HD, :], -BIG))
        rank = jnp.zeros((HD, tq), F32)
        for k in range(n_sel):
            rk = sc[k:k + 1, :]
            rank = rank + jnp.where(jblk > k, jnp.where(rk >= sc, 1.0, 0.0), jnp.where(rk > sc, 1.0, 0.0))
        sel_t.append(jnp.where(rank < top, 1.0, 0.0))
    sel_ref[0] = jnp.concatenate(sel_t, axis=0).T.astype(BF16)


def _cmp_select(qn, kcmp, vcmp, ovl, ovr, gate, tq):
    B, S, qw = qn.shape
    ncb = kcmp.shape[1]
    n_sel = S // SEL_BLOCK
    assert n_sel <= HD and tq == LANES
    row = lambda b, i: (b, i, 0)
    cb = pl.BlockSpec((1, ncb, LANES), lambda b, i: (b, 0, 0))
    ovs = pl.BlockSpec((ncb, LANES), lambda b, i: (0, 0))
    kern = functools.partial(_cmpsel_kernel, tq=tq, n_sel=n_sel, top=min(SEL_TOP, n_sel))
    return pl.pallas_call(
        kern, grid=(B, S // tq),
        in_specs=[pl.BlockSpec((1, tq, qw), row), cb, cb, ovs, ovs, pl.BlockSpec((1, tq, LANES), row)],
        out_specs=[pl.BlockSpec((1, tq, NH_NSA * HD), row), pl.BlockSpec((1, tq, LANES), row)],
        out_shape=[jax.ShapeDtypeStruct((B, S, NH_NSA * HD), F32), jax.ShapeDtypeStruct((B, S, LANES), BF16)],
        compiler_params=_cparams(("parallel", "parallel")),
        name="nsa_cmp_select",
    )(qn, kcmp, vcmp, ovl, ovr, gate)


def _slc_kernel(q_ref, k_ref, v_ref, sel_ref, prev_ref, gate_ref, o_ref, m_ref, l_ref, acc_ref, *, tq, tk):
    i = pl.program_id(1)
    kk = pl.program_id(2)
    nkv = pl.num_programs(2)
    last = (i * tq + tq - 1) // tk

    @pl.when(kk == 0)
    def _():
        m_ref[...] = jnp.full(m_ref.shape, NEG, F32)
        l_ref[...] = jnp.zeros(l_ref.shape, F32)
        acc_ref[...] = jnp.zeros(acc_ref.shape, F32)

    @pl.when(kk <= last)
    def _():
        k0 = kk * tk
        qpos = i * tq + lax.broadcasted_iota(I32, (tq, tk), 0)
        kpos = k0 + lax.broadcasted_iota(I32, (tq, tk), 1)
        causal = kpos <= qpos
        erow = lax.broadcasted_iota(I32, (LANES, tk), 0)
        eblk = (k0 + lax.broadcasted_iota(I32, (LANES, tk), 1)) >> SEL_SHIFT
        sel = sel_ref[0]
        k = k_ref[0]
        v = v_ref[0]
        for g in range(NKV_NSA):
            expand = jnp.where(erow == eblk + g * HD, 1.0, 0.0).astype(BF16)
            chosen = _dot(sel, expand) > 0.5
            bias = jnp.where(causal, jnp.where(chosen, 0.0, NEG), NEG)
            for r in range(NSA_REP):
                h = g * NSA_REP + r
                s = _dot_nt(q_ref[0, :, h * LANES:(h + 1) * LANES], k) + bias
                m_prev = m_ref[h]
                m_new = jnp.maximum(m_prev, s.max(axis=1, keepdims=True))
                alpha = jnp.exp(m_prev - m_new)
                p = jnp.exp(s - m_new[:, 0:1])
                l_ref[h] = alpha * l_ref[h] + p.sum(axis=1, keepdims=True)
                acc_ref[h] = alpha * acc_ref[h] + _dot(p.astype(BF16), v)
                m_ref[h] = m_new

    @pl.when(kk == nkv - 1)
    def _():
        lo = _lane_lo((tq, LANES))
        for p_ in range(NH_NSA // 2):
            h0, h1 = 2 * p_, 2 * p_ + 1
            o_pair = _pair_tile(lo, acc_ref[h0] / l_ref[h0], h0 // NSA_REP, acc_ref[h1] / l_ref[h1], h1 // NSA_REP)
            c0, c1 = 3 * h0 + 1, 3 * h1 + 1
            g = jnp.where(lo, gate_ref[0, :, c0:c0 + 1], gate_ref[0, :, c1:c1 + 1])
            sl = slice(p_ * LANES, (p_ + 1) * LANES)
            o_ref[0, :, sl] = prev_ref[0, :, sl] + g * o_pair


def _selected_attention(qr, ksl, vsl, sel, prev, gate, tq, tk):
    B, S, qw = qr.shape
    row = lambda b, i, kk: (b, i, 0)
    kv = lambda b, i, kk: (b, jnp.minimum(kk, (i * tq + tq - 1) // tk), 0)
    ow = NH_NSA * HD
    kern = functools.partial(_slc_kernel, tq=tq, tk=tk)
    stat = pltpu.VMEM((NH_NSA, tq, LANES), F32)
    return pl.pallas_call(
        kern, grid=(B, S // tq, S // tk),
        in_specs=[pl.BlockSpec((1, tq, qw), row), pl.BlockSpec((1, tk, LANES), kv), pl.BlockSpec((1, tk, LANES), kv),
                  pl.BlockSpec((1, tq, LANES), row), pl.BlockSpec((1, tq, ow), row), pl.BlockSpec((1, tq, LANES), row)],
        out_specs=pl.BlockSpec((1, tq, ow), row),
        out_shape=jax.ShapeDtypeStruct((B, S, ow), F32),
        scratch_shapes=[stat, stat, stat],
        compiler_params=_cparams(("parallel", "parallel", "arbitrary")),
        name="nsa_selected_attention",
    )(qr, ksl, vsl, sel, prev, gate)


def _outproj_kernel(oa_ref, ob_ref, wt_ref, wb_ref, x_ref, gt_ref, g_ref, sc_ref, sh_ref, x1_ref, h2_ref):
    mix = _dot(oa_ref[0], wt_ref[...]) + _dot(ob_ref[0], wb_ref[...])
    x1 = x_ref[0] + gt_ref[0] * mix
    x1_ref[0] = x1
    ms = jnp.mean(x1 * x1, axis=-1, keepdims=True)
    h = (x1 * lax.rsqrt(ms + EPS)) * g_ref[...]
    h2_ref[0] = h * (1.0 + sc_ref[0]) + sh_ref[0]


def _out_projection(oa, ob, w_top, w_bot, x, gt1, g_ffn, sc2, sh2, tm):
    B, S, D = x.shape
    row = lambda b, i: (b, i, 0)
    bvec = pl.BlockSpec((1, 1, D), lambda b, i: (b, 0, 0))
    const2 = lambda a: pl.BlockSpec(a.shape, lambda b, i: (0, 0))
    xs = pl.BlockSpec((1, tm, D), row)
    return pl.pallas_call(
        _outproj_kernel, grid=(B, S // tm),
        in_specs=[pl.BlockSpec((1, tm, oa.shape[2]), row), pl.BlockSpec((1, tm, ob.shape[2]), row),
                  const2(w_top), const2(w_bot), xs, bvec, const2(g_ffn), bvec, bvec],
        out_specs=[xs, xs],
        out_shape=[jax.ShapeDtypeStruct((B, S, D), F32)] * 2,
        compiler_params=_cparams(("parallel", "parallel")),
        name="out_projection",
    )(oa, ob, w_top, w_bot, x, gt1, g_ffn, sc2, sh2)


def _router_kernel(h_ref, wr_ref, bias_ref, eidx_ref, gate_ref):
    tm = h_ref.shape[0]
    logits = _dot_nt(wr_ref[...], h_ref[...].astype(BF16))
    aff = _sigmoid(logits)
    biased = aff + bias_ref[...]
    b3 = biased.reshape(N_GROUPS, GROUP_SIZE, tm)
    jj = lax.broadcasted_iota(I32, (N_GROUPS, GROUP_SIZE, tm), 1).astype(F32)
    m1 = b3.max(axis=1, keepdims=True)
    i1 = jnp.where(b3 == m1, jj, 1e9).min(axis=1, keepdims=True)
    m2 = jnp.where(jj == i1, -jnp.inf, b3).max(axis=1, keepdims=True)
    gs = (m1 + m2).reshape(N_GROUPS, tm)
    gi = lax.broadcasted_iota(I32, (N_GROUPS, tm), 0)
    rank = jnp.zeros((N_GROUPS, tm), F32)
    for k in range(N_GROUPS):
        rk = gs[k:k + 1, :]
        rank = rank + jnp.where(gi > k, jnp.where(rk >= gs, 1.0, 0.0), jnp.where(rk > gs, 1.0, 0.0))
    gsel = (rank < TOPK_GROUPS).reshape(N_GROUPS, 1, tm)
    masked = jnp.where(gsel, b3, -jnp.inf).reshape(N_EXPERTS, tm)
    ee = lax.broadcasted_iota(I32, (N_EXPERTS, tm), 0).astype(F32)
    idx_rows, sel_rows = [], []
    for _ in range(TOP_K):
        m = masked.max(axis=0, keepdims=True)
        idx = jnp.where(masked == m, ee, 1e9).min(axis=0, keepdims=True)
        hit = ee == idx
        idx_rows.append(idx)
        sel_rows.append(jnp.where(hit, aff, 0.0).sum(axis=0, keepdims=True))
        masked = jnp.where(hit, -jnp.inf, masked)
    sel = jnp.concatenate(sel_rows, axis=0)
    eidx_ref[...] = jnp.concatenate(idx_rows, axis=0).astype(I32)
    gate_ref[...] = sel / sel.sum(axis=0, keepdims=True) * ROUTED_SCALE


def _router(h2, wr_t, bias_b, tm):
    T, D = h2.shape
    o_spec = pl.BlockSpec((TOP_K, tm), lambda i: (0, i))
    return pl.pallas_call(
        _router_kernel, grid=(T // tm,),
        in_specs=[pl.BlockSpec((tm, D), lambda i: (i, 0)), pl.BlockSpec(wr_t.shape, lambda i: (0, 0)),
                  pl.BlockSpec(bias_b.shape, lambda i: (0, 0))],
        out_specs=[o_spec, o_spec],
        out_shape=[jax.ShapeDtypeStruct((TOP_K, T), I32), jax.ShapeDtypeStruct((TOP_K, T), F32)],
        compiler_params=_cparams(("parallel",)),
        name="moe_router",
    )(h2, wr_t, bias_b)


def _experts_kernel(blk_e_ref, nvalid_ref, nblk_ref, tok_ref, wts_ref, h_hbm, wg_ref, wu_ref, wd_ref, o_hbm,
                    h_vmem, acc_ref, xg_ref, y_ref, sem):
    b = pl.program_id(0)
    nb = pl.num_programs(0)

    @pl.when(b == 0)
    def _():
        cp = pltpu.make_async_copy(h_hbm, h_vmem, sem)
        cp.start()
        acc_ref[...] = jnp.zeros(acc_ref.shape, F32)
        xg_ref[...] = jnp.zeros(xg_ref.shape, F32)
        cp.wait()

    @pl.when(b < nblk_ref[0])
    def _():
        base = b * MOE_BLOCK
        nv = nvalid_ref[b]

        def gather(r, c):
            t = tok_ref[base + r]
            xg_ref[pl.ds(r, 1), :] = h_vmem[pl.ds(t, 1), :]
            return c

        lax.fori_loop(0, nv, gather, 0)
        xb = xg_ref[...].astype(BF16)
        gt = _dot(xb, wg_ref[0].astype(BF16))
        up = _dot(xb, wu_ref[0].astype(BF16))
        y_ref[...] = _dot((_silu(gt) * up).astype(BF16), wd_ref[0].astype(BF16))

        def scatter(r, c):
            t = tok_ref[base + r]
            w = wts_ref[base + r]
            acc_ref[pl.ds(t, 1), :] = acc_ref[pl.ds(t, 1), :] + w * y_ref[pl.ds(r, 1), :]
            return c

        lax.fori_loop(0, nv, scatter, 0)

    @pl.when(b == nb - 1)
    def _():
        cp = pltpu.make_async_copy(acc_ref, o_hbm, sem)
        cp.start()
        cp.wait()


def _routed_experts(h_chunk, blk_e, nvalid, nblk, tok, wts, w_gate, w_up, w_down):
    tc, D = h_chunk.shape
    nb = blk_e.shape[0]
    ff = w_gate.shape[2]
    wmap = lambda b, be, nv, nbk, tk_, ws: (be[jnp.minimum(b, nbk[0] - 1)], 0, 0)
    grid_spec = pltpu.PrefetchScalarGridSpec(
        num_scalar_prefetch=5, grid=(nb,),
        in_specs=[pl.BlockSpec(memory_space=pl.ANY),
                  pl.BlockSpec((1, D, ff), wmap), pl.BlockSpec((1, D, ff), wmap), pl.BlockSpec((1, ff, D), wmap)],
        out_specs=pl.BlockSpec(memory_space=pl.ANY),
        scratch_shapes=[pltpu.VMEM((tc, D), F32), pltpu.VMEM((tc, D), F32), pltpu.VMEM((MOE_BLOCK, D), F32),
                        pltpu.VMEM((MOE_BLOCK, D), F32), pltpu.SemaphoreType.DMA(())],
    )
    return pl.pallas_call(
        _experts_kernel, grid_spec=grid_spec,
        out_shape=jax.ShapeDtypeStruct((tc, D), F32),
        compiler_params=_cparams(("arbitrary",), vmem=56 * 1024 * 1024),
        name="moe_routed_experts",
    )(blk_e, nvalid, nblk, tok, wts, h_chunk, w_gate, w_up, w_down)


def _dispatch_tables(eidx_c, gates_c):
    tc = eidx_c.shape[1]
    a = tc * TOP_K
    flat_e = eidx_c.T.reshape(a)
    flat_w = gates_c.T.reshape(a)
    flat_t = jnp.repeat(jnp.arange(tc, dtype=I32), TOP_K)
    order = jnp.argsort(flat_e)
    se, st, sw = flat_e[order], flat_t[order], flat_w[order]
    counts = jnp.bincount(flat_e, length=N_EXPERTS).astype(I32)
    nblocks_e = (counts + MOE_BLOCK - 1) // MOE_BLOCK
    pend = jnp.cumsum(nblocks_e * MOE_BLOCK)
    pstart = pend - nblocks_e * MOE_BLOCK
    ustart = jnp.cumsum(counts) - counts
    dest = pstart[se] + jnp.arange(a, dtype=I32) - ustart[se]
    slots = a + N_EXPERTS * MOE_BLOCK
    nb = slots // MOE_BLOCK
    tok = jnp.zeros((slots,), I32).at[dest].set(st)
    wts = jnp.zeros((slots,), F32).at[dest].set(sw)
    blk_start = jnp.arange(nb, dtype=I32) * MOE_BLOCK
    blk_e = jnp.minimum(jnp.searchsorted(pend, blk_start, side="right"), N_EXPERTS - 1).astype(I32)
    nvalid = jnp.clip(pstart[blk_e] + counts[blk_e] - blk_start, 0, MOE_BLOCK).astype(I32)
    nblk = jnp.sum(nblocks_e).astype(I32).reshape(1)
    return blk_e, nvalid, nblk, tok, wts


def _shared_kernel(h_ref, wg_ref, wu_ref, wd_ref, routed_ref, x1_ref, gt_ref, o_ref):
    hb = h_ref[0].astype(BF16)
    act = _silu(_dot(hb, wg_ref[...])) * _dot(hb, wu_ref[...])
    shared = _dot(act.astype(BF16), wd_ref[...])
    o_ref[0] = x1_ref[0] + gt_ref[0] * (routed_ref[0] + shared)


def _shared_and_residual(h2, wsg, wsu, wsd, routed, x1, gt2, tm):
    B, S, D = h2.shape
    row = lambda b, i: (b, i, 0)
    xs = pl.BlockSpec((1, tm, D), row)
    const2 = lambda a: pl.BlockSpec(a.shape, lambda b, i: (0, 0))
    return pl.pallas_call(
        _shared_kernel, grid=(B, S // tm),
        in_specs=[xs, const2(wsg), const2(wsu), const2(wsd), xs, xs, pl.BlockSpec((1, 1, D), lambda b, i: (b, 0, 0))],
        out_specs=xs,
        out_shape=jax.ShapeDtypeStruct((B, S, D), F32),
        compiler_params=_cparams(("parallel", "parallel")),
        name="shared_expert_residual",
    )(h2, wsg, wsu, wsd, routed, x1, gt2)


def _prep_in_weights(w_in, g_q_dil, g_k_dil, g_q_nsa, g_k_slc, g_k_win):
    D = w_in.shape[0]
    pad = jnp.zeros((D, IN_COLS_PADDED - w_in.shape[1]), w_in.dtype)
    w_all = jnp.concatenate([w_in, pad], axis=1).astype(BF16)
    one = jnp.ones((LANES,), F32)
    gcol = jnp.concatenate([
        jnp.tile(g_q_dil, NH_DIL), jnp.tile(g_k_dil, NH_DIL), jnp.ones((512,), F32), jnp.tile(g_q_nsa, NH_NSA),
        one, one, jnp.tile(g_k_slc, NKV_NSA), one, jnp.tile(g_k_win, NKV_NSA), one, one]).reshape(1, IN_COLS_PADDED)
    return w_all, gcol


def _rope_tables(S):
    inv_freq = ROPE_THETA ** (-jnp.arange(ROT_HALF, dtype=F32) / ROT_HALF)
    ang = jnp.arange(S).astype(F32)[:, None] * inv_freq[None, :]
    cos, sin = jnp.cos(ang), jnp.sin(ang)
    zeros = jnp.zeros((S, HD - ROT_DIM), F32)
    z8 = jnp.zeros((S, ROT_HALF), F32)
    cos_h = jnp.concatenate([cos, cos, jnp.ones((S, HD - ROT_DIM), F32)], axis=1)
    sa_h = jnp.concatenate([-sin, z8, zeros], axis=1)
    sb_h = jnp.concatenate([z8, sin, zeros], axis=1)
    two = lambda t: jnp.concatenate([t, t], axis=1)
    return two(cos_h), two(sa_h), two(sb_h)


def _prep_compress(pe, w1, w2):
    eye = jnp.eye(NKV_NSA, dtype=F32)
    w1r = w1.reshape(CMP_LEN, HD, CMP_HIDDEN)

    def half(w1h, peh):
        w = jnp.einsum("ldh,gk->lgdkh", w1h, eye).reshape(CMP_STRIDE * LANES, NKV_NSA * CMP_HIDDEN)
        p = jnp.broadcast_to(peh[:, None, :], (CMP_STRIDE, NKV_NSA, HD)).reshape(1, CMP_STRIDE * LANES)
        return p, w.astype(BF16)

    pa, wa = half(w1r[:CMP_STRIDE], pe[:CMP_STRIDE])
    pb, wb = half(w1r[CMP_STRIDE:], pe[CMP_STRIDE:])
    w2bd = jnp.einsum("hd,gk->ghkd", w2, eye).reshape(NKV_NSA * CMP_HIDDEN, LANES).astype(BF16)
    return pa, pb, wa, wb, w2bd


def _overlap_tables(S):
    ncb = S // CMP_STRIDE
    n_sel = S // SEL_BLOCK
    cs = np.arange(ncb) * CMP_STRIDE
    ss = np.arange(n_sel) * SEL_BLOCK
    ov = np.clip(np.minimum(cs[:, None] + CMP_LEN, ss[None, :] + SEL_BLOCK) - np.maximum(cs[:, None], ss[None, :]), 0, None)
    ov = ov.astype(np.float32) / CMP_STRIDE
    ovl = np.zeros((ncb, LANES), np.float32)
    ovr = np.zeros((ncb, LANES), np.float32)
    ovl[:, :n_sel] = ov
    ovr[:, HD:HD + n_sel] = ov
    return jnp.asarray(ovl, BF16), jnp.asarray(ovr, BF16)


def _block_ones():
    r = np.arange(LANES)
    return jnp.asarray((r[:, None] // HD == r[None, :] // HD).astype(np.float32), BF16)


def _layer(x, c, w_ada, b_ada, g_norm_mix, g_norm_ffn, w_in, g_q_dil, g_k_dil, g_q_nsa, g_k_cmp, g_k_slc, g_k_win,
           cmp_pe_k, cmp_w1_k, cmp_w2_k, cmp_pe_v, cmp_w1_v, cmp_w2_v, w_out, w_router, router_bias,
           w_gate, w_up, w_down, ws_gate, ws_up, ws_down):
    B, S, D = x.shape
    ones_bd = _block_ones()

    c_pad = jnp.zeros((8, D), F32).at[:B].set(c)
    mod = _ada_mod(c_pad, w_ada, b_ada.reshape(1, -1))[:B]
    sh1, sc1, gt1, sh2, sc2, gt2 = [m.reshape(B, 1, D) for m in jnp.split(mod, 6, axis=-1)]

    w_all, gcol = _prep_in_weights(w_in, g_q_dil, g_k_dil, g_q_nsa, g_k_slc, g_k_win)
    cos_t, sa_t, sb_t = _rope_tables(S)
    qa, ka, va, qn, qr, kc, vc, ksl, vsl, kw, vw, gate = _in_projection(
        x, g_norm_mix.reshape(1, D), sc1, sh1, w_all, gcol, cos_t, sa_t, sb_t, ones_bd, tm=256)

    mha = dict(kv_tile=lambda h: h // 2, kv_half=lambda h: h % 2)
    modes = ("first", "mid", "last")
    state = ()
    for (window, dil), mode in zip(DIL_PAIRS, modes):
        res = _band_attention(qa, ka, va, dil=dil, tq=128, n_prev=1, max_dist=window // dil, mode=mode,
                              extra=state, out_dtype=BF16, **mha)
        state = res if mode != "last" else ()
        o_a = res

    kcmp, vcmp = _compress(kc, vc, _prep_compress(cmp_pe_k, cmp_w1_k, cmp_w2_k),
                           _prep_compress(cmp_pe_v, cmp_w1_v, cmp_w2_v), jnp.tile(g_k_cmp, NKV_NSA).reshape(1, LANES),
                           ones_bd)
    ovl, ovr = _overlap_tables(S)
    o_cmp, sel = _cmp_select(qn, kcmp, vcmp, ovl, ovr, gate, tq=128)
    o_cs = _selected_attention(qr, ksl, vsl, sel, o_cmp, gate, tq=256, tk=512)
    gqa = dict(kv_tile=lambda h: 0, kv_half=lambda h: h // NSA_REP)
    o_b = _band_attention(qr, kw, vw, dil=1, tq=256, n_prev=2, max_dist=NSA_WINDOW - 1, mode="gated",
                          extra=(o_cs, gate), out_dtype=BF16, **gqa)

    w_out_b = w_out.astype(BF16)
    x1, h2 = _out_projection(o_a, o_b, w_out_b[:NH_DIL * HD], w_out_b[NH_DIL * HD:], x, gt1,
                             g_norm_ffn.reshape(1, D), sc2, sh2, tm=256)

    tm_r = 256
    eidx, gates = _router(h2.reshape(B * S, D), w_router.T.astype(BF16),
                          jnp.broadcast_to(router_bias.reshape(N_EXPERTS, 1), (N_EXPERTS, tm_r)), tm=tm_r)
    routed = []
    for b in range(B):
        tables = _dispatch_tables(eidx[:, b * S:(b + 1) * S], gates[:, b * S:(b + 1) * S])
        routed.append(_routed_experts(h2[b], *tables, w_gate, w_up, w_down))
    routed = jnp.stack(routed, axis=0)
    return _shared_and_residual(h2, ws_gate.astype(BF16), ws_up.astype(BF16), ws_down.astype(BF16), routed, x1, gt2,
                                tm=256)


def kernel(x, c, w_ada, b_ada, g_norm_mix, g_norm_ffn, w_in, g_q_dil, g_k_dil, g_q_nsa, g_k_cmp, g_k_slc, g_k_win, cmp_pe_k, cmp_w1_k, cmp_w2_k, cmp_pe_v, cmp_w1_v, cmp_w2_v, w_out, w_router, router_bias, w_gate, w_up, w_down, ws_gate, ws_up, ws_down):
    params = (w_ada, b_ada, g_norm_mix, g_norm_ffn, w_in, g_q_dil, g_k_dil, g_q_nsa, g_k_cmp, g_k_slc, g_k_win,
              cmp_pe_k, cmp_w1_k, cmp_w2_k, cmp_pe_v, cmp_w1_v, cmp_w2_v, w_out, w_router, router_bias,
              w_gate, w_up, w_down, ws_gate, ws_up, ws_down)
    for layer in range(w_ada.shape[0]):
        x = _layer(x, c, *[a[layer] for a in params])
    return x
```

```python
import functools
import math

import numpy as np
import jax
import jax.numpy as jnp
from jax import lax
from jax.experimental import pallas as pl
from jax.experimental.pallas import tpu as pltpu

F32 = jnp.float32
BF16 = jnp.bfloat16
I32 = jnp.int32

HD = 64
LANES = 128
NH_DIL = 8
NH_NSA = 8
NKV_NSA = 2
NSA_REP = NH_NSA // NKV_NSA
DIL_PAIRS = ((128, 1), (512, 4), (2048, 16))
ROPE_THETA = 500000.0
ROT_DIM = HD // 4
ROT_HALF = ROT_DIM // 2
CMP_LEN = 32
CMP_STRIDE = 16
CMP_HIDDEN = 128
SEL_BLOCK = 64
SEL_SHIFT = 6
SEL_TOP = 16
NSA_WINDOW = 512
N_EXPERTS = 256
TOP_K = 8
TOP_K_SHIFT = 3
N_GROUPS = 8
GROUP_SIZE = N_EXPERTS // N_GROUPS
TOPK_GROUPS = 4
EXPERT_FF = 256
SHARED_FF = 256
ROUTED_SCALE = 2.5
EPS = 1e-6
QK_SCALE = 1.0 / math.sqrt(HD)

NEG = -1e30
BIG = 3e38

C_QA, C_KA, C_VA, C_QB = 0, 512, 1024, 1536
C_KC, C_VC, C_KSL, C_VSL, C_KW, C_VW, C_GB = 2048, 2176, 2304, 2432, 2560, 2688, 2816
IN_COLS_PADDED = 2944

MOE_BLOCK = 128
NULL_ROWS = 8
SCATTER_GROUP = 16
VMEM_LIMIT = 48 * 1024 * 1024


def _cparams(sem, vmem=VMEM_LIMIT):
    return pltpu.CompilerParams(dimension_semantics=sem, vmem_limit_bytes=vmem)


def _sigmoid(v):
    return 1.0 / (1.0 + jnp.exp(-v))


def _silu(v):
    return v * _sigmoid(v)


def _dot(a, b):
    return jnp.dot(a, b, preferred_element_type=F32)


def _dot_nt(a, b):
    return lax.dot_general(a, b, (((1,), (1,)), ((), ())), preferred_element_type=F32)


def _lane_lo(shape):
    return lax.broadcasted_iota(I32, shape, len(shape) - 1) < HD


def _head_sums(v, ones_bd):
    hi = v.astype(BF16)
    lo = (v - hi.astype(F32)).astype(BF16)
    return _dot(hi, ones_bd) + _dot(lo, ones_bd)


def _head_rmsnorm(y, ones_bd, gain):
    ms = _head_sums(y * y, ones_bd) * (1.0 / HD)
    return y * lax.rsqrt(ms + EPS) * gain


def _rope(y, cos, sin_a, sin_b):
    return y * cos + pltpu.roll(y, LANES - ROT_HALF, 1) * sin_a + pltpu.roll(y, ROT_HALF, 1) * sin_b


def _ada_kernel(c_ref, w_ref, b_ref, o_ref):
    a = _silu(c_ref[...]).astype(BF16)
    o_ref[...] = _dot(a, w_ref[...].astype(BF16)) + b_ref[...]


def _ada_mod(c_pad, w_ada, b_ada):
    rows, d = c_pad.shape
    n = w_ada.shape[1]
    tn = 1536 if n % 1536 == 0 else n
    return pl.pallas_call(
        _ada_kernel,
        grid=(n // tn,),
        in_specs=[pl.BlockSpec((rows, d), lambda j: (0, 0)),
                  pl.BlockSpec((d, tn), lambda j: (0, j)),
                  pl.BlockSpec((1, tn), lambda j: (0, j))],
        out_specs=pl.BlockSpec((rows, tn), lambda j: (0, j)),
        out_shape=jax.ShapeDtypeStruct((rows, n), F32),
        compiler_params=_cparams(("arbitrary",)),
        name="ada_mod",
    )(c_pad, w_ada, b_ada)


def _inproj_kernel(x_ref, g_ref, sc_ref, sh_ref, w_ref, gcol_ref, cos_ref, sa_ref, sb_ref, ones_ref,
                   qa_ref, ka_ref, va_ref, qn_ref, qr_ref, kc_ref, vc_ref, ksl_ref, vsl_ref, kw_ref, vw_ref,
                   gate_ref):
    x = x_ref[0]
    ms = jnp.mean(x * x, axis=-1, keepdims=True)
    h = (x * lax.rsqrt(ms + EPS)) * g_ref[...]
    h = h * (1.0 + sc_ref[0]) + sh_ref[0]
    hb = h.astype(BF16)
    ones_bd = ones_ref[...]
    cos, sa, sb = cos_ref[...], sa_ref[...], sb_ref[...]
    lo = _lane_lo(cos.shape)

    def proj(c0, width):
        return _dot(hb, w_ref[:, c0:c0 + width])

    def normed(tile, c0):
        return _head_rmsnorm(tile, ones_bd, gcol_ref[:, c0:c0 + LANES])

    acc = proj(C_QA, 512)
    for p in range(4):
        y = _rope(normed(acc[:, p * LANES:(p + 1) * LANES], C_QA + p * LANES), cos, sa, sb) * QK_SCALE
        qa_ref[0, :, (2 * p) * LANES:(2 * p + 1) * LANES] = jnp.where(lo, y, 0.0).astype(BF16)
        qa_ref[0, :, (2 * p + 1) * LANES:(2 * p + 2) * LANES] = jnp.where(lo, 0.0, y).astype(BF16)
    acc = proj(C_KA, 512)
    for p in range(4):
        y = _rope(normed(acc[:, p * LANES:(p + 1) * LANES], C_KA + p * LANES), cos, sa, sb)
        ka_ref[0, :, p * LANES:(p + 1) * LANES] = y.astype(BF16)
    va_ref[0] = proj(C_VA, 512).astype(BF16)

    acc = proj(C_QB, 512)
    for p in range(4):
        yn = normed(acc[:, p * LANES:(p + 1) * LANES], C_QB + p * LANES)
        yr = _rope(yn, cos, sa, sb)
        for y, ref in ((yn * QK_SCALE, qn_ref), (yr * QK_SCALE, qr_ref)):
            ysw = pltpu.roll(y, HD, 1)
            for half in range(2):
                head = 2 * p + half
                grp = head // NSA_REP
                src = y if grp == half else ysw
                keep = lo if grp == 0 else jnp.logical_not(lo)
                ref[0, :, head * LANES:(head + 1) * LANES] = jnp.where(keep, src, 0.0).astype(BF16)

    acc = proj(C_KC, 512)
    kc_ref[0] = acc[:, 0:LANES]
    vc_ref[0] = acc[:, LANES:2 * LANES]
    ksl_ref[0] = _rope(normed(acc[:, 2 * LANES:3 * LANES], C_KSL), cos, sa, sb).astype(BF16)
    vsl_ref[0] = acc[:, 3 * LANES:4 * LANES].astype(BF16)
    acc = proj(C_KW, 384)
    kw_ref[0] = _rope(normed(acc[:, 0:LANES], C_KW), cos, sa, sb).astype(BF16)
    vw_ref[0] = acc[:, LANES:2 * LANES].astype(BF16)
    gate_ref[0] = _sigmoid(acc[:, 2 * LANES:3 * LANES])


def _in_projection(x, g_mix, sc1, sh1, w_all, gcol, cos_t, sa_t, sb_t, ones_bd, tm):
    B, S, D = x.shape
    nc = w_all.shape[1]
    row = lambda b, i: (b, i, 0)
    bvec = pl.BlockSpec((1, 1, D), lambda b, i: (b, 0, 0))
    tab = pl.BlockSpec((tm, LANES), lambda b, i: (i, 0))
    const2 = lambda shape: pl.BlockSpec(shape, lambda b, i: (0, 0))
    widths_dt = [(1024, BF16), (512, BF16), (512, BF16), (1024, BF16), (1024, BF16), (LANES, F32), (LANES, F32),
                 (LANES, BF16), (LANES, BF16), (LANES, BF16), (LANES, BF16), (LANES, F32)]
    return pl.pallas_call(
        _inproj_kernel,
        grid=(B, S // tm),
        in_specs=[pl.BlockSpec((1, tm, D), row), const2((1, D)), bvec, bvec, const2((D, nc)), const2((1, nc)),
                  tab, tab, tab, const2((LANES, LANES))],
        out_specs=[pl.BlockSpec((1, tm, w), row) for w, _ in widths_dt],
        out_shape=[jax.ShapeDtypeStruct((B, S, w), dt) for w, dt in widths_dt],
        compiler_params=_cparams(("parallel", "parallel")),
        name="in_projection",
    )(x, g_mix, sc1, sh1, w_all, gcol, cos_t, sa_t, sb_t, ones_bd)


def _pair_tile(lo, t0, half0, t1, half1):
    a = t0 if half0 == 0 else pltpu.roll(t0, HD, 1)
    b = t1 if half1 == 1 else pltpu.roll(t1, HD, 1)
    return jnp.where(lo, a, b)


def _band_kernel(*refs, tq, n_prev, max_dist, n_heads, kv_tile, kv_half, mode):
    nk = n_prev + 1
    q_ref = refs[0]
    k_refs = refs[1:1 + nk]
    v_refs = refs[1 + nk:1 + 2 * nk]
    rest = refs[1 + 2 * nk:]
    i = pl.program_id(2)

    row = lax.broadcasted_iota(I32, (tq, tq), 0)
    col = lax.broadcasted_iota(I32, (tq, tq), 1)
    diff = row - col
    biases = []
    for j in range(nk):
        d = diff + (n_prev - j) * tq
        ok = jnp.where(d >= 0, jnp.where(d <= max_dist, 0.0, NEG), NEG)
        biases.append(jnp.where(i >= n_prev - j, ok, NEG))
    lo = _lane_lo((tq, LANES))

    outs, lses = [], []
    for h in range(n_heads):
        qh = q_ref[0, :, h * LANES:(h + 1) * LANES]
        kt = kv_tile(h)
        s = [_dot_nt(qh, k_refs[j][0, :, kt * LANES:(kt + 1) * LANES]) + biases[j] for j in range(nk)]
        m = s[0].max(axis=1, keepdims=True)
        for j in range(1, nk):
            m = jnp.maximum(m, s[j].max(axis=1, keepdims=True))
        p = [jnp.exp(sj - m) for sj in s]
        l = p[0].sum(axis=1, keepdims=True)
        for j in range(1, nk):
            l = l + p[j].sum(axis=1, keepdims=True)
        o = _dot(p[0].astype(BF16), v_refs[0][0, :, kt * LANES:(kt + 1) * LANES])
        for j in range(1, nk):
            o = o + _dot(p[j].astype(BF16), v_refs[j][0, :, kt * LANES:(kt + 1) * LANES])
        outs.append(o / l)
        lses.append(m + jnp.log(l))

    for p_ in range(n_heads // 2):
        h0, h1 = 2 * p_, 2 * p_ + 1
        sl = slice(p_ * LANES, (p_ + 1) * LANES)
        o_pair = _pair_tile(lo, outs[h0], kv_half(h0), outs[h1], kv_half(h1))
        if mode == "gated":
            prev_ref, gate_ref, out_ref = rest
            c0, c1 = 3 * h0 + 2, 3 * h1 + 2
            g = jnp.where(lo, gate_ref[0, :, c0:c0 + 1], gate_ref[0, :, c1:c1 + 1])
            out_ref[0, :, sl] = (prev_ref[0, :, sl] + g * o_pair).astype(out_ref.dtype)
            continue
        lse_pair = jnp.where(lo, lses[h0], lses[h1])
        if mode == "first":
            o_out, lse_out = rest
            o_out[0, :, sl] = o_pair
            lse_out[0, :, sl] = lse_pair
            continue
        po_ref, pl_ref = rest[0], rest[1]
        lp = pl_ref[0, :, sl]
        mx = jnp.maximum(lp, lse_pair)
        wp = jnp.exp(lp - mx)
        wn = jnp.exp(lse_pair - mx)
        den = wp + wn
        merged = (po_ref[0, :, sl] * wp + o_pair * wn) / den
        if mode == "mid":
            rest[2][0, :, sl] = merged
            rest[3][0, :, sl] = mx + jnp.log(den)
        else:
            rest[2][0, :, sl] = merged.astype(rest[2].dtype)


def _band_attention(q, k, v, *, dil, tq, n_prev, max_dist, kv_tile, kv_half, mode, extra=(), out_dtype=F32):
    B, S, qw = q.shape
    kvw = k.shape[2]
    L = S // dil
    nk = n_prev + 1
    ow = NH_DIL * HD
    view = lambda a: a.reshape(B, L, dil * a.shape[2])
    cur = lambda b, r, i: (b, i, r)

    def prev_map(j):
        return lambda b, r, i: (b, jnp.maximum(i - (n_prev - j), 0), r)

    in_specs = [pl.BlockSpec((1, tq, qw), cur)]
    in_specs += [pl.BlockSpec((1, tq, kvw), prev_map(j)) for j in range(nk)] * 2
    args = [view(q)] + [view(k)] * nk + [view(v)] * nk
    for a in extra:
        in_specs.append(pl.BlockSpec((1, tq, a.shape[2]), cur))
        args.append(view(a))
    o_spec = pl.BlockSpec((1, tq, ow), cur)
    if mode in ("first", "mid"):
        out_specs = [o_spec, o_spec]
        out_shape = [jax.ShapeDtypeStruct((B, L, dil * ow), F32)] * 2
    else:
        out_specs = o_spec
        out_shape = jax.ShapeDtypeStruct((B, L, dil * ow), out_dtype)
    kern = functools.partial(_band_kernel, tq=tq, n_prev=n_prev, max_dist=max_dist, n_heads=NH_DIL,
                             kv_tile=kv_tile, kv_half=kv_half, mode=mode)
    res = pl.pallas_call(
        kern, grid=(B, dil, L // tq), in_specs=in_specs, out_specs=out_specs, out_shape=out_shape,
        compiler_params=_cparams(("parallel", "parallel", "parallel")),
        name=f"band_attention_d{dil}_{mode}",
    )(*args)
    if mode in ("first", "mid"):
        return res[0].reshape(B, S, ow), res[1].reshape(B, S, ow)
    return res.reshape(B, S, ow)


def _compress_kernel(kc_ref, vc_ref, pak_ref, pbk_ref, wak_ref, wbk_ref, w2k_ref, gk_ref, ones_ref,
                     pav_ref, pbv_ref, wav_ref, wbv_ref, w2v_ref, ko_ref, vo_ref):
    ncb = kc_ref.shape[1]

    def branch(t_ref, pa, pb, wa, wb, w2):
        t = t_ref[0]
        ua = _dot((t + pa[...]).astype(BF16), wa[...])
        ub = _dot((t + pb[...]).astype(BF16), wb[...])
        pre = ua + pltpu.roll(ub, ncb - 1, 0)
        return _dot(jax.nn.gelu(pre).astype(BF16), w2[...])

    live = lax.broadcasted_iota(I32, (ncb, LANES), 0) < ncb - 1
    kcmp = _head_rmsnorm(branch(kc_ref, pak_ref, pbk_ref, wak_ref, wbk_ref, w2k_ref), ones_ref[...], gk_ref[...])
    ko_ref[0] = jnp.where(live, kcmp, 0.0).astype(BF16)
    vo_ref[0] = jnp.where(live, branch(vc_ref, pav_ref, pbv_ref, wav_ref, wbv_ref, w2v_ref), 0.0).astype(BF16)


def _compress(kc, vc, wk, wv, gk2, ones_bd):
    B, S, _ = kc.shape
    ncb = S // CMP_STRIDE
    cw = CMP_STRIDE * LANES
    view = lambda a: a.reshape(B, ncb, cw)
    blk = pl.BlockSpec((1, ncb, cw), lambda b: (b, 0, 0))
    full = lambda a: pl.BlockSpec(a.shape, lambda b: (0,) * a.ndim)
    consts = [*wk, gk2, ones_bd, *wv]
    o_spec = pl.BlockSpec((1, ncb, LANES), lambda b: (b, 0, 0))
    return pl.pallas_call(
        _compress_kernel, grid=(B,),
        in_specs=[blk, blk] + [full(a) for a in consts],
        out_specs=[o_spec, o_spec],
        out_shape=[jax.ShapeDtypeStruct((B, ncb, LANES), BF16)] * 2,
        compiler_params=_cparams(("parallel",)),
        name="nsa_compress",
    )(view(kc), view(vc), *consts)


def _cmpsel_kernel(qn_ref, kc_ref, vc_ref, ovl_ref, ovr_ref, gate_ref, o_ref, sel_ref, *, tq, n_sel, top):
    ncb = kc_ref.shape[1]
    i = pl.program_id(1)
    pos0 = i * tq
    qpos = pos0 + lax.broadcasted_iota(I32, (tq, ncb), 0)
    cend = lax.broadcasted_iota(I32, (tq, ncb), 1) * CMP_STRIDE + (CMP_LEN - 1)
    valid = cend <= qpos
    lo = _lane_lo((tq, LANES))
    kc = kc_ref[0]
    vc = vc_ref[0]

    outs = []
    psum = [None] * NKV_NSA
    for h in range(NH_NSA):
        g = h // NSA_REP
        s = _dot_nt(qn_ref[0, :, h * LANES:(h + 1) * LANES], kc)
        m = jnp.where(valid, s, NEG).max(axis=1, keepdims=True)
        m = jnp.where(m > 0.5 * NEG, m, 0.0)
        p = jnp.where(valid, jnp.exp(s - m), 0.0)
        l = p.sum(axis=1, keepdims=True)
        p = p / jnp.where(l > 0.0, l, 1.0)
        outs.append(_dot(p.astype(BF16), vc))
        psum[g] = p if psum[g] is None else psum[g] + p

    for p_ in range(NH_NSA // 2):
        h0, h1 = 2 * p_, 2 * p_ + 1
        o_pair = _pair_tile(lo, outs[h0], h0 // NSA_REP, outs[h1], h1 // NSA_REP)
        g = jnp.where(lo, gate_ref[0, :, 3 * h0:3 * h0 + 1], gate_ref[0, :, 3 * h1:3 * h1 + 1])
        o_ref[0, :, p_ * LANES:(p_ + 1) * LANES] = g * o_pair

    def hilo_dot(a, w):
        hi = a.astype(BF16)
        return _dot(hi, w) + _dot((a - hi.astype(F32)).astype(BF16), w)

    pslc = hilo_dot(psum[0], ovl_ref[...]) + hilo_dot(psum[1], ovr_ref[...])
    pslc_t = pslc.T
    jblk = lax.broadcasted_iota(I32, (HD, tq), 0)
    cur = (pos0 + lax.broadcasted_iota(I32, (HD, tq), 1)) >> SEL_SHIFT
    forced = (jblk == 0) | (jblk == cur) | (jblk == cur - 1)
    sel_t = []
    for g in range(NKV_NSA):
        sc = jnp.where(forced, BIG, jnp.where(jblk <= cur, pslc_t[g * HD:(g + 1) * HD, :], -BIG))
        rank = jnp.zeros((HD, tq), F32)
        for k in range(n_sel):
            rk = sc[k:k + 1, :]
            rank = rank + jnp.where(jblk > k, jnp.where(rk >= sc, 1.0, 0.0), jnp.where(rk > sc, 1.0, 0.0))
        sel_t.append(jnp.where(rank < top, 1.0, 0.0))
    sel_ref[0] = jnp.concatenate(sel_t, axis=0).T.astype(BF16)


def _cmp_select(qn, kcmp, vcmp, ovl, ovr, gate, tq):
    B, S, qw = qn.shape
    ncb = kcmp.shape[1]
    n_sel = S // SEL_BLOCK
    assert n_sel <= HD and tq == LANES
    row = lambda b, i: (b, i, 0)
    cb = pl.BlockSpec((1, ncb, LANES), lambda b, i: (b, 0, 0))
    ovs = pl.BlockSpec((ncb, LANES), lambda b, i: (0, 0))
    kern = functools.partial(_cmpsel_kernel, tq=tq, n_sel=n_sel, top=min(SEL_TOP, n_sel))
    return pl.pallas_call(
        kern, grid=(B, S // tq),
        in_specs=[pl.BlockSpec((1, tq, qw), row), cb, cb, ovs, ovs, pl.BlockSpec((1, tq, LANES), row)],
        out_specs=[pl.BlockSpec((1, tq, NH_NSA * HD), row), pl.BlockSpec((1, tq, LANES), row)],
        out_shape=[jax.ShapeDtypeStruct((B, S, NH_NSA * HD), F32), jax.ShapeDtypeStruct((B, S, LANES), BF16)],
        compiler_params=_cparams(("parallel", "parallel")),
        name="nsa_cmp_select",
    )(qn, kcmp, vcmp, ovl, ovr, gate)


def _slc_kernel(q_ref, k_ref, v_ref, sel_ref, prev_ref, gate_ref, o_ref, m_ref, l_ref, acc_ref, *, tq, tk):
    i = pl.program_id(1)
    kk = pl.program_id(2)
    nkv = pl.num_programs(2)
    last = (i * tq + tq - 1) // tk

    @pl.when(kk == 0)
    def _():
        m_ref[...] = jnp.full(m_ref.shape, NEG, F32)
        l_ref[...] = jnp.zeros(l_ref.shape, F32)
        acc_ref[...] = jnp.zeros(acc_ref.shape, F32)

    @pl.when(kk <= last)
    def _():
        k0 = kk * tk
        qpos = i * tq + lax.broadcasted_iota(I32, (tq, tk), 0)
        kpos = k0 + lax.broadcasted_iota(I32, (tq, tk), 1)
        causal = kpos <= qpos
        erow = lax.broadcasted_iota(I32, (LANES, tk), 0)
        eblk = (k0 + lax.broadcasted_iota(I32, (LANES, tk), 1)) >> SEL_SHIFT
        sel = sel_ref[0]
        k = k_ref[0]
        v = v_ref[0]
        for g in range(NKV_NSA):
            expand = jnp.where(erow == eblk + g * HD, 1.0, 0.0).astype(BF16)
            chosen = _dot(sel, expand) > 0.5
            bias = jnp.where(causal, jnp.where(chosen, 0.0, NEG), NEG)
            for r in range(NSA_REP):
                h = g * NSA_REP + r
                s = _dot_nt(q_ref[0, :, h * LANES:(h + 1) * LANES], k) + bias
                m_prev = m_ref[h]
                m_new = jnp.maximum(m_prev, s.max(axis=1, keepdims=True))
                alpha = jnp.exp(m_prev - m_new)
                p = jnp.exp(s - m_new[:, 0:1])
                l_ref[h] = alpha * l_ref[h] + p.sum(axis=1, keepdims=True)
                acc_ref[h] = alpha * acc_ref[h] + _dot(p.astype(BF16), v)
                m_ref[h] = m_new

    @pl.when(kk == nkv - 1)
    def _():
        lo = _lane_lo((tq, LANES))
        for p_ in range(NH_NSA // 2):
            h0, h1 = 2 * p_, 2 * p_ + 1
            o_pair = _pair_tile(lo, acc_ref[h0] / l_ref[h0], h0 // NSA_REP, acc_ref[h1] / l_ref[h1], h1 // NSA_REP)
            c0, c1 = 3 * h0 + 1, 3 * h1 + 1
            g = jnp.where(lo, gate_ref[0, :, c0:c0 + 1], gate_ref[0, :, c1:c1 + 1])
            sl = slice(p_ * LANES, (p_ + 1) * LANES)
            o_ref[0, :, sl] = prev_ref[0, :, sl] + g * o_pair


def _selected_attention(qr, ksl, vsl, sel, prev, gate, tq, tk):
    B, S, qw = qr.shape
    row = lambda b, i, kk: (b, i, 0)
    kv = lambda b, i, kk: (b, jnp.minimum(kk, (i * tq + tq - 1) // tk), 0)
    ow = NH_NSA * HD
    kern = functools.partial(_slc_kernel, tq=tq, tk=tk)
    stat = pltpu.VMEM((NH_NSA, tq, LANES), F32)
    return pl.pallas_call(
        kern, grid=(B, S // tq, S // tk),
        in_specs=[pl.BlockSpec((1, tq, qw), row), pl.BlockSpec((1, tk, LANES), kv), pl.BlockSpec((1, tk, LANES), kv),
                  pl.BlockSpec((1, tq, LANES), row), pl.BlockSpec((1, tq, ow), row), pl.BlockSpec((1, tq, LANES), row)],
        out_specs=pl.BlockSpec((1, tq, ow), row),
        out_shape=jax.ShapeDtypeStruct((B, S, ow), F32),
        scratch_shapes=[stat, stat, stat],
        compiler_params=_cparams(("parallel", "parallel", "arbitrary")),
        name="nsa_selected_attention",
    )(qr, ksl, vsl, sel, prev, gate)


def _outproj_kernel(oa_ref, ob_ref, wt_ref, wb_ref, x_ref, gt_ref, g_ref, sc_ref, sh_ref, x1_ref, h2_ref):
    mix = _dot(oa_ref[0], wt_ref[...]) + _dot(ob_ref[0], wb_ref[...])
    x1 = x_ref[0] + gt_ref[0] * mix
    x1_ref[0] = x1
    ms = jnp.mean(x1 * x1, axis=-1, keepdims=True)
    h = (x1 * lax.rsqrt(ms + EPS)) * g_ref[...]
    h2_ref[0] = h * (1.0 + sc_ref[0]) + sh_ref[0]


def _out_projection(oa, ob, w_top, w_bot, x, gt1, g_ffn, sc2, sh2, tm):
    B, S, D = x.shape
    row = lambda b, i: (b, i, 0)
    bvec = pl.BlockSpec((1, 1, D), lambda b, i: (b, 0, 0))
    const2 = lambda a: pl.BlockSpec(a.shape, lambda b, i: (0, 0))
    xs = pl.BlockSpec((1, tm, D), row)
    return pl.pallas_call(
        _outproj_kernel, grid=(B, S // tm),
        in_specs=[pl.BlockSpec((1, tm, oa.shape[2]), row), pl.BlockSpec((1, tm, ob.shape[2]), row),
                  const2(w_top), const2(w_bot), xs, bvec, const2(g_ffn), bvec, bvec],
        out_specs=[xs, xs],
        out_shape=[jax.ShapeDtypeStruct((B, S, D), F32)] * 2,
        compiler_params=_cparams(("parallel", "parallel")),
        name="out_projection",
    )(oa, ob, w_top, w_bot, x, gt1, g_ffn, sc2, sh2)


def _router_kernel(h_ref, wr_ref, bias_ref, eidx_ref, gate_ref):
    tm = h_ref.shape[0]
    logits = _dot_nt(wr_ref[...], h_ref[...].astype(BF16))
    aff = _sigmoid(logits)
    biased = aff + bias_ref[...]
    b3 = biased.reshape(N_GROUPS, GROUP_SIZE, tm)
    jj = lax.broadcasted_iota(I32, (N_GROUPS, GROUP_SIZE, tm), 1).astype(F32)
    m1 = b3.max(axis=1, keepdims=True)
    i1 = jnp.where(b3 == m1, jj, 1e9).min(axis=1, keepdims=True)
    m2 = jnp.where(jj == i1, -jnp.inf, b3).max(axis=1, keepdims=True)
    gs = (m1 + m2).reshape(N_GROUPS, tm)
    gi = lax.broadcasted_iota(I32, (N_GROUPS, tm), 0)
    rank = jnp.zeros((N_GROUPS, tm), F32)
    for k in range(N_GROUPS):
        rk = gs[k:k + 1, :]
        rank = rank + jnp.where(gi > k, jnp.where(rk >= gs, 1.0, 0.0), jnp.where(rk > gs, 1.0, 0.0))
    gsel = (rank < TOPK_GROUPS).reshape(N_GROUPS, 1, tm)
    masked = jnp.where(gsel, b3, -jnp.inf).reshape(N_EXPERTS, tm)
    ee = lax.broadcasted_iota(I32, (N_EXPERTS, tm), 0).astype(F32)
    idx_rows, sel_rows = [], []
    for _ in range(TOP_K):
        m = masked.max(axis=0, keepdims=True)
        idx = jnp.where(masked == m, ee, 1e9).min(axis=0, keepdims=True)
        hit = ee == idx
        idx_rows.append(idx)
        sel_rows.append(jnp.where(hit, aff, 0.0).sum(axis=0, keepdims=True))
        masked = jnp.where(hit, -jnp.inf, masked)
    sel = jnp.concatenate(sel_rows, axis=0)
    eidx_ref[...] = jnp.concatenate(idx_rows, axis=0).astype(I32)
    gate_ref[...] = sel / sel.sum(axis=0, keepdims=True) * ROUTED_SCALE


def _router(h2, wr_t, bias_b, tm):
    T, D = h2.shape
    o_spec = pl.BlockSpec((TOP_K, tm), lambda i: (0, i))
    return pl.pallas_call(
        _router_kernel, grid=(T // tm,),
        in_specs=[pl.BlockSpec((tm, D), lambda i: (i, 0)), pl.BlockSpec(wr_t.shape, lambda i: (0, 0)),
                  pl.BlockSpec(bias_b.shape, lambda i: (0, 0))],
        out_specs=[o_spec, o_spec],
        out_shape=[jax.ShapeDtypeStruct((TOP_K, T), I32), jax.ShapeDtypeStruct((TOP_K, T), F32)],
        compiler_params=_cparams(("parallel",)),
        name="moe_router",
    )(h2, wr_t, bias_b)


def _experts_kernel(blk_e_ref, row0_ref, nblk_ref, skey_ref, gflat_ref, h_hbm, wg_ref, wu_ref, wd_ref,
                    o_hbm, h_vmem, acc_ref, xg0, xg1, y0, y1, sem):
    b = pl.program_id(0)
    last_step = pl.num_programs(0) - 1
    nblk = nblk_ref[0]
    tc = h_vmem.shape[0] - NULL_ROWS
    key_mod = 2 * tc * TOP_K

    def first_row(blk):
        return row0_ref[jnp.clip(blk, 0, nblk - 1)]

    def gather(blk, xg):
        row0 = first_row(blk)
        for r in range(MOE_BLOCK):
            t = (skey_ref[row0 + r] & (key_mod - 1)) >> TOP_K_SHIFT
            xg[r:r + 1, :] = h_vmem[pl.ds(t, 1), :]

    def scatter(blk, y):
        row0 = first_row(blk)
        for g0 in range(0, MOE_BLOCK, SCATTER_GROUP):
            updated = []
            for r in range(g0, g0 + SCATTER_GROUP):
                a = skey_ref[row0 + r] & (key_mod - 1)
                t = a >> TOP_K_SHIFT
                updated.append((t, acc_ref[pl.ds(t, 1), :] + gflat_ref[a] * y[r:r + 1, :]))
            for t, v in updated:
                acc_ref[pl.ds(t, 1), :] = v

    @pl.when(b == 0)
    def _():
        cp = pltpu.make_async_copy(h_hbm, h_vmem.at[pl.ds(0, tc)], sem)
        cp.start()
        h_vmem[pl.ds(tc, NULL_ROWS), :] = jnp.zeros((NULL_ROWS, h_vmem.shape[1]), F32)
        acc_ref[...] = jnp.zeros(acc_ref.shape, F32)
        y0[...] = jnp.zeros(y0.shape, F32)
        y1[...] = jnp.zeros(y1.shape, F32)
        cp.wait()
        gather(0, xg0)

    def step(xg_cur, xg_nxt, y_cur, y_prv):
        gather(b + 1, xg_nxt)
        xb = xg_cur[...].astype(BF16)
        gt = _dot(xb, wg_ref[0].astype(BF16))
        up = _dot(xb, wu_ref[0].astype(BF16))
        y_cur[...] = _dot((_silu(gt) * up).astype(BF16), wd_ref[0].astype(BF16))
        scatter(b - 1, y_prv)

    @pl.when(jnp.logical_and(b <= nblk, (b & 1) == 0))
    def _():
        step(xg0, xg1, y0, y1)

    @pl.when(jnp.logical_and(b <= nblk, (b & 1) == 1))
    def _():
        step(xg1, xg0, y1, y0)

    @pl.when(b == last_step)
    def _():
        cp = pltpu.make_async_copy(acc_ref.at[pl.ds(0, tc)], o_hbm, sem)
        cp.start()
        cp.wait()


def _routed_experts(h_chunk, blk_e, row0, nblk, skey, gflat, w_gate, w_up, w_down):
    tc, D = h_chunk.shape
    nb = blk_e.shape[0]
    ff = w_gate.shape[2]
    wmap = lambda b, be, r0, nbk, sk, gf: (be[jnp.minimum(b, nbk[0] - 1)], 0, 0)
    rows = pltpu.VMEM((MOE_BLOCK, D), F32)
    resident = pltpu.VMEM((tc + NULL_ROWS, D), F32)
    grid_spec = pltpu.PrefetchScalarGridSpec(
        num_scalar_prefetch=5, grid=(nb + 1,),
        in_specs=[pl.BlockSpec(memory_space=pl.ANY),
                  pl.BlockSpec((1, D, ff), wmap), pl.BlockSpec((1, D, ff), wmap), pl.BlockSpec((1, ff, D), wmap)],
        out_specs=pl.BlockSpec(memory_space=pl.ANY),
        scratch_shapes=[resident, resident, rows, rows, rows, rows, pltpu.SemaphoreType.DMA(())],
    )
    return pl.pallas_call(
        _experts_kernel, grid_spec=grid_spec,
        out_shape=jax.ShapeDtypeStruct((tc, D), F32),
        compiler_params=_cparams(("arbitrary",), vmem=56 * 1024 * 1024),
        name="moe_routed_experts",
    )(blk_e, row0, nblk, skey, gflat, h_chunk, w_gate, w_up, w_down)


def _dispatch_tables(eidx, gates, B, S):
    n_assign = S * TOP_K
    assert n_assign & (n_assign - 1) == 0
    key_mod = 2 * n_assign
    fill = MOE_BLOCK - 1
    e = eidx.reshape(TOP_K, B, S)
    t = lax.broadcasted_iota(I32, e.shape, 2)
    k = lax.broadcasted_iota(I32, e.shape, 0)
    real = (e * key_mod + t * TOP_K + k).transpose(1, 0, 2).reshape(B, n_assign)
    filler = jnp.repeat(jnp.arange(N_EXPERTS, dtype=I32) * key_mod + n_assign, fill)
    tail = jnp.full((MOE_BLOCK,), N_EXPERTS * key_mod + n_assign, I32)
    keys = jnp.concatenate([real, jnp.broadcast_to(jnp.concatenate([filler, tail]), (B, N_EXPERTS * fill + MOE_BLOCK))],
                           axis=1)
    skey = lax.sort(keys, dimension=1)
    bounds = jnp.arange(N_EXPERTS + 1, dtype=I32) * key_mod
    seg = jnp.sum((skey[:, None, :] < bounds[None, :, None]).astype(I32), axis=-1)
    counts = seg[:, 1:] - seg[:, :-1] - fill
    seg = seg[:, :-1]
    nblocks_e = (counts + MOE_BLOCK - 1) // MOE_BLOCK
    bend = jnp.cumsum(nblocks_e, axis=1)
    bstart = bend - nblocks_e
    nb = n_assign // MOE_BLOCK + N_EXPERTS
    bidx = jnp.arange(nb, dtype=I32)
    blk_e = jnp.minimum(jnp.sum((bend[:, None, :] <= bidx[None, :, None]).astype(I32), axis=-1), N_EXPERTS - 1)
    onehot = (blk_e[:, :, None] == jnp.arange(N_EXPERTS, dtype=I32)[None, None, :]).astype(I32)
    pick = lambda v: jnp.sum(onehot * v[:, None, :], axis=-1)
    row0 = pick(seg) + (bidx[None, :] - pick(bstart)) * MOE_BLOCK
    gflat = gates.reshape(TOP_K, B, S).transpose(1, 2, 0).reshape(B, n_assign)
    gflat = jnp.concatenate([gflat, jnp.zeros((B, TOP_K), F32)], axis=1)
    return blk_e, row0, bend[:, -1:], skey, gflat


def _shared_kernel(h_ref, wg_ref, wu_ref, wd_ref, routed_ref, x1_ref, gt_ref, o_ref):
    hb = h_ref[0].astype(BF16)
    act = _silu(_dot(hb, wg_ref[...])) * _dot(hb, wu_ref[...])
    shared = _dot(act.astype(BF16), wd_ref[...])
    o_ref[0] = x1_ref[0] + gt_ref[0] * (routed_ref[0] + shared)


def _shared_and_residual(h2, wsg, wsu, wsd, routed, x1, gt2, tm):
    B, S, D = h2.shape
    row = lambda b, i: (b, i, 0)
    xs = pl.BlockSpec((1, tm, D), row)
    const2 = lambda a: pl.BlockSpec(a.shape, lambda b, i: (0, 0))
    return pl.pallas_call(
        _shared_kernel, grid=(B, S // tm),
        in_specs=[xs, const2(wsg), const2(wsu), const2(wsd), xs, xs, pl.BlockSpec((1, 1, D), lambda b, i: (b, 0, 0))],
        out_specs=xs,
        out_shape=jax.ShapeDtypeStruct((B, S, D), F32),
        compiler_params=_cparams(("parallel", "parallel")),
        name="shared_expert_residual",
    )(h2, wsg, wsu, wsd, routed, x1, gt2)


def _prep_in_weights(w_in, g_q_dil, g_k_dil, g_q_nsa, g_k_slc, g_k_win):
    D = w_in.shape[0]
    pad = jnp.zeros((D, IN_COLS_PADDED - w_in.shape[1]), w_in.dtype)
    w_all = jnp.concatenate([w_in, pad], axis=1).astype(BF16)
    one = jnp.ones((LANES,), F32)
    gcol = jnp.concatenate([
        jnp.tile(g_q_dil, NH_DIL), jnp.tile(g_k_dil, NH_DIL), jnp.ones((512,), F32), jnp.tile(g_q_nsa, NH_NSA),
        one, one, jnp.tile(g_k_slc, NKV_NSA), one, jnp.tile(g_k_win, NKV_NSA), one, one]).reshape(1, IN_COLS_PADDED)
    return w_all, gcol


def _rope_tables(S):
    inv_freq = ROPE_THETA ** (-jnp.arange(ROT_HALF, dtype=F32) / ROT_HALF)
    ang = jnp.arange(S).astype(F32)[:, None] * inv_freq[None, :]
    cos, sin = jnp.cos(ang), jnp.sin(ang)
    zeros = jnp.zeros((S, HD - ROT_DIM), F32)
    z8 = jnp.zeros((S, ROT_HALF), F32)
    cos_h = jnp.concatenate([cos, cos, jnp.ones((S, HD - ROT_DIM), F32)], axis=1)
    sa_h = jnp.concatenate([-sin, z8, zeros], axis=1)
    sb_h = jnp.concatenate([z8, sin, zeros], axis=1)
    two = lambda t: jnp.concatenate([t, t], axis=1)
    return two(cos_h), two(sa_h), two(sb_h)


def _prep_compress(pe, w1, w2):
    eye = jnp.eye(NKV_NSA, dtype=F32)
    w1r = w1.reshape(CMP_LEN, HD, CMP_HIDDEN)

    def half(w1h, peh):
        w = jnp.einsum("ldh,gk->lgdkh", w1h, eye).reshape(CMP_STRIDE * LANES, NKV_NSA * CMP_HIDDEN)
        p = jnp.broadcast_to(peh[:, None, :], (CMP_STRIDE, NKV_NSA, HD)).reshape(1, CMP_STRIDE * LANES)
        return p, w.astype(BF16)

    pa, wa = half(w1r[:CMP_STRIDE], pe[:CMP_STRIDE])
    pb, wb = half(w1r[CMP_STRIDE:], pe[CMP_STRIDE:])
    w2bd = jnp.einsum("hd,gk->ghkd", w2, eye).reshape(NKV_NSA * CMP_HIDDEN, LANES).astype(BF16)
    return pa, pb, wa, wb, w2bd


def _overlap_tables(S):
    ncb = S // CMP_STRIDE
    n_sel = S // SEL_BLOCK
    cs = np.arange(ncb) * CMP_STRIDE
    ss = np.arange(n_sel) * SEL_BLOCK
    ov = np.clip(np.minimum(cs[:, None] + CMP_LEN, ss[None, :] + SEL_BLOCK) - np.maximum(cs[:, None], ss[None, :]), 0, None)
    ov = ov.astype(np.float32) / CMP_STRIDE
    ovl = np.zeros((ncb, LANES), np.float32)
    ovr = np.zeros((ncb, LANES), np.float32)
    ovl[:, :n_sel] = ov
    ovr[:, HD:HD + n_sel] = ov
    return jnp.asarray(ovl, BF16), jnp.asarray(ovr, BF16)


def _block_ones():
    r = np.arange(LANES)
    return jnp.asarray((r[:, None] // HD == r[None, :] // HD).astype(np.float32), BF16)


def _layer(x, c, w_ada, b_ada, g_norm_mix, g_norm_ffn, w_in, g_q_dil, g_k_dil, g_q_nsa, g_k_cmp, g_k_slc, g_k_win,
           cmp_pe_k, cmp_w1_k, cmp_w2_k, cmp_pe_v, cmp_w1_v, cmp_w2_v, w_out, w_router, router_bias,
           w_gate, w_up, w_down, ws_gate, ws_up, ws_down):
    B, S, D = x.shape
    ones_bd = _block_ones()

    c_pad = jnp.zeros((8, D), F32).at[:B].set(c)
    mod = _ada_mod(c_pad, w_ada, b_ada.reshape(1, -1))[:B]
    sh1, sc1, gt1, sh2, sc2, gt2 = [m.reshape(B, 1, D) for m in jnp.split(mod, 6, axis=-1)]

    w_all, gcol = _prep_in_weights(w_in, g_q_dil, g_k_dil, g_q_nsa, g_k_slc, g_k_win)
    cos_t, sa_t, sb_t = _rope_tables(S)
    qa, ka, va, qn, qr, kc, vc, ksl, vsl, kw, vw, gate = _in_projection(
        x, g_norm_mix.reshape(1, D), sc1, sh1, w_all, gcol, cos_t, sa_t, sb_t, ones_bd, tm=256)

    mha = dict(kv_tile=lambda h: h // 2, kv_half=lambda h: h % 2)
    modes = ("first", "mid", "last")
    state = ()
    for (window, dil), mode in zip(DIL_PAIRS, modes):
        res = _band_attention(qa, ka, va, dil=dil, tq=128, n_prev=1, max_dist=window // dil, mode=mode,
                              extra=state, out_dtype=BF16, **mha)
        state = res if mode != "last" else ()
        o_a = res

    kcmp, vcmp = _compress(kc, vc, _prep_compress(cmp_pe_k, cmp_w1_k, cmp_w2_k),
                           _prep_compress(cmp_pe_v, cmp_w1_v, cmp_w2_v), jnp.tile(g_k_cmp, NKV_NSA).reshape(1, LANES),
                           ones_bd)
    ovl, ovr = _overlap_tables(S)
    o_cmp, sel = _cmp_select(qn, kcmp, vcmp, ovl, ovr, gate, tq=128)
    o_cs = _selected_attention(qr, ksl, vsl, sel, o_cmp, gate, tq=256, tk=512)
    gqa = dict(kv_tile=lambda h: 0, kv_half=lambda h: h // NSA_REP)
    o_b = _band_attention(qr, kw, vw, dil=1, tq=256, n_prev=2, max_dist=NSA_WINDOW - 1, mode="gated",
                          extra=(o_cs, gate), out_dtype=BF16, **gqa)

    w_out_b = w_out.astype(BF16)
    x1, h2 = _out_projection(o_a, o_b, w_out_b[:NH_DIL * HD], w_out_b[NH_DIL * HD:], x, gt1,
                             g_norm_ffn.reshape(1, D), sc2, sh2, tm=256)

    tm_r = 256
    eidx, gates = _router(h2.reshape(B * S, D), w_router.T.astype(BF16),
                          jnp.broadcast_to(router_bias.reshape(N_EXPERTS, 1), (N_EXPERTS, tm_r)), tm=tm_r)
    tables = _dispatch_tables(eidx, gates, B, S)
    routed = jnp.stack([_routed_experts(h2[b], *[tbl[b] for tbl in tables], w_gate, w_up, w_down) for b in range(B)],
                       axis=0)
    return _shared_and_residual(h2, ws_gate.astype(BF16), ws_up.astype(BF16), ws_down.astype(BF16), routed, x1, gt2,
                                tm=256)


def kernel(x, c, w_ada, b_ada, g_norm_mix, g_norm_ffn, w_in, g_q_dil, g_k_dil, g_q_nsa, g_k_cmp, g_k_slc, g_k_win, cmp_pe_k, cmp_w1_k, cmp_w2_k, cmp_pe_v, cmp_w1_v, cmp_w2_v, w_out, w_router, router_bias, w_gate, w_up, w_down, ws_gate, ws_up, ws_down):
    params = (w_ada, b_ada, g_norm_mix, g_norm_ffn, w_in, g_q_dil, g_k_dil, g_q_nsa, g_k_cmp, g_k_slc, g_k_win,
              cmp_pe_k, cmp_w1_k, cmp_w2_k, cmp_pe_v, cmp_w1_v, cmp_w2_v, w_out, w_router, router_bias,
              w_gate, w_up, w_down, ws_gate, ws_up, ws_down)
    for layer in range(w_ada.shape[0]):
        x = _layer(x, c, *[a[layer] for a in params])
    return x
```

```python
import functools
import math

import numpy as np
import jax
import jax.numpy as jnp
from jax import lax
from jax.experimental import pallas as pl
from jax.experimental.pallas import tpu as pltpu

F32 = jnp.float32
BF16 = jnp.bfloat16
I32 = jnp.int32

HD = 64
LANES = 128
NH_DIL = 8
NH_NSA = 8
NKV_NSA = 2
NSA_REP = NH_NSA // NKV_NSA
DIL_PAIRS = ((128, 1), (512, 4), (2048, 16))
DIL_TILE = 128
DIL_UNROLL = 8
ROPE_THETA = 500000.0
ROT_DIM = HD // 4
ROT_HALF = ROT_DIM // 2
CMP_LEN = 32
CMP_STRIDE = 16
CMP_HIDDEN = 128
SEL_BLOCK = 64
SEL_SHIFT = 6
SEL_TOP = 16
NSA_WINDOW = 512
N_EXPERTS = 256
TOP_K = 8
TOP_K_SHIFT = 3
N_GROUPS = 8
GROUP_SIZE = N_EXPERTS // N_GROUPS
TOPK_GROUPS = 4
EXPERT_FF = 256
SHARED_FF = 256
ROUTED_SCALE = 2.5
EPS = 1e-6
QK_SCALE = 1.0 / math.sqrt(HD)

NEG = -1e30
BIG = 3e38

C_QA, C_KA, C_VA, C_QB = 0, 512, 1024, 1536
C_KC, C_VC, C_KSL, C_VSL, C_KW, C_VW, C_GB = 2048, 2176, 2304, 2432, 2560, 2688, 2816
IN_COLS_PADDED = 2944

MOE_BLOCK = 128
NULL_ROWS = 8
SCATTER_GROUP = 16
VMEM_LIMIT = 48 * 1024 * 1024


def _cparams(sem, vmem=VMEM_LIMIT):
    return pltpu.CompilerParams(dimension_semantics=sem, vmem_limit_bytes=vmem)


def _sigmoid(v):
    return 1.0 / (1.0 + jnp.exp(-v))


def _silu(v):
    return v * _sigmoid(v)


def _dot(a, b):
    return jnp.dot(a, b, preferred_element_type=F32)


def _dot_nt(a, b):
    return lax.dot_general(a, b, (((1,), (1,)), ((), ())), preferred_element_type=F32)


def _lane_lo(shape):
    return lax.broadcasted_iota(I32, shape, len(shape) - 1) < HD


def _head_sums(v, ones_bd):
    hi = v.astype(BF16)
    lo = (v - hi.astype(F32)).astype(BF16)
    return _dot(hi, ones_bd) + _dot(lo, ones_bd)


def _head_rmsnorm(y, ones_bd, gain):
    ms = _head_sums(y * y, ones_bd) * (1.0 / HD)
    return y * lax.rsqrt(ms + EPS) * gain


def _rope(y, cos, sin_a, sin_b):
    return y * cos + pltpu.roll(y, LANES - ROT_HALF, 1) * sin_a + pltpu.roll(y, ROT_HALF, 1) * sin_b


def _ada_kernel(c_ref, w_ref, b_ref, o_ref):
    a = _silu(c_ref[...]).astype(BF16)
    o_ref[...] = _dot(a, w_ref[...].astype(BF16)) + b_ref[...]


def _ada_mod(c_pad, w_ada, b_ada):
    rows, d = c_pad.shape
    n = w_ada.shape[1]
    tn = 1536 if n % 1536 == 0 else n
    return pl.pallas_call(
        _ada_kernel,
        grid=(n // tn,),
        in_specs=[pl.BlockSpec((rows, d), lambda j: (0, 0)),
                  pl.BlockSpec((d, tn), lambda j: (0, j)),
                  pl.BlockSpec((1, tn), lambda j: (0, j))],
        out_specs=pl.BlockSpec((rows, tn), lambda j: (0, j)),
        out_shape=jax.ShapeDtypeStruct((rows, n), F32),
        compiler_params=_cparams(("arbitrary",)),
        name="ada_mod",
    )(c_pad, w_ada, b_ada)


def _inproj_kernel(x_ref, g_ref, sc_ref, sh_ref, w_ref, gcol_ref, cos_ref, sa_ref, sb_ref, ones_ref,
                   qa_ref, ka_ref, va_ref, qn_ref, qr_ref, kc_ref, vc_ref, ksl_ref, vsl_ref, kw_ref, vw_ref,
                   gate_ref):
    x = x_ref[0]
    ms = jnp.mean(x * x, axis=-1, keepdims=True)
    h = (x * lax.rsqrt(ms + EPS)) * g_ref[...]
    h = h * (1.0 + sc_ref[0]) + sh_ref[0]
    hb = h.astype(BF16)
    ones_bd = ones_ref[...]
    cos, sa, sb = cos_ref[...], sa_ref[...], sb_ref[...]
    lo = _lane_lo(cos.shape)

    def proj(c0, width):
        return _dot(hb, w_ref[:, c0:c0 + width])

    def normed(tile, c0):
        return _head_rmsnorm(tile, ones_bd, gcol_ref[:, c0:c0 + LANES])

    acc = proj(C_QA, 512)
    for p in range(4):
        y = _rope(normed(acc[:, p * LANES:(p + 1) * LANES], C_QA + p * LANES), cos, sa, sb) * QK_SCALE
        qa_ref[0, :, p * LANES:(p + 1) * LANES] = y
    acc = proj(C_KA, 512)
    for p in range(4):
        y = _rope(normed(acc[:, p * LANES:(p + 1) * LANES], C_KA + p * LANES), cos, sa, sb)
        ka_ref[0, :, p * LANES:(p + 1) * LANES] = y
    va_ref[0] = proj(C_VA, 512)

    acc = proj(C_QB, 512)
    for p in range(4):
        yn = normed(acc[:, p * LANES:(p + 1) * LANES], C_QB + p * LANES)
        yr = _rope(yn, cos, sa, sb)
        for y, ref in ((yn * QK_SCALE, qn_ref), (yr * QK_SCALE, qr_ref)):
            ysw = pltpu.roll(y, HD, 1)
            for half in range(2):
                head = 2 * p + half
                grp = head // NSA_REP
                src = y if grp == half else ysw
                keep = lo if grp == 0 else jnp.logical_not(lo)
                ref[0, :, head * LANES:(head + 1) * LANES] = jnp.where(keep, src, 0.0).astype(BF16)

    acc = proj(C_KC, 512)
    kc_ref[0] = acc[:, 0:LANES]
    vc_ref[0] = acc[:, LANES:2 * LANES]
    ksl_ref[0] = _rope(normed(acc[:, 2 * LANES:3 * LANES], C_KSL), cos, sa, sb).astype(BF16)
    vsl_ref[0] = acc[:, 3 * LANES:4 * LANES].astype(BF16)
    acc = proj(C_KW, 384)
    kw_ref[0] = _rope(normed(acc[:, 0:LANES], C_KW), cos, sa, sb).astype(BF16)
    vw_ref[0] = acc[:, LANES:2 * LANES].astype(BF16)
    gate_ref[0] = _sigmoid(acc[:, 2 * LANES:3 * LANES])


def _in_projection(x, g_mix, sc1, sh1, w_all, gcol, cos_t, sa_t, sb_t, ones_bd, tm):
    B, S, D = x.shape
    nc = w_all.shape[1]
    row = lambda b, i: (b, i, 0)
    bvec = pl.BlockSpec((1, 1, D), lambda b, i: (b, 0, 0))
    tab = pl.BlockSpec((tm, LANES), lambda b, i: (i, 0))
    const2 = lambda shape: pl.BlockSpec(shape, lambda b, i: (0, 0))
    widths_dt = [(512, F32), (512, F32), (512, F32), (1024, BF16), (1024, BF16), (LANES, F32), (LANES, F32),
                 (LANES, BF16), (LANES, BF16), (LANES, BF16), (LANES, BF16), (LANES, F32)]
    return pl.pallas_call(
        _inproj_kernel,
        grid=(B, S // tm),
        in_specs=[pl.BlockSpec((1, tm, D), row), const2((1, D)), bvec, bvec, const2((D, nc)), const2((1, nc)),
                  tab, tab, tab, const2((LANES, LANES))],
        out_specs=[pl.BlockSpec((1, tm, w), row) for w, _ in widths_dt],
        out_shape=[jax.ShapeDtypeStruct((B, S, w), dt) for w, dt in widths_dt],
        compiler_params=_cparams(("parallel", "parallel")),
        name="in_projection",
    )(x, g_mix, sc1, sh1, w_all, gcol, cos_t, sa_t, sb_t, ones_bd)


def _pair_tile(lo, t0, half0, t1, half1):
    a = t0 if half0 == 0 else pltpu.roll(t0, HD, 1)
    b = t1 if half1 == 1 else pltpu.roll(t1, HD, 1)
    return jnp.where(lo, a, b)


def _band_kernel(*refs, tq, n_prev, max_dist, n_heads, kv_tile, kv_half, mode):
    nk = n_prev + 1
    q_ref = refs[0]
    k_refs = refs[1:1 + nk]
    v_refs = refs[1 + nk:1 + 2 * nk]
    rest = refs[1 + 2 * nk:]
    i = pl.program_id(2)

    row = lax.broadcasted_iota(I32, (tq, tq), 0)
    col = lax.broadcasted_iota(I32, (tq, tq), 1)
    diff = row - col
    biases = []
    for j in range(nk):
        d = diff + (n_prev - j) * tq
        ok = jnp.where(d >= 0, jnp.where(d <= max_dist, 0.0, NEG), NEG)
        biases.append(jnp.where(i >= n_prev - j, ok, NEG))
    lo = _lane_lo((tq, LANES))

    outs, lses = [], []
    for h in range(n_heads):
        qh = q_ref[0, :, h * LANES:(h + 1) * LANES]
        kt = kv_tile(h)
        s = [_dot_nt(qh, k_refs[j][0, :, kt * LANES:(kt + 1) * LANES]) + biases[j] for j in range(nk)]
        m = s[0].max(axis=1, keepdims=True)
        for j in range(1, nk):
            m = jnp.maximum(m, s[j].max(axis=1, keepdims=True))
        p = [jnp.exp(sj - m) for sj in s]
        l = p[0].sum(axis=1, keepdims=True)
        for j in range(1, nk):
            l = l + p[j].sum(axis=1, keepdims=True)
        o = _dot(p[0].astype(BF16), v_refs[0][0, :, kt * LANES:(kt + 1) * LANES])
        for j in range(1, nk):
            o = o + _dot(p[j].astype(BF16), v_refs[j][0, :, kt * LANES:(kt + 1) * LANES])
        outs.append(o / l)
        lses.append(m + jnp.log(l))

    for p_ in range(n_heads // 2):
        h0, h1 = 2 * p_, 2 * p_ + 1
        sl = slice(p_ * LANES, (p_ + 1) * LANES)
        o_pair = _pair_tile(lo, outs[h0], kv_half(h0), outs[h1], kv_half(h1))
        if mode == "gated":
            prev_ref, gate_ref, out_ref = rest
            c0, c1 = 3 * h0 + 2, 3 * h1 + 2
            g = jnp.where(lo, gate_ref[0, :, c0:c0 + 1], gate_ref[0, :, c1:c1 + 1])
            out_ref[0, :, sl] = (prev_ref[0, :, sl] + g * o_pair).astype(out_ref.dtype)
            continue
        lse_pair = jnp.where(lo, lses[h0], lses[h1])
        if mode == "first":
            o_out, lse_out = rest
            o_out[0, :, sl] = o_pair
            lse_out[0, :, sl] = lse_pair
            continue
        po_ref, pl_ref = rest[0], rest[1]
        lp = pl_ref[0, :, sl]
        mx = jnp.maximum(lp, lse_pair)
        wp = jnp.exp(lp - mx)
        wn = jnp.exp(lse_pair - mx)
        den = wp + wn
        merged = (po_ref[0, :, sl] * wp + o_pair * wn) / den
        if mode == "mid":
            rest[2][0, :, sl] = merged
            rest[3][0, :, sl] = mx + jnp.log(den)
        else:
            rest[2][0, :, sl] = merged.astype(rest[2].dtype)


def _band_attention(q, k, v, *, dil, tq, n_prev, max_dist, kv_tile, kv_half, mode, extra=(), out_dtype=F32):
    B, S, qw = q.shape
    kvw = k.shape[2]
    L = S // dil
    nk = n_prev + 1
    ow = NH_DIL * HD
    view = lambda a: a.reshape(B, L, dil * a.shape[2])
    cur = lambda b, r, i: (b, i, r)

    def prev_map(j):
        return lambda b, r, i: (b, jnp.maximum(i - (n_prev - j), 0), r)

    in_specs = [pl.BlockSpec((1, tq, qw), cur)]
    in_specs += [pl.BlockSpec((1, tq, kvw), prev_map(j)) for j in range(nk)] * 2
    args = [view(q)] + [view(k)] * nk + [view(v)] * nk
    for a in extra:
        in_specs.append(pl.BlockSpec((1, tq, a.shape[2]), cur))
        args.append(view(a))
    o_spec = pl.BlockSpec((1, tq, ow), cur)
    if mode in ("first", "mid"):
        out_specs = [o_spec, o_spec]
        out_shape = [jax.ShapeDtypeStruct((B, L, dil * ow), F32)] * 2
    else:
        out_specs = o_spec
        out_shape = jax.ShapeDtypeStruct((B, L, dil * ow), out_dtype)
    kern = functools.partial(_band_kernel, tq=tq, n_prev=n_prev, max_dist=max_dist, n_heads=NH_DIL,
                             kv_tile=kv_tile, kv_half=kv_half, mode=mode)
    res = pl.pallas_call(
        kern, grid=(B, dil, L // tq), in_specs=in_specs, out_specs=out_specs, out_shape=out_shape,
        compiler_params=_cparams(("parallel", "parallel", "parallel")),
        name=f"band_attention_d{dil}_{mode}",
    )(*args)
    if mode in ("first", "mid"):
        return res[0].reshape(B, S, ow), res[1].reshape(B, S, ow)
    return res.reshape(B, S, ow)


def _dot_tn(a, b):
    return lax.dot_general(a, b, (((0,), (0,)), ((), ())), preferred_element_type=F32)


def _dilated_kernel(q_ref, k_ref, v_ref, o_ref, oacc, lacc, *, sb_rows, pairs):
    sb = pl.program_id(2)
    base = sb * sb_rows
    t = DIL_TILE
    kk = lax.broadcasted_iota(I32, (t, t), 0)
    qq = lax.broadcasted_iota(I32, (t, t), 1)
    bias_cur = jnp.where(qq >= kk, 0.0, NEG)
    bias_prev = jnp.where(qq <= kk, 0.0, NEG)
    lane_lo = _lane_lo((t, LANES))
    row_lo = lax.broadcasted_iota(I32, (LANES, t), 0) < HD

    def block(d, qs, first, mode):
        qg = base + qs
        rows_q = pl.ds(qs, t, stride=d) if d > 1 else pl.ds(qs, t)

        def strided(ref, start):
            return ref[0, pl.ds(start, t, stride=d), :] if d > 1 else ref[0, pl.ds(start, t), :]

        prev_start = jnp.where(first, qg, qg - t * d)
        qt = q_ref[0, rows_q, :].astype(BF16)
        zero = jnp.zeros_like(qt)
        q2 = jnp.concatenate([jnp.where(lane_lo, qt, zero), jnp.where(lane_lo, zero, qt)], axis=0)
        k2 = jnp.concatenate([strided(k_ref, prev_start), strided(k_ref, qg)], axis=0).astype(BF16)
        v2 = jnp.concatenate([strided(v_ref, prev_start), strided(v_ref, qg)], axis=0).astype(BF16)
        bias = jnp.concatenate([jnp.where(first, NEG, bias_prev), bias_cur], axis=0)
        s = _dot_nt(k2, q2) + jnp.concatenate([bias, bias], axis=1)
        m = s.max(axis=0, keepdims=True)
        p = jnp.exp(s - m)
        l = p.sum(axis=0, keepdims=True)
        ov = _dot_tn(v2, p.astype(BF16)) / l
        lse2 = m + jnp.log(l)
        o = jnp.where(row_lo, ov[:, :t], ov[:, t:]).T
        lse = jnp.where(row_lo, lse2[:, :t], lse2[:, t:]).T
        if mode == "init":
            oacc[rows_q, :] = o
            lacc[rows_q, :] = lse
            return
        lp = lacc[rows_q, :]
        mx = jnp.maximum(lp, lse)
        wp, wn = jnp.exp(lp - mx), jnp.exp(lse - mx)
        den = wp + wn
        oacc[rows_q, :] = (oacc[rows_q, :] * wp + o * wn) / den
        if mode == "mid":
            lacc[rows_q, :] = mx + jnp.log(den)

    modes = ("init",) + ("mid",) * (len(pairs) - 2) + ("last",)
    for (window, d), mode in zip(pairs, modes):
        assert window // d == t and d & (d - 1) == 0
        n_j = sb_rows // (t * d)

        def body(i, c, d=d, mode=mode):
            r, j = i & (d - 1), i >> (d.bit_length() - 1)
            block(d, j * (t * d) + r, jnp.logical_and(sb == 0, j == 0), mode)
            return c

        lax.fori_loop(0, d * n_j, body, 0, unroll=DIL_UNROLL)
    o_ref[0] = oacc[...].astype(o_ref.dtype)


def _dilated_attention(qa, ka, va):
    B, S, w = qa.shape
    sb_rows = DIL_TILE * max(d for _, d in DIL_PAIRS)
    assert S % sb_rows == 0
    kern = functools.partial(_dilated_kernel, sb_rows=sb_rows, pairs=DIL_PAIRS)
    whole = pl.BlockSpec((1, S, LANES), lambda b, p, s: (b, 0, p))
    blk = pl.BlockSpec((1, sb_rows, LANES), lambda b, p, s: (b, s, p))
    acc = pltpu.VMEM((sb_rows, LANES), F32)
    return pl.pallas_call(
        kern, grid=(B, w // LANES, S // sb_rows),
        in_specs=[blk, whole, whole], out_specs=blk,
        out_shape=jax.ShapeDtypeStruct((B, S, w), BF16),
        scratch_shapes=[acc, acc],
        compiler_params=_cparams(("parallel", "parallel", "arbitrary")),
        name="dilated_attention",
    )(qa, ka, va)


def _compress_kernel(kc_ref, vc_ref, pak_ref, pbk_ref, wak_ref, wbk_ref, w2k_ref, gk_ref, ones_ref,
                     pav_ref, pbv_ref, wav_ref, wbv_ref, w2v_ref, ko_ref, vo_ref):
    ncb = kc_ref.shape[1]

    def branch(t_ref, pa, pb, wa, wb, w2):
        t = t_ref[0]
        ua = _dot((t + pa[...]).astype(BF16), wa[...])
        ub = _dot((t + pb[...]).astype(BF16), wb[...])
        pre = ua + pltpu.roll(ub, ncb - 1, 0)
        return _dot(jax.nn.gelu(pre).astype(BF16), w2[...])

    live = lax.broadcasted_iota(I32, (ncb, LANES), 0) < ncb - 1
    kcmp = _head_rmsnorm(branch(kc_ref, pak_ref, pbk_ref, wak_ref, wbk_ref, w2k_ref), ones_ref[...], gk_ref[...])
    ko_ref[0] = jnp.where(live, kcmp, 0.0).astype(BF16)
    vo_ref[0] = jnp.where(live, branch(vc_ref, pav_ref, pbv_ref, wav_ref, wbv_ref, w2v_ref), 0.0).astype(BF16)


def _compress(kc, vc, wk, wv, gk2, ones_bd):
    B, S, _ = kc.shape
    ncb = S // CMP_STRIDE
    cw = CMP_STRIDE * LANES
    view = lambda a: a.reshape(B, ncb, cw)
    blk = pl.BlockSpec((1, ncb, cw), lambda b: (b, 0, 0))
    full = lambda a: pl.BlockSpec(a.shape, lambda b: (0,) * a.ndim)
    consts = [*wk, gk2, ones_bd, *wv]
    o_spec = pl.BlockSpec((1, ncb, LANES), lambda b: (b, 0, 0))
    return pl.pallas_call(
        _compress_kernel, grid=(B,),
        in_specs=[blk, blk] + [full(a) for a in consts],
        out_specs=[o_spec, o_spec],
        out_shape=[jax.ShapeDtypeStruct((B, ncb, LANES), BF16)] * 2,
        compiler_params=_cparams(("parallel",)),
        name="nsa_compress",
    )(view(kc), view(vc), *consts)


def _cmpsel_kernel(qn_ref, kc_ref, vc_ref, ovl_ref, ovr_ref, gate_ref, o_ref, sel_ref, *, tq, n_sel, top):
    ncb = kc_ref.shape[1]
    i = pl.program_id(1)
    pos0 = i * tq
    qpos = pos0 + lax.broadcasted_iota(I32, (tq, ncb), 0)
    cend = lax.broadcasted_iota(I32, (tq, ncb), 1) * CMP_STRIDE + (CMP_LEN - 1)
    valid = cend <= qpos
    lo = _lane_lo((tq, LANES))
    kc = kc_ref[0]
    vc = vc_ref[0]

    outs = []
    psum = [None] * NKV_NSA
    for h in range(NH_NSA):
        g = h // NSA_REP
        s = _dot_nt(qn_ref[0, :, h * LANES:(h + 1) * LANES], kc)
        m = jnp.where(valid, s, NEG).max(axis=1, keepdims=True)
        m = jnp.where(m > 0.5 * NEG, m, 0.0)
        p = jnp.where(valid, jnp.exp(s - m), 0.0)
        l = p.sum(axis=1, keepdims=True)
        p = p / jnp.where(l > 0.0, l, 1.0)
        outs.append(_dot(p.astype(BF16), vc))
        psum[g] = p if psum[g] is None else psum[g] + p

    for p_ in range(NH_NSA // 2):
        h0, h1 = 2 * p_, 2 * p_ + 1
        o_pair = _pair_tile(lo, outs[h0], h0 // NSA_REP, outs[h1], h1 // NSA_REP)
        g = jnp.where(lo, gate_ref[0, :, 3 * h0:3 * h0 + 1], gate_ref[0, :, 3 * h1:3 * h1 + 1])
        o_ref[0, :, p_ * LANES:(p_ + 1) * LANES] = g * o_pair

    def hilo_dot(a, w):
        hi = a.astype(BF16)
        return _dot(hi, w) + _dot((a - hi.astype(F32)).astype(BF16), w)

    pslc = hilo_dot(psum[0], ovl_ref[...]) + hilo_dot(psum[1], ovr_ref[...])
    pslc_t = pslc.T
    jblk = lax.broadcasted_iota(I32, (HD, tq), 0)
    cur = (pos0 + lax.broadcasted_iota(I32, (HD, tq), 1)) >> SEL_SHIFT
    forced = (jblk == 0) | (jblk == cur) | (jblk == cur - 1)
    sel_t = []
    for g in range(NKV_NSA):
        sc = jnp.where(forced, BIG, jnp.where(jblk <= cur, pslc_t[g * HD:(g + 1) * HD, :], -BIG))
        rank = jnp.zeros((HD, tq), F32)
        for k in range(n_sel):
            rk = sc[k:k + 1, :]
            rank = rank + jnp.where(jblk > k, jnp.where(rk >= sc, 1.0, 0.0), jnp.where(rk > sc, 1.0, 0.0))
        sel_t.append(jnp.where(rank < top, 1.0, 0.0))
    sel_ref[0] = jnp.concatenate(sel_t, axis=0).T.astype(BF16)


def _cmp_select(qn, kcmp, vcmp, ovl, ovr, gate, tq):
    B, S, qw = qn.shape
    ncb = kcmp.shape[1]
    n_sel = S // SEL_BLOCK
    assert n_sel <= HD and tq == LANES
    row = lambda b, i: (b, i, 0)
    cb = pl.BlockSpec((1, ncb, LANES), lambda b, i: (b, 0, 0))
    ovs = pl.BlockSpec((ncb, LANES), lambda b, i: (0, 0))
    kern = functools.partial(_cmpsel_kernel, tq=tq, n_sel=n_sel, top=min(SEL_TOP, n_sel))
    return pl.pallas_call(
        kern, grid=(B, S // tq),
        in_specs=[pl.BlockSpec((1, tq, qw), row), cb, cb, ovs, ovs, pl.BlockSpec((1, tq, LANES), row)],
        out_specs=[pl.BlockSpec((1, tq, NH_NSA * HD), row), pl.BlockSpec((1, tq, LANES), row)],
        out_shape=[jax.ShapeDtypeStruct((B, S, NH_NSA * HD), F32), jax.ShapeDtypeStruct((B, S, LANES), BF16)],
        compiler_params=_cparams(("parallel", "parallel")),
        name="nsa_cmp_select",
    )(qn, kcmp, vcmp, ovl, ovr, gate)


def _slc_kernel(q_ref, k_ref, v_ref, sel_ref, prev_ref, gate_ref, o_ref, m_ref, l_ref, acc_ref, *, tq, tk):
    i = pl.program_id(1)
    kk = pl.program_id(2)
    nkv = pl.num_programs(2)
    last = (i * tq + tq - 1) // tk

    @pl.when(kk == 0)
    def _():
        m_ref[...] = jnp.full(m_ref.shape, NEG, F32)
        l_ref[...] = jnp.zeros(l_ref.shape, F32)
        acc_ref[...] = jnp.zeros(acc_ref.shape, F32)

    @pl.when(kk <= last)
    def _():
        k0 = kk * tk
        qpos = i * tq + lax.broadcasted_iota(I32, (tq, tk), 0)
        kpos = k0 + lax.broadcasted_iota(I32, (tq, tk), 1)
        causal = kpos <= qpos
        erow = lax.broadcasted_iota(I32, (LANES, tk), 0)
        eblk = (k0 + lax.broadcasted_iota(I32, (LANES, tk), 1)) >> SEL_SHIFT
        sel = sel_ref[0]
        k = k_ref[0]
        v = v_ref[0]
        for g in range(NKV_NSA):
            expand = jnp.where(erow == eblk + g * HD, 1.0, 0.0).astype(BF16)
            chosen = _dot(sel, expand) > 0.5
            bias = jnp.where(causal, jnp.where(chosen, 0.0, NEG), NEG)
            for r in range(NSA_REP):
                h = g * NSA_REP + r
                s = _dot_nt(q_ref[0, :, h * LANES:(h + 1) * LANES], k) + bias
                m_prev = m_ref[h]
                m_new = jnp.maximum(m_prev, s.max(axis=1, keepdims=True))
                alpha = jnp.exp(m_prev - m_new)
                p = jnp.exp(s - m_new[:, 0:1])
                l_ref[h] = alpha * l_ref[h] + p.sum(axis=1, keepdims=True)
                acc_ref[h] = alpha * acc_ref[h] + _dot(p.astype(BF16), v)
                m_ref[h] = m_new

    @pl.when(kk == nkv - 1)
    def _():
        lo = _lane_lo((tq, LANES))
        for p_ in range(NH_NSA // 2):
            h0, h1 = 2 * p_, 2 * p_ + 1
            o_pair = _pair_tile(lo, acc_ref[h0] / l_ref[h0], h0 // NSA_REP, acc_ref[h1] / l_ref[h1], h1 // NSA_REP)
            c0, c1 = 3 * h0 + 1, 3 * h1 + 1
            g = jnp.where(lo, gate_ref[0, :, c0:c0 + 1], gate_ref[0, :, c1:c1 + 1])
            sl = slice(p_ * LANES, (p_ + 1) * LANES)
            o_ref[0, :, sl] = prev_ref[0, :, sl] + g * o_pair


def _selected_attention(qr, ksl, vsl, sel, prev, gate, tq, tk):
    B, S, qw = qr.shape
    row = lambda b, i, kk: (b, i, 0)
    kv = lambda b, i, kk: (b, jnp.minimum(kk, (i * tq + tq - 1) // tk), 0)
    ow = NH_NSA * HD
    kern = functools.partial(_slc_kernel, tq=tq, tk=tk)
    stat = pltpu.VMEM((NH_NSA, tq, LANES), F32)
    return pl.pallas_call(
        kern, grid=(B, S // tq, S // tk),
        in_specs=[pl.BlockSpec((1, tq, qw), row), pl.BlockSpec((1, tk, LANES), kv), pl.BlockSpec((1, tk, LANES), kv),
                  pl.BlockSpec((1, tq, LANES), row), pl.BlockSpec((1, tq, ow), row), pl.BlockSpec((1, tq, LANES), row)],
        out_specs=pl.BlockSpec((1, tq, ow), row),
        out_shape=jax.ShapeDtypeStruct((B, S, ow), F32),
        scratch_shapes=[stat, stat, stat],
        compiler_params=_cparams(("parallel", "parallel", "arbitrary")),
        name="nsa_selected_attention",
    )(qr, ksl, vsl, sel, prev, gate)


def _to_token_tiles(ref, value):
    m, d = value.shape
    rpt = d // LANES
    for j in range(rpt):
        ref[0, pl.ds(j, m, stride=rpt), :] = value[:, j * LANES:(j + 1) * LANES]


def _from_token_tiles(ref, m):
    rpt = ref.shape[1] // m
    return jnp.concatenate([ref[0, pl.ds(j, m, stride=rpt), :] for j in range(rpt)], axis=1)


def _outproj_kernel(oa_ref, ob_ref, wt_ref, wb_ref, x_ref, gt_ref, g_ref, sc_ref, sh_ref, x1_ref, h2_ref, h2t_ref):
    mix = _dot(oa_ref[0], wt_ref[...]) + _dot(ob_ref[0], wb_ref[...])
    x1 = x_ref[0] + gt_ref[0] * mix
    x1_ref[0] = x1
    ms = jnp.mean(x1 * x1, axis=-1, keepdims=True)
    h = (x1 * lax.rsqrt(ms + EPS)) * g_ref[...]
    h2 = h * (1.0 + sc_ref[0]) + sh_ref[0]
    h2_ref[0] = h2
    _to_token_tiles(h2t_ref, h2)


def _out_projection(oa, ob, w_top, w_bot, x, gt1, g_ffn, sc2, sh2, tm):
    B, S, D = x.shape
    row = lambda b, i: (b, i, 0)
    bvec = pl.BlockSpec((1, 1, D), lambda b, i: (b, 0, 0))
    const2 = lambda a: pl.BlockSpec(a.shape, lambda b, i: (0, 0))
    xs = pl.BlockSpec((1, tm, D), row)
    return pl.pallas_call(
        _outproj_kernel, grid=(B, S // tm),
        in_specs=[pl.BlockSpec((1, tm, oa.shape[2]), row), pl.BlockSpec((1, tm, ob.shape[2]), row),
                  const2(w_top), const2(w_bot), xs, bvec, const2(g_ffn), bvec, bvec],
        out_specs=[xs, xs, pl.BlockSpec((1, tm * (D // LANES), LANES), row)],
        out_shape=[jax.ShapeDtypeStruct((B, S, D), F32)] * 2 + [jax.ShapeDtypeStruct((B, S * (D // LANES), LANES), F32)],
        compiler_params=_cparams(("parallel", "parallel")),
        name="out_projection",
    )(oa, ob, w_top, w_bot, x, gt1, g_ffn, sc2, sh2)


def _router_kernel(h_ref, wr_ref, bias_ref, eidx_ref, gate_ref, cnt_ref):
    tm = h_ref.shape[1]
    logits = _dot_nt(wr_ref[...], h_ref[0].astype(BF16))
    aff = _sigmoid(logits)
    biased = aff + bias_ref[...]
    b3 = biased.reshape(N_GROUPS, GROUP_SIZE, tm)
    jj = lax.broadcasted_iota(I32, (N_GROUPS, GROUP_SIZE, tm), 1).astype(F32)
    m1 = b3.max(axis=1, keepdims=True)
    i1 = jnp.where(b3 == m1, jj, 1e9).min(axis=1, keepdims=True)
    m2 = jnp.where(jj == i1, -jnp.inf, b3).max(axis=1, keepdims=True)
    gs = (m1 + m2).reshape(N_GROUPS, tm)
    gi = lax.broadcasted_iota(I32, (N_GROUPS, tm), 0)
    rank = jnp.zeros((N_GROUPS, tm), F32)
    for k in range(N_GROUPS):
        rk = gs[k:k + 1, :]
        rank = rank + jnp.where(gi > k, jnp.where(rk >= gs, 1.0, 0.0), jnp.where(rk > gs, 1.0, 0.0))
    gsel = (rank < TOPK_GROUPS).reshape(N_GROUPS, 1, tm)
    masked = jnp.where(gsel, b3, -jnp.inf).reshape(N_EXPERTS, tm)
    ee = lax.broadcasted_iota(I32, (N_EXPERTS, tm), 0).astype(F32)
    idx_rows, sel_rows = [], []
    chosen = jnp.zeros((N_EXPERTS, tm), F32)
    for _ in range(TOP_K):
        m = masked.max(axis=0, keepdims=True)
        idx = jnp.where(masked == m, ee, 1e9).min(axis=0, keepdims=True)
        hit = ee == idx
        idx_rows.append(idx)
        sel_rows.append(jnp.where(hit, aff, 0.0).sum(axis=0, keepdims=True))
        masked = jnp.where(hit, -jnp.inf, masked)
        chosen = chosen + jnp.where(hit, 1.0, 0.0)
    sel = jnp.concatenate(sel_rows, axis=0)
    eidx_ref[...] = jnp.concatenate(idx_rows, axis=0).astype(I32)
    gate_ref[...] = sel / sel.sum(axis=0, keepdims=True) * ROUTED_SCALE

    @pl.when(pl.program_id(1) == 0)
    def _():
        cnt_ref[...] = jnp.zeros(cnt_ref.shape, F32)

    cnt_ref[0] = cnt_ref[0] + chosen.sum(axis=1, keepdims=True)


def _router(h2, wr_t, bias_b, tm):
    B, S, D = h2.shape
    nt = S // tm
    o_spec = pl.BlockSpec((TOP_K, tm), lambda b, i: (0, b * nt + i))
    return pl.pallas_call(
        _router_kernel, grid=(B, nt),
        in_specs=[pl.BlockSpec((1, tm, D), lambda b, i: (b, i, 0)), pl.BlockSpec(wr_t.shape, lambda b, i: (0, 0)),
                  pl.BlockSpec(bias_b.shape, lambda b, i: (0, 0))],
        out_specs=[o_spec, o_spec, pl.BlockSpec((1, N_EXPERTS, LANES), lambda b, i: (b, 0, 0))],
        out_shape=[jax.ShapeDtypeStruct((TOP_K, B * S), I32), jax.ShapeDtypeStruct((TOP_K, B * S), F32),
                   jax.ShapeDtypeStruct((B, N_EXPERTS, LANES), F32)],
        compiler_params=_cparams(("parallel", "arbitrary")),
        name="moe_router",
    )(h2, wr_t, bias_b)


def _experts_kernel(blk_e_ref, row0_ref, nblk_ref, skey_ref, gflat_ref, h_hbm, wg_ref, wu_ref, wd_ref,
                    o_hbm, h_vmem, acc_ref, xg0, xg1, y0, y1, sem):
    b = pl.program_id(0)
    last_step = pl.num_programs(0) - 1
    nblk = nblk_ref[0]
    rpt = wg_ref.shape[1] // LANES
    tc = h_vmem.shape[0] // rpt - 1
    key_mod = 2 * tc * TOP_K

    def first_row(blk):
        return row0_ref[jnp.clip(blk, 0, nblk - 1)]

    def tile_of(a):
        off = (a & ~(TOP_K - 1)) if rpt == TOP_K else (a >> TOP_K_SHIFT) * rpt
        return pl.ds(pl.multiple_of(off, rpt), rpt)

    def gather(blk, xg):
        row0 = first_row(blk)
        for r in range(MOE_BLOCK):
            xg[r * rpt:(r + 1) * rpt, :] = h_vmem[tile_of(skey_ref[row0 + r] & (key_mod - 1)), :]

    def scatter(blk, y):
        row0 = first_row(blk)
        for g0 in range(0, MOE_BLOCK, SCATTER_GROUP):
            updated = []
            for r in range(g0, g0 + SCATTER_GROUP):
                a = skey_ref[row0 + r] & (key_mod - 1)
                rows = tile_of(a)
                updated.append((rows, acc_ref[rows, :] + gflat_ref[a] * y[r * rpt:(r + 1) * rpt, :]))
            for rows, v in updated:
                acc_ref[rows, :] = v

    @pl.when(b == 0)
    def _():
        cp = pltpu.make_async_copy(h_hbm, h_vmem.at[pl.ds(0, tc * rpt)], sem)
        cp.start()
        h_vmem[pl.ds(tc * rpt, rpt), :] = jnp.zeros((rpt, LANES), F32)
        acc_ref[...] = jnp.zeros(acc_ref.shape, F32)
        y0[...] = jnp.zeros(y0.shape, F32)
        y1[...] = jnp.zeros(y1.shape, F32)
        cp.wait()
        gather(0, xg0)

    def step(xg_cur, xg_nxt, y_cur, y_prv):
        gather(b + 1, xg_nxt)
        xb = jnp.concatenate([xg_cur[pl.ds(j, MOE_BLOCK, stride=rpt), :] for j in range(rpt)], axis=1).astype(BF16)
        gt = _dot(xb, wg_ref[0].astype(BF16))
        up = _dot(xb, wu_ref[0].astype(BF16))
        y = _dot((_silu(gt) * up).astype(BF16), wd_ref[0].astype(BF16))
        for j in range(rpt):
            y_cur[pl.ds(j, MOE_BLOCK, stride=rpt), :] = y[:, j * LANES:(j + 1) * LANES]
        scatter(b - 1, y_prv)

    @pl.when(jnp.logical_and(b <= nblk, (b & 1) == 0))
    def _():
        step(xg0, xg1, y0, y1)

    @pl.when(jnp.logical_and(b <= nblk, (b & 1) == 1))
    def _():
        step(xg1, xg0, y1, y0)

    @pl.when(b == last_step)
    def _():
        cp = pltpu.make_async_copy(acc_ref.at[pl.ds(0, tc * rpt)], o_hbm, sem)
        cp.start()
        cp.wait()


def _routed_experts(h_tiles, blk_e, row0, nblk, skey, gflat, w_gate, w_up, w_down):
    D, ff = w_gate.shape[1], w_gate.shape[2]
    rpt = D // LANES
    tc = h_tiles.shape[0] // rpt
    nb = blk_e.shape[0]
    wmap = lambda b, be, r0, nbk, sk, gf: (be[jnp.minimum(b, nbk[0] - 1)], 0, 0)
    rows = pltpu.VMEM((MOE_BLOCK * rpt, LANES), F32)
    resident = pltpu.VMEM(((tc + 1) * rpt, LANES), F32)
    grid_spec = pltpu.PrefetchScalarGridSpec(
        num_scalar_prefetch=5, grid=(nb + 1,),
        in_specs=[pl.BlockSpec(memory_space=pl.ANY),
                  pl.BlockSpec((1, D, ff), wmap), pl.BlockSpec((1, D, ff), wmap), pl.BlockSpec((1, ff, D), wmap)],
        out_specs=pl.BlockSpec(memory_space=pl.ANY),
        scratch_shapes=[resident, resident, rows, rows, rows, rows, pltpu.SemaphoreType.DMA(())],
    )
    return pl.pallas_call(
        _experts_kernel, grid_spec=grid_spec,
        out_shape=jax.ShapeDtypeStruct((tc * rpt, LANES), F32),
        compiler_params=_cparams(("arbitrary",), vmem=56 * 1024 * 1024),
        name="moe_routed_experts",
    )(blk_e, row0, nblk, skey, gflat, h_tiles, w_gate, w_up, w_down)


def _dispatch_tables(eidx, gates, counts, B, S):
    n_assign = S * TOP_K
    assert n_assign & (n_assign - 1) == 0
    key_mod = 2 * n_assign
    fill = MOE_BLOCK - 1
    e = eidx.reshape(TOP_K, B, S)
    t = lax.broadcasted_iota(I32, e.shape, 2)
    k = lax.broadcasted_iota(I32, e.shape, 0)
    real = (e * key_mod + t * TOP_K + k).transpose(1, 0, 2).reshape(B, n_assign)
    filler = jnp.repeat(jnp.arange(N_EXPERTS, dtype=I32) * key_mod + n_assign, fill)
    n_sorted = pl.next_power_of_2(n_assign + N_EXPERTS * fill + MOE_BLOCK)
    tail = jnp.full((n_sorted - n_assign - N_EXPERTS * fill,), N_EXPERTS * key_mod + n_assign, I32)
    keys = jnp.concatenate([real, jnp.broadcast_to(jnp.concatenate([filler, tail]), (B, n_sorted - n_assign))], axis=1)
    skey = lax.sort(keys, dimension=1)
    seg = jnp.cumsum(counts, axis=1) - counts + fill * jnp.arange(N_EXPERTS, dtype=I32)[None, :]
    nblocks_e = (counts + MOE_BLOCK - 1) // MOE_BLOCK
    bend = jnp.cumsum(nblocks_e, axis=1)
    bstart = bend - nblocks_e
    nb = n_assign // MOE_BLOCK + N_EXPERTS
    bidx = jnp.arange(nb, dtype=I32)
    blk_e = jnp.minimum(jnp.sum((bend[:, None, :] <= bidx[None, :, None]).astype(I32), axis=-1), N_EXPERTS - 1)
    onehot = (blk_e[:, :, None] == jnp.arange(N_EXPERTS, dtype=I32)[None, None, :]).astype(I32)
    pick = lambda v: jnp.sum(onehot * v[:, None, :], axis=-1)
    row0 = pick(seg) + (bidx[None, :] - pick(bstart)) * MOE_BLOCK
    gflat = gates.reshape(TOP_K, B, S).transpose(1, 2, 0).reshape(B, n_assign)
    gflat = jnp.concatenate([gflat, jnp.zeros((B, TOP_K), F32)], axis=1)
    return blk_e, row0, bend[:, -1:], skey, gflat


def _shared_kernel(h_ref, wg_ref, wu_ref, wd_ref, routed_ref, x1_ref, gt_ref, o_ref):
    hb = h_ref[0].astype(BF16)
    act = _silu(_dot(hb, wg_ref[...])) * _dot(hb, wu_ref[...])
    shared = _dot(act.astype(BF16), wd_ref[...])
    o_ref[0] = x1_ref[0] + gt_ref[0] * (_from_token_tiles(routed_ref, hb.shape[0]) + shared)


def _shared_and_residual(h2, wsg, wsu, wsd, routed, x1, gt2, tm):
    B, S, D = h2.shape
    row = lambda b, i: (b, i, 0)
    xs = pl.BlockSpec((1, tm, D), row)
    const2 = lambda a: pl.BlockSpec(a.shape, lambda b, i: (0, 0))
    return pl.pallas_call(
        _shared_kernel, grid=(B, S // tm),
        in_specs=[xs, const2(wsg), const2(wsu), const2(wsd), pl.BlockSpec((1, tm * (D // LANES), LANES), row), xs,
                  pl.BlockSpec((1, 1, D), lambda b, i: (b, 0, 0))],
        out_specs=xs,
        out_shape=jax.ShapeDtypeStruct((B, S, D), F32),
        compiler_params=_cparams(("parallel", "parallel")),
        name="shared_expert_residual",
    )(h2, wsg, wsu, wsd, routed, x1, gt2)


def _prep_in_weights(w_in, g_q_dil, g_k_dil, g_q_nsa, g_k_slc, g_k_win):
    D = w_in.shape[0]
    pad = jnp.zeros((D, IN_COLS_PADDED - w_in.shape[1]), w_in.dtype)
    w_all = jnp.concatenate([w_in, pad], axis=1).astype(BF16)
    one = jnp.ones((LANES,), F32)
    gcol = jnp.concatenate([
        jnp.tile(g_q_dil, NH_DIL), jnp.tile(g_k_dil, NH_DIL), jnp.ones((512,), F32), jnp.tile(g_q_nsa, NH_NSA),
        one, one, jnp.tile(g_k_slc, NKV_NSA), one, jnp.tile(g_k_win, NKV_NSA), one, one]).reshape(1, IN_COLS_PADDED)
    return w_all, gcol


def _rope_tables(S):
    inv_freq = ROPE_THETA ** (-jnp.arange(ROT_HALF, dtype=F32) / ROT_HALF)
    ang = jnp.arange(S).astype(F32)[:, None] * inv_freq[None, :]
    cos, sin = jnp.cos(ang), jnp.sin(ang)
    zeros = jnp.zeros((S, HD - ROT_DIM), F32)
    z8 = jnp.zeros((S, ROT_HALF), F32)
    cos_h = jnp.concatenate([cos, cos, jnp.ones((S, HD - ROT_DIM), F32)], axis=1)
    sa_h = jnp.concatenate([-sin, z8, zeros], axis=1)
    sb_h = jnp.concatenate([z8, sin, zeros], axis=1)
    two = lambda t: jnp.concatenate([t, t], axis=1)
    return two(cos_h), two(sa_h), two(sb_h)


def _prep_compress(pe, w1, w2):
    eye = jnp.eye(NKV_NSA, dtype=F32)
    w1r = w1.reshape(CMP_LEN, HD, CMP_HIDDEN)

    def half(w1h, peh):
        w = jnp.einsum("ldh,gk->lgdkh", w1h, eye).reshape(CMP_STRIDE * LANES, NKV_NSA * CMP_HIDDEN)
        p = jnp.broadcast_to(peh[:, None, :], (CMP_STRIDE, NKV_NSA, HD)).reshape(1, CMP_STRIDE * LANES)
        return p, w.astype(BF16)

    pa, wa = half(w1r[:CMP_STRIDE], pe[:CMP_STRIDE])
    pb, wb = half(w1r[CMP_STRIDE:], pe[CMP_STRIDE:])
    w2bd = jnp.einsum("hd,gk->ghkd", w2, eye).reshape(NKV_NSA * CMP_HIDDEN, LANES).astype(BF16)
    return pa, pb, wa, wb, w2bd


def _overlap_tables(S):
    ncb = S // CMP_STRIDE
    n_sel = S // SEL_BLOCK
    cs = np.arange(ncb) * CMP_STRIDE
    ss = np.arange(n_sel) * SEL_BLOCK
    ov = np.clip(np.minimum(cs[:, None] + CMP_LEN, ss[None, :] + SEL_BLOCK) - np.maximum(cs[:, None], ss[None, :]), 0, None)
    ov = ov.astype(np.float32) / CMP_STRIDE
    ovl = np.zeros((ncb, LANES), np.float32)
    ovr = np.zeros((ncb, LANES), np.float32)
    ovl[:, :n_sel] = ov
    ovr[:, HD:HD + n_sel] = ov
    return jnp.asarray(ovl, BF16), jnp.asarray(ovr, BF16)


def _block_ones():
    r = np.arange(LANES)
    return jnp.asarray((r[:, None] // HD == r[None, :] // HD).astype(np.float32), BF16)


def _layer(x, c, w_ada, b_ada, g_norm_mix, g_norm_ffn, w_in, g_q_dil, g_k_dil, g_q_nsa, g_k_cmp, g_k_slc, g_k_win,
           cmp_pe_k, cmp_w1_k, cmp_w2_k, cmp_pe_v, cmp_w1_v, cmp_w2_v, w_out, w_router, router_bias,
           w_gate, w_up, w_down, ws_gate, ws_up, ws_down):
    B, S, D = x.shape
    ones_bd = _block_ones()

    c_pad = jnp.zeros((8, D), F32).at[:B].set(c)
    mod = _ada_mod(c_pad, w_ada, b_ada.reshape(1, -1))[:B]
    sh1, sc1, gt1, sh2, sc2, gt2 = [m.reshape(B, 1, D) for m in jnp.split(mod, 6, axis=-1)]

    w_all, gcol = _prep_in_weights(w_in, g_q_dil, g_k_dil, g_q_nsa, g_k_slc, g_k_win)
    cos_t, sa_t, sb_t = _rope_tables(S)
    qa, ka, va, qn, qr, kc, vc, ksl, vsl, kw, vw, gate = _in_projection(
        x, g_norm_mix.reshape(1, D), sc1, sh1, w_all, gcol, cos_t, sa_t, sb_t, ones_bd, tm=256)

    o_a = _dilated_attention(qa, ka, va)

    kcmp, vcmp = _compress(kc, vc, _prep_compress(cmp_pe_k, cmp_w1_k, cmp_w2_k),
                           _prep_compress(cmp_pe_v, cmp_w1_v, cmp_w2_v), jnp.tile(g_k_cmp, NKV_NSA).reshape(1, LANES),
                           ones_bd)
    ovl, ovr = _overlap_tables(S)
    o_cmp, sel = _cmp_select(qn, kcmp, vcmp, ovl, ovr, gate, tq=128)
    o_cs = _selected_attention(qr, ksl, vsl, sel, o_cmp, gate, tq=256, tk=512)
    gqa = dict(kv_tile=lambda h: 0, kv_half=lambda h: h // NSA_REP)
    o_b = _band_attention(qr, kw, vw, dil=1, tq=256, n_prev=2, max_dist=NSA_WINDOW - 1, mode="gated",
                          extra=(o_cs, gate), out_dtype=BF16, **gqa)

    w_out_b = w_out.astype(BF16)
    x1, h2, h2_tiles = _out_projection(o_a, o_b, w_out_b[:NH_DIL * HD], w_out_b[NH_DIL * HD:], x, gt1,
                                       g_norm_ffn.reshape(1, D), sc2, sh2, tm=256)

    tm_r = 256
    eidx, gates, counts = _router(h2, w_router.T.astype(BF16),
                                  jnp.broadcast_to(router_bias.reshape(N_EXPERTS, 1), (N_EXPERTS, tm_r)), tm=tm_r)
    tables = _dispatch_tables(eidx, gates, counts[:, :, 0].astype(I32), B, S)
    routed = jnp.stack([_routed_experts(h2_tiles[b], *[tbl[b] for tbl in tables], w_gate, w_up, w_down)
                        for b in range(B)], axis=0)
    return _shared_and_residual(h2, ws_gate.astype(BF16), ws_up.astype(BF16), ws_down.astype(BF16), routed, x1, gt2,
                                tm=256)


def kernel(x, c, w_ada, b_ada, g_norm_mix, g_norm_ffn, w_in, g_q_dil, g_k_dil, g_q_nsa, g_k_cmp, g_k_slc, g_k_win, cmp_pe_k, cmp_w1_k, cmp_w2_k, cmp_pe_v, cmp_w1_v, cmp_w2_v, w_out, w_router, router_bias, w_gate, w_up, w_down, ws_gate, ws_up, ws_down):
    params = (w_ada, b_ada, g_norm_mix, g_norm_ffn, w_in, g_q_dil, g_k_dil, g_q_nsa, g_k_cmp, g_k_slc, g_k_win,
              cmp_pe_k, cmp_w1_k, cmp_w2_k, cmp_pe_v, cmp_w1_v, cmp_w2_v, w_out, w_router, router_bias,
              w_gate, w_up, w_down, ws_gate, ws_up, ws_down)
    for layer in range(w_ada.shape[0]):
        x = _layer(x, c, *[a[layer] for a in params])
    return x
```

```python
import functools
import math

import numpy as np
import jax
import jax.numpy as jnp
from jax import lax
from jax.experimental import pallas as pl
from jax.experimental.pallas import tpu as pltpu

F32 = jnp.float32
BF16 = jnp.bfloat16
I32 = jnp.int32

HD = 64
LANES = 128
NH_DIL = 8
NH_NSA = 8
NKV_NSA = 2
NSA_REP = NH_NSA // NKV_NSA
DIL_PAIRS = ((128, 1), (512, 4), (2048, 16))
DIL_TILE = 128
DIL_UNROLL = 8
ROPE_THETA = 500000.0
ROT_DIM = HD // 4
ROT_HALF = ROT_DIM // 2
CMP_LEN = 32
CMP_STRIDE = 16
CMP_HIDDEN = 128
SEL_BLOCK = 64
SEL_SHIFT = 6
SEL_TOP = 16
NSA_WINDOW = 512
N_EXPERTS = 256
TOP_K = 8
TOP_K_SHIFT = 3
N_GROUPS = 8
GROUP_SIZE = N_EXPERTS // N_GROUPS
TOPK_GROUPS = 4
EXPERT_FF = 256
SHARED_FF = 256
ROUTED_SCALE = 2.5
EPS = 1e-6
QK_SCALE = 1.0 / math.sqrt(HD)

NEG = -1e30
BIG = 3e38

C_QA, C_KA, C_VA, C_QB = 0, 512, 1024, 1536
C_KC, C_VC, C_KSL, C_VSL, C_KW, C_VW, C_GB = 2048, 2176, 2304, 2432, 2560, 2688, 2816
IN_COLS_PADDED = 2944
VSL_OUT = 8

MOE_BLOCK = 128
NULL_ROWS = 8
SCATTER_GROUP = 16
VMEM_LIMIT = 48 * 1024 * 1024


def _cparams(sem, vmem=VMEM_LIMIT):
    return pltpu.CompilerParams(dimension_semantics=sem, vmem_limit_bytes=vmem)


def _sigmoid(v):
    return 1.0 / (1.0 + jnp.exp(-v))


def _silu(v):
    return v * _sigmoid(v)


def _dot(a, b):
    return jnp.dot(a, b, preferred_element_type=F32)


def _dot_nt(a, b):
    return lax.dot_general(a, b, (((1,), (1,)), ((), ())), preferred_element_type=F32)


def _lane_lo(shape):
    return lax.broadcasted_iota(I32, shape, len(shape) - 1) < HD


def _head_sums(v, ones_bd):
    hi = v.astype(BF16)
    lo = (v - hi.astype(F32)).astype(BF16)
    return _dot(hi, ones_bd) + _dot(lo, ones_bd)


def _head_rmsnorm(y, ones_bd, gain):
    ms = _head_sums(y * y, ones_bd) * (1.0 / HD)
    return y * lax.rsqrt(ms + EPS) * gain


def _rope(y, cos, sin_a, sin_b):
    return y * cos + pltpu.roll(y, LANES - ROT_HALF, 1) * sin_a + pltpu.roll(y, ROT_HALF, 1) * sin_b


def _ada_kernel(c_ref, w_ref, b_ref, o_ref):
    a = _silu(c_ref[...]).astype(BF16)
    o_ref[...] = _dot(a, w_ref[...].astype(BF16)) + b_ref[...]


def _ada_mod(c_pad, w_ada, b_ada):
    rows, d = c_pad.shape
    n = w_ada.shape[1]
    tn = 1536 if n % 1536 == 0 else n
    return pl.pallas_call(
        _ada_kernel,
        grid=(n // tn,),
        in_specs=[pl.BlockSpec((rows, d), lambda j: (0, 0)),
                  pl.BlockSpec((d, tn), lambda j: (0, j)),
                  pl.BlockSpec((1, tn), lambda j: (0, j))],
        out_specs=pl.BlockSpec((rows, tn), lambda j: (0, j)),
        out_shape=jax.ShapeDtypeStruct((rows, n), F32),
        compiler_params=_cparams(("arbitrary",)),
        name="ada_mod",
    )(c_pad, w_ada, b_ada)


def _inproj_kernel(x_ref, g_ref, sc_ref, sh_ref, w_ref, gcol_ref, cos_ref, sa_ref, sb_ref, ones_ref,
                   qa_ref, ka_ref, va_ref, qn_ref, qr_ref, kc_ref, vc_ref, ksl_ref, vsl_ref, kw_ref, vw_ref,
                   gate_ref):
    x = x_ref[0]
    ms = jnp.mean(x * x, axis=-1, keepdims=True)
    h = (x * lax.rsqrt(ms + EPS)) * g_ref[...]
    h = h * (1.0 + sc_ref[0]) + sh_ref[0]
    hb = h.astype(BF16)
    ones_bd = ones_ref[...]
    cos, sa, sb = cos_ref[...], sa_ref[...], sb_ref[...]
    lo = _lane_lo(cos.shape)

    def proj(c0, width):
        return _dot(hb, w_ref[:, c0:c0 + width])

    def normed(tile, c0):
        return _head_rmsnorm(tile, ones_bd, gcol_ref[:, c0:c0 + LANES])

    acc = proj(C_QA, 512)
    for p in range(4):
        y = _rope(normed(acc[:, p * LANES:(p + 1) * LANES], C_QA + p * LANES), cos, sa, sb) * QK_SCALE
        qa_ref[0, :, p * LANES:(p + 1) * LANES] = y
    acc = proj(C_KA, 512)
    for p in range(4):
        y = _rope(normed(acc[:, p * LANES:(p + 1) * LANES], C_KA + p * LANES), cos, sa, sb)
        ka_ref[0, :, p * LANES:(p + 1) * LANES] = y
    va_ref[0] = proj(C_VA, 512)

    acc = proj(C_QB, 512)
    for p in range(4):
        yn = normed(acc[:, p * LANES:(p + 1) * LANES], C_QB + p * LANES)
        yr = _rope(yn, cos, sa, sb)
        for y, ref in ((yn * QK_SCALE, qn_ref), (yr * QK_SCALE, qr_ref)):
            ysw = pltpu.roll(y, HD, 1)
            for half in range(2):
                head = 2 * p + half
                grp = head // NSA_REP
                src = y if grp == half else ysw
                keep = lo if grp == 0 else jnp.logical_not(lo)
                ref[0, :, head * LANES:(head + 1) * LANES] = jnp.where(keep, src, 0.0).astype(BF16)

    acc = proj(C_KC, 512)
    kc_ref[0] = acc[:, 0:LANES]
    vc_ref[0] = acc[:, LANES:2 * LANES]
    ksl_ref[0] = _rope(normed(acc[:, 2 * LANES:3 * LANES], C_KSL), cos, sa, sb).astype(BF16)
    vsl = acc[:, 3 * LANES:4 * LANES]
    vsl_ref[0] = jnp.concatenate([vsl[r0:r0 + LANES].T for r0 in range(0, vsl.shape[0], LANES)], axis=1).astype(BF16)
    acc = proj(C_KW, 384)
    kw_ref[0] = _rope(normed(acc[:, 0:LANES], C_KW), cos, sa, sb).astype(BF16)
    vw_ref[0] = acc[:, LANES:2 * LANES].astype(BF16)
    gate_ref[0] = _sigmoid(acc[:, 2 * LANES:3 * LANES])


def _in_projection(x, g_mix, sc1, sh1, w_all, gcol, cos_t, sa_t, sb_t, ones_bd, tm):
    B, S, D = x.shape
    nc = w_all.shape[1]
    row = lambda b, i: (b, i, 0)
    bvec = pl.BlockSpec((1, 1, D), lambda b, i: (b, 0, 0))
    tab = pl.BlockSpec((tm, LANES), lambda b, i: (i, 0))
    const2 = lambda shape: pl.BlockSpec(shape, lambda b, i: (0, 0))
    widths_dt = [(512, F32), (512, F32), (512, F32), (1024, BF16), (1024, BF16), (LANES, F32), (LANES, F32),
                 (LANES, BF16), (LANES, BF16), (LANES, BF16), (LANES, BF16), (LANES, F32)]
    return pl.pallas_call(
        _inproj_kernel,
        grid=(B, S // tm),
        in_specs=[pl.BlockSpec((1, tm, D), row), const2((1, D)), bvec, bvec, const2((D, nc)), const2((1, nc)),
                  tab, tab, tab, const2((LANES, LANES))],
        out_specs=[pl.BlockSpec((1, tm, w), row) if n != VSL_OUT else pl.BlockSpec((1, w, tm), lambda b, i: (b, 0, i))
                   for n, (w, _) in enumerate(widths_dt)],
        out_shape=[jax.ShapeDtypeStruct((B, S, w) if n != VSL_OUT else (B, w, S), dt)
                   for n, (w, dt) in enumerate(widths_dt)],
        compiler_params=_cparams(("parallel", "parallel")),
        name="in_projection",
    )(x, g_mix, sc1, sh1, w_all, gcol, cos_t, sa_t, sb_t, ones_bd)


def _pair_tile(lo, t0, half0, t1, half1):
    a = t0 if half0 == 0 else pltpu.roll(t0, HD, 1)
    b = t1 if half1 == 1 else pltpu.roll(t1, HD, 1)
    return jnp.where(lo, a, b)


def _band_kernel(*refs, tq, n_prev, max_dist, n_heads, kv_tile, kv_half, mode):
    nk = n_prev + 1
    q_ref = refs[0]
    k_refs = refs[1:1 + nk]
    v_refs = refs[1 + nk:1 + 2 * nk]
    rest = refs[1 + 2 * nk:]
    i = pl.program_id(2)

    row = lax.broadcasted_iota(I32, (tq, tq), 0)
    col = lax.broadcasted_iota(I32, (tq, tq), 1)
    diff = row - col
    biases = []
    for j in range(nk):
        d = diff + (n_prev - j) * tq
        ok = jnp.where(d >= 0, jnp.where(d <= max_dist, 0.0, NEG), NEG)
        biases.append(jnp.where(i >= n_prev - j, ok, NEG))
    lo = _lane_lo((tq, LANES))

    outs, lses = [], []
    for h in range(n_heads):
        qh = q_ref[0, :, h * LANES:(h + 1) * LANES]
        kt = kv_tile(h)
        s = [_dot_nt(qh, k_refs[j][0, :, kt * LANES:(kt + 1) * LANES]) + biases[j] for j in range(nk)]
        m = s[0].max(axis=1, keepdims=True)
        for j in range(1, nk):
            m = jnp.maximum(m, s[j].max(axis=1, keepdims=True))
        p = [jnp.exp(sj - m) for sj in s]
        l = p[0].sum(axis=1, keepdims=True)
        for j in range(1, nk):
            l = l + p[j].sum(axis=1, keepdims=True)
        o = _dot(p[0].astype(BF16), v_refs[0][0, :, kt * LANES:(kt + 1) * LANES])
        for j in range(1, nk):
            o = o + _dot(p[j].astype(BF16), v_refs[j][0, :, kt * LANES:(kt + 1) * LANES])
        outs.append(o / l)
        lses.append(m + jnp.log(l))

    for p_ in range(n_heads // 2):
        h0, h1 = 2 * p_, 2 * p_ + 1
        sl = slice(p_ * LANES, (p_ + 1) * LANES)
        o_pair = _pair_tile(lo, outs[h0], kv_half(h0), outs[h1], kv_half(h1))
        if mode == "gated":
            prev_ref, gate_ref, out_ref = rest
            c0, c1 = 3 * h0 + 2, 3 * h1 + 2
            g = jnp.where(lo, gate_ref[0, :, c0:c0 + 1], gate_ref[0, :, c1:c1 + 1])
            out_ref[0, :, sl] = (prev_ref[0, :, sl] + g * o_pair).astype(out_ref.dtype)
            continue
        lse_pair = jnp.where(lo, lses[h0], lses[h1])
        if mode == "first":
            o_out, lse_out = rest
            o_out[0, :, sl] = o_pair
            lse_out[0, :, sl] = lse_pair
            continue
        po_ref, pl_ref = rest[0], rest[1]
        lp = pl_ref[0, :, sl]
        mx = jnp.maximum(lp, lse_pair)
        wp = jnp.exp(lp - mx)
        wn = jnp.exp(lse_pair - mx)
        den = wp + wn
        merged = (po_ref[0, :, sl] * wp + o_pair * wn) / den
        if mode == "mid":
            rest[2][0, :, sl] = merged
            rest[3][0, :, sl] = mx + jnp.log(den)
        else:
            rest[2][0, :, sl] = merged.astype(rest[2].dtype)


def _band_attention(q, k, v, *, dil, tq, n_prev, max_dist, kv_tile, kv_half, mode, extra=(), out_dtype=F32):
    B, S, qw = q.shape
    kvw = k.shape[2]
    L = S // dil
    nk = n_prev + 1
    ow = NH_DIL * HD
    view = lambda a: a.reshape(B, L, dil * a.shape[2])
    cur = lambda b, r, i: (b, i, r)

    def prev_map(j):
        return lambda b, r, i: (b, jnp.maximum(i - (n_prev - j), 0), r)

    in_specs = [pl.BlockSpec((1, tq, qw), cur)]
    in_specs += [pl.BlockSpec((1, tq, kvw), prev_map(j)) for j in range(nk)] * 2
    args = [view(q)] + [view(k)] * nk + [view(v)] * nk
    for a in extra:
        in_specs.append(pl.BlockSpec((1, tq, a.shape[2]), cur))
        args.append(view(a))
    o_spec = pl.BlockSpec((1, tq, ow), cur)
    if mode in ("first", "mid"):
        out_specs = [o_spec, o_spec]
        out_shape = [jax.ShapeDtypeStruct((B, L, dil * ow), F32)] * 2
    else:
        out_specs = o_spec
        out_shape = jax.ShapeDtypeStruct((B, L, dil * ow), out_dtype)
    kern = functools.partial(_band_kernel, tq=tq, n_prev=n_prev, max_dist=max_dist, n_heads=NH_DIL,
                             kv_tile=kv_tile, kv_half=kv_half, mode=mode)
    res = pl.pallas_call(
        kern, grid=(B, dil, L // tq), in_specs=in_specs, out_specs=out_specs, out_shape=out_shape,
        compiler_params=_cparams(("parallel", "parallel", "parallel")),
        name=f"band_attention_d{dil}_{mode}",
    )(*args)
    if mode in ("first", "mid"):
        return res[0].reshape(B, S, ow), res[1].reshape(B, S, ow)
    return res.reshape(B, S, ow)


def _dot_tn(a, b):
    return lax.dot_general(a, b, (((0,), (0,)), ((), ())), preferred_element_type=F32)


def _dilated_kernel(q_ref, k_ref, v_ref, o_ref, oacc, lacc, *, sb_rows, pairs):
    sb = pl.program_id(2)
    base = sb * sb_rows
    t = DIL_TILE
    kk = lax.broadcasted_iota(I32, (t, t), 0)
    qq = lax.broadcasted_iota(I32, (t, t), 1)
    bias_cur = jnp.where(qq >= kk, 0.0, NEG)
    bias_prev = jnp.where(qq <= kk, 0.0, NEG)
    lane_lo = _lane_lo((t, LANES))
    row_lo = lax.broadcasted_iota(I32, (LANES, t), 0) < HD

    def block(d, qs, first, mode):
        qg = base + qs
        rows_q = pl.ds(qs, t, stride=d) if d > 1 else pl.ds(qs, t)

        def strided(ref, start):
            return ref[0, pl.ds(start, t, stride=d), :] if d > 1 else ref[0, pl.ds(start, t), :]

        prev_start = jnp.where(first, qg, qg - t * d)
        qt = q_ref[0, rows_q, :].astype(BF16)
        zero = jnp.zeros_like(qt)
        q2 = jnp.concatenate([jnp.where(lane_lo, qt, zero), jnp.where(lane_lo, zero, qt)], axis=0)
        k2 = jnp.concatenate([strided(k_ref, prev_start), strided(k_ref, qg)], axis=0).astype(BF16)
        v2 = jnp.concatenate([strided(v_ref, prev_start), strided(v_ref, qg)], axis=0).astype(BF16)
        bias = jnp.concatenate([jnp.where(first, NEG, bias_prev), bias_cur], axis=0)
        s = _dot_nt(k2, q2) + jnp.concatenate([bias, bias], axis=1)
        m = s.max(axis=0, keepdims=True)
        p = jnp.exp(s - m)
        l = p.sum(axis=0, keepdims=True)
        ov = _dot_tn(v2, p.astype(BF16)) / l
        lse2 = m + jnp.log(l)
        o = jnp.where(row_lo, ov[:, :t], ov[:, t:]).T
        lse = jnp.where(row_lo, lse2[:, :t], lse2[:, t:]).T
        if mode == "init":
            oacc[rows_q, :] = o
            lacc[rows_q, :] = lse
            return
        lp = lacc[rows_q, :]
        mx = jnp.maximum(lp, lse)
        wp, wn = jnp.exp(lp - mx), jnp.exp(lse - mx)
        den = wp + wn
        oacc[rows_q, :] = (oacc[rows_q, :] * wp + o * wn) / den
        if mode == "mid":
            lacc[rows_q, :] = mx + jnp.log(den)

    modes = ("init",) + ("mid",) * (len(pairs) - 2) + ("last",)
    for (window, d), mode in zip(pairs, modes):
        assert window // d == t and d & (d - 1) == 0
        n_j = sb_rows // (t * d)

        def body(i, c, d=d, mode=mode):
            r, j = i & (d - 1), i >> (d.bit_length() - 1)
            block(d, j * (t * d) + r, jnp.logical_and(sb == 0, j == 0), mode)
            return c

        lax.fori_loop(0, d * n_j, body, 0, unroll=DIL_UNROLL)
    o_ref[0] = oacc[...].astype(o_ref.dtype)


def _dilated_attention(qa, ka, va):
    B, S, w = qa.shape
    sb_rows = DIL_TILE * max(d for _, d in DIL_PAIRS)
    assert S % sb_rows == 0
    kern = functools.partial(_dilated_kernel, sb_rows=sb_rows, pairs=DIL_PAIRS)
    whole = pl.BlockSpec((1, S, LANES), lambda b, p, s: (b, 0, p))
    blk = pl.BlockSpec((1, sb_rows, LANES), lambda b, p, s: (b, s, p))
    acc = pltpu.VMEM((sb_rows, LANES), F32)
    return pl.pallas_call(
        kern, grid=(B, w // LANES, S // sb_rows),
        in_specs=[blk, whole, whole], out_specs=blk,
        out_shape=jax.ShapeDtypeStruct((B, S, w), BF16),
        scratch_shapes=[acc, acc],
        compiler_params=_cparams(("parallel", "parallel", "arbitrary")),
        name="dilated_attention",
    )(qa, ka, va)


def _compress_kernel(kc_ref, vc_ref, pak_ref, pbk_ref, wak_ref, wbk_ref, w2k_ref, gk_ref, ones_ref,
                     pav_ref, pbv_ref, wav_ref, wbv_ref, w2v_ref, ko_ref, vo_ref):
    ncb = kc_ref.shape[1]

    def branch(t_ref, pa, pb, wa, wb, w2):
        t = t_ref[0]
        ua = _dot((t + pa[...]).astype(BF16), wa[...])
        ub = _dot((t + pb[...]).astype(BF16), wb[...])
        pre = ua + pltpu.roll(ub, ncb - 1, 0)
        return _dot(jax.nn.gelu(pre).astype(BF16), w2[...])

    live = lax.broadcasted_iota(I32, (ncb, LANES), 0) < ncb - 1
    kcmp = _head_rmsnorm(branch(kc_ref, pak_ref, pbk_ref, wak_ref, wbk_ref, w2k_ref), ones_ref[...], gk_ref[...])
    ko_ref[0] = jnp.where(live, kcmp, 0.0).astype(BF16)
    vo_ref[0] = jnp.where(live, branch(vc_ref, pav_ref, pbv_ref, wav_ref, wbv_ref, w2v_ref), 0.0).astype(BF16)


def _compress(kc, vc, wk, wv, gk2, ones_bd):
    B, S, _ = kc.shape
    ncb = S // CMP_STRIDE
    cw = CMP_STRIDE * LANES
    view = lambda a: a.reshape(B, ncb, cw)
    blk = pl.BlockSpec((1, ncb, cw), lambda b: (b, 0, 0))
    full = lambda a: pl.BlockSpec(a.shape, lambda b: (0,) * a.ndim)
    consts = [*wk, gk2, ones_bd, *wv]
    o_spec = pl.BlockSpec((1, ncb, LANES), lambda b: (b, 0, 0))
    return pl.pallas_call(
        _compress_kernel, grid=(B,),
        in_specs=[blk, blk] + [full(a) for a in consts],
        out_specs=[o_spec, o_spec],
        out_shape=[jax.ShapeDtypeStruct((B, ncb, LANES), BF16)] * 2,
        compiler_params=_cparams(("parallel",)),
        name="nsa_compress",
    )(view(kc), view(vc), *consts)


def _cmpsel_kernel(qn_ref, kc_ref, vc_ref, ovl_ref, ovr_ref, gate_ref, o_ref, sel_ref, *, tq, n_sel, top):
    ncb = kc_ref.shape[1]
    i = pl.program_id(1)
    pos0 = i * tq
    qpos = pos0 + lax.broadcasted_iota(I32, (tq, ncb), 0)
    cend = lax.broadcasted_iota(I32, (tq, ncb), 1) * CMP_STRIDE + (CMP_LEN - 1)
    valid = cend <= qpos
    lo = _lane_lo((tq, LANES))
    kc = kc_ref[0]
    vc = vc_ref[0]

    outs = []
    psum = [None] * NKV_NSA
    for h in range(NH_NSA):
        g = h // NSA_REP
        s = _dot_nt(qn_ref[0, :, h * LANES:(h + 1) * LANES], kc)
        m = jnp.where(valid, s, NEG).max(axis=1, keepdims=True)
        m = jnp.where(m > 0.5 * NEG, m, 0.0)
        p = jnp.where(valid, jnp.exp(s - m), 0.0)
        l = p.sum(axis=1, keepdims=True)
        p = p / jnp.where(l > 0.0, l, 1.0)
        outs.append(_dot(p.astype(BF16), vc))
        psum[g] = p if psum[g] is None else psum[g] + p

    for p_ in range(NH_NSA // 2):
        h0, h1 = 2 * p_, 2 * p_ + 1
        o_pair = _pair_tile(lo, outs[h0], h0 // NSA_REP, outs[h1], h1 // NSA_REP)
        g = jnp.where(lo, gate_ref[0, :, 3 * h0:3 * h0 + 1], gate_ref[0, :, 3 * h1:3 * h1 + 1])
        o_ref[0, :, p_ * LANES:(p_ + 1) * LANES] = g * o_pair

    def hilo_dot(a, w):
        hi = a.astype(BF16)
        return _dot(hi, w) + _dot((a - hi.astype(F32)).astype(BF16), w)

    pslc = hilo_dot(psum[0], ovl_ref[...]) + hilo_dot(psum[1], ovr_ref[...])
    pslc_t = pslc.T
    jblk = lax.broadcasted_iota(I32, (HD, tq), 0)
    cur = (pos0 + lax.broadcasted_iota(I32, (HD, tq), 1)) >> SEL_SHIFT
    forced = (jblk == 0) | (jblk == cur) | (jblk == cur - 1)
    sel_t = []
    for g in range(NKV_NSA):
        sc = jnp.where(forced, BIG, jnp.where(jblk <= cur, pslc_t[g * HD:(g + 1) * HD, :], -BIG))
        rank = jnp.zeros((HD, tq), F32)
        for k in range(n_sel):
            rk = sc[k:k + 1, :]
            rank = rank + jnp.where(jblk > k, jnp.where(rk >= sc, 1.0, 0.0), jnp.where(rk > sc, 1.0, 0.0))
        sel_t.append(jnp.where(rank < top, 1.0, 0.0))
    sel_ref[0] = jnp.concatenate(sel_t, axis=0)


def _cmp_select(qn, kcmp, vcmp, ovl, ovr, gate, tq):
    B, S, qw = qn.shape
    ncb = kcmp.shape[1]
    n_sel = S // SEL_BLOCK
    assert n_sel <= HD and tq == LANES
    row = lambda b, i: (b, i, 0)
    cb = pl.BlockSpec((1, ncb, LANES), lambda b, i: (b, 0, 0))
    ovs = pl.BlockSpec((ncb, LANES), lambda b, i: (0, 0))
    kern = functools.partial(_cmpsel_kernel, tq=tq, n_sel=n_sel, top=min(SEL_TOP, n_sel))
    return pl.pallas_call(
        kern, grid=(B, S // tq),
        in_specs=[pl.BlockSpec((1, tq, qw), row), cb, cb, ovs, ovs, pl.BlockSpec((1, tq, LANES), row)],
        out_specs=[pl.BlockSpec((1, tq, NH_NSA * HD), row), pl.BlockSpec((1, LANES, tq), lambda b, i: (b, 0, i))],
        out_shape=[jax.ShapeDtypeStruct((B, S, NH_NSA * HD), F32), jax.ShapeDtypeStruct((B, LANES, S), F32)],
        compiler_params=_cparams(("parallel", "parallel")),
        name="nsa_cmp_select",
    )(qn, kcmp, vcmp, ovl, ovr, gate)


def _slc_kernel(q_ref, k_ref, vt_ref, selt_ref, prev_ref, gate_ref, o_ref, m_ref, l_ref, acc_ref, *, tq, tk):
    i = pl.program_id(1)
    kk = pl.program_id(2)
    nkv = pl.num_programs(2)
    last = (i * tq + tq - 1) // tk
    nblk = tk // SEL_BLOCK

    @pl.when(kk == 0)
    def _():
        m_ref[...] = jnp.full(m_ref.shape, NEG, F32)
        l_ref[...] = jnp.zeros(l_ref.shape, F32)
        acc_ref[...] = jnp.zeros(acc_ref.shape, F32)

    @pl.when(kk <= last)
    def _():
        kpos = kk * tk + lax.broadcasted_iota(I32, (tk, tq), 0)
        qpos = i * tq + lax.broadcasted_iota(I32, (tk, tq), 1)
        causal = kpos <= qpos
        k = k_ref[0]
        vt = vt_ref[0]
        for g in range(NKV_NSA):
            rows = selt_ref[0, pl.ds(pl.multiple_of(g * HD + kk * nblk, nblk), nblk), :]
            chosen = jnp.broadcast_to(rows[:, None, :], (nblk, SEL_BLOCK, tq)).reshape(tk, tq) > 0.5
            bias = jnp.where(causal, jnp.where(chosen, 0.0, NEG), NEG)
            heads = range(g * NSA_REP, (g + 1) * NSA_REP)
            q4 = jnp.concatenate([q_ref[0, :, h * LANES:(h + 1) * LANES] for h in heads], axis=0)
            s = _dot_nt(k, q4) + jnp.concatenate([bias] * NSA_REP, axis=1)
            m_prev = m_ref[g]
            m_new = jnp.maximum(m_prev, s.max(axis=0, keepdims=True))
            alpha = jnp.exp(m_prev - m_new)
            p = jnp.exp(s - m_new[0:1])
            l_ref[g] = alpha * l_ref[g] + p.sum(axis=0, keepdims=True)
            acc_ref[g] = alpha[0:1] * acc_ref[g] + _dot(vt, p.astype(BF16))
            m_ref[g] = m_new

    @pl.when(kk == nkv - 1)
    def _():
        lo = _lane_lo((tq, LANES))

        def head_t(h):
            g, r = divmod(h, NSA_REP)
            cols = slice(r * tq, (r + 1) * tq)
            return acc_ref[g, g * HD:(g + 1) * HD, cols] / l_ref[g, 0:1, cols]

        for p_ in range(NH_NSA // 2):
            h0, h1 = 2 * p_, 2 * p_ + 1
            o_pair = jnp.concatenate([head_t(h0), head_t(h1)], axis=0).T
            c0, c1 = 3 * h0 + 1, 3 * h1 + 1
            gt = jnp.where(lo, gate_ref[0, :, c0:c0 + 1], gate_ref[0, :, c1:c1 + 1])
            sl = slice(p_ * LANES, (p_ + 1) * LANES)
            o_ref[0, :, sl] = prev_ref[0, :, sl] + gt * o_pair


def _selected_attention(qr, ksl, vsl_t, sel_t, prev, gate, tq, tk):
    B, S, qw = qr.shape
    row = lambda b, i, kk: (b, i, 0)
    last = lambda i, kk: jnp.minimum(kk, (i * tq + tq - 1) // tk)
    ow = NH_NSA * HD
    kern = functools.partial(_slc_kernel, tq=tq, tk=tk)
    stat = pltpu.VMEM((NKV_NSA, 8, NSA_REP * tq), F32)
    return pl.pallas_call(
        kern, grid=(B, S // tq, S // tk),
        in_specs=[pl.BlockSpec((1, tq, qw), row),
                  pl.BlockSpec((1, tk, LANES), lambda b, i, kk: (b, last(i, kk), 0)),
                  pl.BlockSpec((1, LANES, tk), lambda b, i, kk: (b, 0, last(i, kk))),
                  pl.BlockSpec((1, LANES, tq), lambda b, i, kk: (b, 0, i)),
                  pl.BlockSpec((1, tq, ow), row), pl.BlockSpec((1, tq, LANES), row)],
        out_specs=pl.BlockSpec((1, tq, ow), row),
        out_shape=jax.ShapeDtypeStruct((B, S, ow), F32),
        scratch_shapes=[stat, stat, pltpu.VMEM((NKV_NSA, LANES, NSA_REP * tq), F32)],
        compiler_params=_cparams(("parallel", "parallel", "arbitrary")),
        name="nsa_selected_attention",
    )(qr, ksl, vsl_t, sel_t, prev, gate)


def _to_token_tiles(ref, value):
    m, d = value.shape
    rpt = d // LANES
    for j in range(rpt):
        ref[0, pl.ds(j, m, stride=rpt), :] = value[:, j * LANES:(j + 1) * LANES]


def _from_token_tiles(ref, m):
    rpt = ref.shape[1] // m
    return jnp.concatenate([ref[0, pl.ds(j, m, stride=rpt), :] for j in range(rpt)], axis=1)


def _outproj_kernel(oa_ref, ob_ref, wt_ref, wb_ref, x_ref, gt_ref, g_ref, sc_ref, sh_ref, x1_ref, h2_ref, h2t_ref):
    mix = _dot(oa_ref[0], wt_ref[...]) + _dot(ob_ref[0], wb_ref[...])
    x1 = x_ref[0] + gt_ref[0] * mix
    x1_ref[0] = x1
    ms = jnp.mean(x1 * x1, axis=-1, keepdims=True)
    h = (x1 * lax.rsqrt(ms + EPS)) * g_ref[...]
    h2 = h * (1.0 + sc_ref[0]) + sh_ref[0]
    h2_ref[0] = h2
    _to_token_tiles(h2t_ref, h2)


def _out_projection(oa, ob, w_top, w_bot, x, gt1, g_ffn, sc2, sh2, tm):
    B, S, D = x.shape
    row = lambda b, i: (b, i, 0)
    bvec = pl.BlockSpec((1, 1, D), lambda b, i: (b, 0, 0))
    const2 = lambda a: pl.BlockSpec(a.shape, lambda b, i: (0, 0))
    xs = pl.BlockSpec((1, tm, D), row)
    return pl.pallas_call(
        _outproj_kernel, grid=(B, S // tm),
        in_specs=[pl.BlockSpec((1, tm, oa.shape[2]), row), pl.BlockSpec((1, tm, ob.shape[2]), row),
                  const2(w_top), const2(w_bot), xs, bvec, const2(g_ffn), bvec, bvec],
        out_specs=[xs, xs, pl.BlockSpec((1, tm * (D // LANES), LANES), row)],
        out_shape=[jax.ShapeDtypeStruct((B, S, D), F32)] * 2 + [jax.ShapeDtypeStruct((B, S * (D // LANES), LANES), F32)],
        compiler_params=_cparams(("parallel", "parallel")),
        name="out_projection",
    )(oa, ob, w_top, w_bot, x, gt1, g_ffn, sc2, sh2)


def _router_kernel(h_ref, wr_ref, bias_ref, eidx_ref, gate_ref, cnt_ref):
    tm = h_ref.shape[1]
    logits = _dot_nt(wr_ref[...], h_ref[0].astype(BF16))
    aff = _sigmoid(logits)
    biased = aff + bias_ref[...]
    b3 = biased.reshape(N_GROUPS, GROUP_SIZE, tm)
    jj = lax.broadcasted_iota(I32, (N_GROUPS, GROUP_SIZE, tm), 1).astype(F32)
    m1 = b3.max(axis=1, keepdims=True)
    i1 = jnp.where(b3 == m1, jj, 1e9).min(axis=1, keepdims=True)
    m2 = jnp.where(jj == i1, -jnp.inf, b3).max(axis=1, keepdims=True)
    gs = (m1 + m2).reshape(N_GROUPS, tm)
    gi = lax.broadcasted_iota(I32, (N_GROUPS, tm), 0)
    rank = jnp.zeros((N_GROUPS, tm), F32)
    for k in range(N_GROUPS):
        rk = gs[k:k + 1, :]
        rank = rank + jnp.where(gi > k, jnp.where(rk >= gs, 1.0, 0.0), jnp.where(rk > gs, 1.0, 0.0))
    gsel = (rank < TOPK_GROUPS).reshape(N_GROUPS, 1, tm)
    masked = jnp.where(gsel, b3, -jnp.inf).reshape(N_EXPERTS, tm)
    ee = lax.broadcasted_iota(I32, (N_EXPERTS, tm), 0).astype(F32)
    idx_rows, sel_rows = [], []
    chosen = jnp.zeros((N_EXPERTS, tm), F32)
    for _ in range(TOP_K):
        m = masked.max(axis=0, keepdims=True)
        idx = jnp.where(masked == m, ee, 1e9).min(axis=0, keepdims=True)
        hit = ee == idx
        idx_rows.append(idx)
        sel_rows.append(jnp.where(hit, aff, 0.0).sum(axis=0, keepdims=True))
        masked = jnp.where(hit, -jnp.inf, masked)
        chosen = chosen + jnp.where(hit, 1.0, 0.0)
    sel = jnp.concatenate(sel_rows, axis=0)
    eidx_ref[...] = jnp.concatenate(idx_rows, axis=0).astype(I32)
    gate_ref[...] = sel / sel.sum(axis=0, keepdims=True) * ROUTED_SCALE

    @pl.when(pl.program_id(1) == 0)
    def _():
        cnt_ref[...] = jnp.zeros(cnt_ref.shape, F32)

    cnt_ref[0] = cnt_ref[0] + chosen.sum(axis=1, keepdims=True)


def _router(h2, wr_t, bias_b, tm):
    B, S, D = h2.shape
    nt = S // tm
    o_spec = pl.BlockSpec((TOP_K, tm), lambda b, i: (0, b * nt + i))
    return pl.pallas_call(
        _router_kernel, grid=(B, nt),
        in_specs=[pl.BlockSpec((1, tm, D), lambda b, i: (b, i, 0)), pl.BlockSpec(wr_t.shape, lambda b, i: (0, 0)),
                  pl.BlockSpec(bias_b.shape, lambda b, i: (0, 0))],
        out_specs=[o_spec, o_spec, pl.BlockSpec((1, N_EXPERTS, LANES), lambda b, i: (b, 0, 0))],
        out_shape=[jax.ShapeDtypeStruct((TOP_K, B * S), I32), jax.ShapeDtypeStruct((TOP_K, B * S), F32),
                   jax.ShapeDtypeStruct((B, N_EXPERTS, LANES), F32)],
        compiler_params=_cparams(("parallel", "arbitrary")),
        name="moe_router",
    )(h2, wr_t, bias_b)


def _experts_kernel(blk_e_ref, row0_ref, nvalid_ref, nblk_ref, skey_ref, gflat_ref, h_hbm, wg_ref, wu_ref, wd_ref,
                    o_hbm, h_vmem, acc_ref, xg0, xg1, y0, y1, sem):
    b = pl.program_id(0)
    last_step = pl.num_programs(0) - 1
    nblk = nblk_ref[0]
    rpt = wg_ref.shape[1] // LANES
    tc = h_vmem.shape[0] // rpt - 1
    n_assign = tc * TOP_K

    def block_rows(blk):
        bc = jnp.clip(blk, 0, nblk - 1)
        return row0_ref[bc], nvalid_ref[bc]

    def assignment(rows, r):
        row0, nv = rows
        key = skey_ref[jnp.minimum(row0 + r, n_assign - 1)]
        return jnp.where(r < nv, key & (n_assign - 1), n_assign)

    def tile_of(a):
        off = (a & ~(TOP_K - 1)) if rpt == TOP_K else (a >> TOP_K_SHIFT) * rpt
        return pl.ds(pl.multiple_of(off, rpt), rpt)

    def gather(blk, xg):
        rows = block_rows(blk)
        for r in range(MOE_BLOCK):
            xg[r * rpt:(r + 1) * rpt, :] = h_vmem[tile_of(assignment(rows, r)), :]

    def scatter(blk, y):
        blk_rows = block_rows(blk)
        for g0 in range(0, MOE_BLOCK, SCATTER_GROUP):
            updated = []
            for r in range(g0, g0 + SCATTER_GROUP):
                a = assignment(blk_rows, r)
                rows = tile_of(a)
                updated.append((rows, acc_ref[rows, :] + gflat_ref[a] * y[r * rpt:(r + 1) * rpt, :]))
            for rows, v in updated:
                acc_ref[rows, :] = v

    @pl.when(b == 0)
    def _():
        cp = pltpu.make_async_copy(h_hbm, h_vmem.at[pl.ds(0, tc * rpt)], sem)
        cp.start()
        h_vmem[pl.ds(tc * rpt, rpt), :] = jnp.zeros((rpt, LANES), F32)
        acc_ref[...] = jnp.zeros(acc_ref.shape, F32)
        y0[...] = jnp.zeros(y0.shape, F32)
        y1[...] = jnp.zeros(y1.shape, F32)
        cp.wait()
        gather(0, xg0)

    def step(xg_cur, xg_nxt, y_cur, y_prv):
        gather(b + 1, xg_nxt)
        xb = jnp.concatenate([xg_cur[pl.ds(j, MOE_BLOCK, stride=rpt), :] for j in range(rpt)], axis=1).astype(BF16)
        gt = _dot(xb, wg_ref[0].astype(BF16))
        up = _dot(xb, wu_ref[0].astype(BF16))
        y = _dot((_silu(gt) * up).astype(BF16), wd_ref[0].astype(BF16))
        for j in range(rpt):
            y_cur[pl.ds(j, MOE_BLOCK, stride=rpt), :] = y[:, j * LANES:(j + 1) * LANES]
        scatter(b - 1, y_prv)

    @pl.when(jnp.logical_and(b <= nblk, (b & 1) == 0))
    def _():
        step(xg0, xg1, y0, y1)

    @pl.when(jnp.logical_and(b <= nblk, (b & 1) == 1))
    def _():
        step(xg1, xg0, y1, y0)

    @pl.when(b == last_step)
    def _():
        cp = pltpu.make_async_copy(acc_ref.at[pl.ds(0, tc * rpt)], o_hbm, sem)
        cp.start()
        cp.wait()


def _routed_experts(h_tiles, blk_e, row0, nvalid, nblk, skey, gflat, w_gate, w_up, w_down):
    D, ff = w_gate.shape[1], w_gate.shape[2]
    rpt = D // LANES
    tc = h_tiles.shape[0] // rpt
    nb = blk_e.shape[0]
    wmap = lambda b, be, r0, nv, nbk, sk, gf: (be[jnp.minimum(b, nbk[0] - 1)], 0, 0)
    wspec = lambda shape: pl.BlockSpec(shape, wmap)
    rows = pltpu.VMEM((MOE_BLOCK * rpt, LANES), F32)
    resident = pltpu.VMEM(((tc + 1) * rpt, LANES), F32)
    grid_spec = pltpu.PrefetchScalarGridSpec(
        num_scalar_prefetch=6, grid=(nb + 1,),
        in_specs=[pl.BlockSpec(memory_space=pl.ANY), wspec((1, D, ff)), wspec((1, D, ff)), wspec((1, ff, D))],
        out_specs=pl.BlockSpec(memory_space=pl.ANY),
        scratch_shapes=[resident, resident, rows, rows, rows, rows, pltpu.SemaphoreType.DMA(())],
    )
    return pl.pallas_call(
        _experts_kernel, grid_spec=grid_spec,
        out_shape=jax.ShapeDtypeStruct((tc * rpt, LANES), F32),
        compiler_params=_cparams(("arbitrary",), vmem=56 * 1024 * 1024),
        name="moe_routed_experts",
    )(blk_e, row0, nvalid, nblk, skey, gflat, h_tiles, w_gate, w_up, w_down)


def _dispatch_tables(eidx, gates, counts, B, S):
    n_assign = S * TOP_K
    assert n_assign & (n_assign - 1) == 0
    e = eidx.reshape(TOP_K, B, S)
    t = lax.broadcasted_iota(I32, e.shape, 2)
    k = lax.broadcasted_iota(I32, e.shape, 0)
    keys = (e * n_assign + t * TOP_K + k).transpose(1, 0, 2).reshape(B, n_assign)
    skey = jnp.stack([lax.sort(keys[b]) for b in range(B)], axis=0)
    seg = jnp.cumsum(counts, axis=1) - counts
    nblocks_e = (counts + MOE_BLOCK - 1) // MOE_BLOCK
    bend = jnp.cumsum(nblocks_e, axis=1)
    bstart = bend - nblocks_e
    nb = n_assign // MOE_BLOCK + N_EXPERTS
    bidx = jnp.arange(nb, dtype=I32)
    blk_e = jnp.minimum(jnp.sum((bend[:, None, :] <= bidx[None, :, None]).astype(I32), axis=-1), N_EXPERTS - 1)
    onehot = (blk_e[:, :, None] == jnp.arange(N_EXPERTS, dtype=I32)[None, None, :]).astype(I32)
    pick = lambda v: jnp.sum(onehot * v[:, None, :], axis=-1)
    off = (bidx[None, :] - pick(bstart)) * MOE_BLOCK
    row0 = pick(seg) + off
    nvalid = jnp.clip(pick(counts) - off, 0, MOE_BLOCK)
    gflat = gates.reshape(TOP_K, B, S).transpose(1, 2, 0).reshape(B, n_assign)
    gflat = jnp.concatenate([gflat, jnp.zeros((B, TOP_K), F32)], axis=1)
    return blk_e, row0, nvalid, bend[:, -1:], skey, gflat


def _shared_kernel(h_ref, wg_ref, wu_ref, wd_ref, routed_ref, x1_ref, gt_ref, o_ref):
    hb = h_ref[0].astype(BF16)
    act = _silu(_dot(hb, wg_ref[...])) * _dot(hb, wu_ref[...])
    shared = _dot(act.astype(BF16), wd_ref[...])
    o_ref[0] = x1_ref[0] + gt_ref[0] * (_from_token_tiles(routed_ref, hb.shape[0]) + shared)


def _shared_and_residual(h2, wsg, wsu, wsd, routed, x1, gt2, tm):
    B, S, D = h2.shape
    row = lambda b, i: (b, i, 0)
    xs = pl.BlockSpec((1, tm, D), row)
    const2 = lambda a: pl.BlockSpec(a.shape, lambda b, i: (0, 0))
    return pl.pallas_call(
        _shared_kernel, grid=(B, S // tm),
        in_specs=[xs, const2(wsg), const2(wsu), const2(wsd), pl.BlockSpec((1, tm * (D // LANES), LANES), row), xs,
                  pl.BlockSpec((1, 1, D), lambda b, i: (b, 0, 0))],
        out_specs=xs,
        out_shape=jax.ShapeDtypeStruct((B, S, D), F32),
        compiler_params=_cparams(("parallel", "parallel")),
        name="shared_expert_residual",
    )(h2, wsg, wsu, wsd, routed, x1, gt2)


def _prep_in_weights(w_in, g_q_dil, g_k_dil, g_q_nsa, g_k_slc, g_k_win):
    D = w_in.shape[0]
    pad = jnp.zeros((D, IN_COLS_PADDED - w_in.shape[1]), w_in.dtype)
    w_all = jnp.concatenate([w_in, pad], axis=1).astype(BF16)
    one = jnp.ones((LANES,), F32)
    gcol = jnp.concatenate([
        jnp.tile(g_q_dil, NH_DIL), jnp.tile(g_k_dil, NH_DIL), jnp.ones((512,), F32), jnp.tile(g_q_nsa, NH_NSA),
        one, one, jnp.tile(g_k_slc, NKV_NSA), one, jnp.tile(g_k_win, NKV_NSA), one, one]).reshape(1, IN_COLS_PADDED)
    return w_all, gcol


def _rope_tables(S):
    inv_freq = ROPE_THETA ** (-jnp.arange(ROT_HALF, dtype=F32) / ROT_HALF)
    ang = jnp.arange(S).astype(F32)[:, None] * inv_freq[None, :]
    cos, sin = jnp.cos(ang), jnp.sin(ang)
    zeros = jnp.zeros((S, HD - ROT_DIM), F32)
    z8 = jnp.zeros((S, ROT_HALF), F32)
    cos_h = jnp.concatenate([cos, cos, jnp.ones((S, HD - ROT_DIM), F32)], axis=1)
    sa_h = jnp.concatenate([-sin, z8, zeros], axis=1)
    sb_h = jnp.concatenate([z8, sin, zeros], axis=1)
    two = lambda t: jnp.concatenate([t, t], axis=1)
    return two(cos_h), two(sa_h), two(sb_h)


def _prep_compress(pe, w1, w2):
    eye = jnp.eye(NKV_NSA, dtype=F32)
    w1r = w1.reshape(CMP_LEN, HD, CMP_HIDDEN)

    def half(w1h, peh):
        w = jnp.einsum("ldh,gk->lgdkh", w1h, eye).reshape(CMP_STRIDE * LANES, NKV_NSA * CMP_HIDDEN)
        p = jnp.broadcast_to(peh[:, None, :], (CMP_STRIDE, NKV_NSA, HD)).reshape(1, CMP_STRIDE * LANES)
        return p, w.astype(BF16)

    pa, wa = half(w1r[:CMP_STRIDE], pe[:CMP_STRIDE])
    pb, wb = half(w1r[CMP_STRIDE:], pe[CMP_STRIDE:])
    w2bd = jnp.einsum("hd,gk->ghkd", w2, eye).reshape(NKV_NSA * CMP_HIDDEN, LANES).astype(BF16)
    return pa, pb, wa, wb, w2bd


def _overlap_tables(S):
    ncb = S // CMP_STRIDE
    n_sel = S // SEL_BLOCK
    cs = np.arange(ncb) * CMP_STRIDE
    ss = np.arange(n_sel) * SEL_BLOCK
    ov = np.clip(np.minimum(cs[:, None] + CMP_LEN, ss[None, :] + SEL_BLOCK) - np.maximum(cs[:, None], ss[None, :]), 0, None)
    ov = ov.astype(np.float32) / CMP_STRIDE
    ovl = np.zeros((ncb, LANES), np.float32)
    ovr = np.zeros((ncb, LANES), np.float32)
    ovl[:, :n_sel] = ov
    ovr[:, HD:HD + n_sel] = ov
    return jnp.asarray(ovl, BF16), jnp.asarray(ovr, BF16)


def _block_ones():
    r = np.arange(LANES)
    return jnp.asarray((r[:, None] // HD == r[None, :] // HD).astype(np.float32), BF16)


def _layer(x, c, w_ada, b_ada, g_norm_mix, g_norm_ffn, w_in, g_q_dil, g_k_dil, g_q_nsa, g_k_cmp, g_k_slc, g_k_win,
           cmp_pe_k, cmp_w1_k, cmp_w2_k, cmp_pe_v, cmp_w1_v, cmp_w2_v, w_out, w_router, router_bias,
           w_gate, w_up, w_down, ws_gate, ws_up, ws_down):
    B, S, D = x.shape
    ones_bd = _block_ones()

    c_pad = jnp.zeros((8, D), F32).at[:B].set(c)
    mod = _ada_mod(c_pad, w_ada, b_ada.reshape(1, -1))[:B]
    sh1, sc1, gt1, sh2, sc2, gt2 = [m.reshape(B, 1, D) for m in jnp.split(mod, 6, axis=-1)]

    w_all, gcol = _prep_in_weights(w_in, g_q_dil, g_k_dil, g_q_nsa, g_k_slc, g_k_win)
    cos_t, sa_t, sb_t = _rope_tables(S)
    qa, ka, va, qn, qr, kc, vc, ksl, vsl, kw, vw, gate = _in_projection(
        x, g_norm_mix.reshape(1, D), sc1, sh1, w_all, gcol, cos_t, sa_t, sb_t, ones_bd, tm=256)

    o_a = _dilated_attention(qa, ka, va)

    kcmp, vcmp = _compress(kc, vc, _prep_compress(cmp_pe_k, cmp_w1_k, cmp_w2_k),
                           _prep_compress(cmp_pe_v, cmp_w1_v, cmp_w2_v), jnp.tile(g_k_cmp, NKV_NSA).reshape(1, LANES),
                           ones_bd)
    ovl, ovr = _overlap_tables(S)
    o_cmp, sel = _cmp_select(qn, kcmp, vcmp, ovl, ovr, gate, tq=128)
    o_cs = _selected_attention(qr, ksl, vsl, sel, o_cmp, gate, tq=256, tk=512)
    gqa = dict(kv_tile=lambda h: 0, kv_half=lambda h: h // NSA_REP)
    o_b = _band_attention(qr, kw, vw, dil=1, tq=256, n_prev=2, max_dist=NSA_WINDOW - 1, mode="gated",
                          extra=(o_cs, gate), out_dtype=BF16, **gqa)

    w_out_b = w_out.astype(BF16)
    x1, h2, h2_tiles = _out_projection(o_a, o_b, w_out_b[:NH_DIL * HD], w_out_b[NH_DIL * HD:], x, gt1,
                                       g_norm_ffn.reshape(1, D), sc2, sh2, tm=256)

    tm_r = 256
    eidx, gates, counts = _router(h2, w_router.T.astype(BF16),
                                  jnp.broadcast_to(router_bias.reshape(N_EXPERTS, 1), (N_EXPERTS, tm_r)), tm=tm_r)
    tables = _dispatch_tables(eidx, gates, counts[:, :, 0].astype(I32), B, S)
    routed = jnp.stack([_routed_experts(h2_tiles[b], *[tbl[b] for tbl in tables], w_gate, w_up, w_down)
                        for b in range(B)], axis=0)
    return _shared_and_residual(h2, ws_gate.astype(BF16), ws_up.astype(BF16), ws_down.astype(BF16), routed, x1, gt2,
                                tm=256)


def kernel(x, c, w_ada, b_ada, g_norm_mix, g_norm_ffn, w_in, g_q_dil, g_k_dil, g_q_nsa, g_k_cmp, g_k_slc, g_k_win, cmp_pe_k, cmp_w1_k, cmp_w2_k, cmp_pe_v, cmp_w1_v, cmp_w2_v, w_out, w_router, router_bias, w_gate, w_up, w_down, ws_gate, ws_up, ws_down):
    params = (w_ada, b_ada, g_norm_mix, g_norm_ffn, w_in, g_q_dil, g_k_dil, g_q_nsa, g_k_cmp, g_k_slc, g_k_win,
              cmp_pe_k, cmp_w1_k, cmp_w2_k, cmp_pe_v, cmp_w1_v, cmp_w2_v, w_out, w_router, router_bias,
              w_gate, w_up, w_down, ws_gate, ws_up, ws_down)
    for layer in range(w_ada.shape[0]):
        x = _layer(x, c, *[a[layer] for a in params])
    return x
```

```python
import functools
import math

import numpy as np
import jax
import jax.numpy as jnp
from jax import lax
from jax.experimental import pallas as pl
from jax.experimental.pallas import tpu as pltpu

F32 = jnp.float32
BF16 = jnp.bfloat16
I32 = jnp.int32

HD = 64
LANES = 128
NH_DIL = 8
NH_NSA = 8
NKV_NSA = 2
NSA_REP = NH_NSA // NKV_NSA
DIL_PAIRS = ((128, 1), (512, 4), (2048, 16))
DIL_TILE = 128
DIL_UNROLL = 8
ROPE_THETA = 500000.0
ROT_DIM = HD // 4
ROT_HALF = ROT_DIM // 2
CMP_LEN = 32
CMP_STRIDE = 16
CMP_HIDDEN = 128
SEL_BLOCK = 64
SEL_SHIFT = 6
SEL_TOP = 16
NSA_WINDOW = 512
N_EXPERTS = 256
TOP_K = 8
TOP_K_SHIFT = 3
N_GROUPS = 8
GROUP_SIZE = N_EXPERTS // N_GROUPS
TOPK_GROUPS = 4
EXPERT_FF = 256
SHARED_FF = 256
ROUTED_SCALE = 2.5
EPS = 1e-6
QK_SCALE = 1.0 / math.sqrt(HD)

NEG = -1e30
BIG = 3e38

C_QA, C_KA, C_VA, C_QB = 0, 512, 1024, 1536
C_KC, C_VC, C_KSL, C_VSL, C_KW, C_VW, C_GB = 2048, 2176, 2304, 2432, 2560, 2688, 2816
IN_COLS_PADDED = 2944
TRANSPOSED_OUTS = (8, 10)

MOE_BLOCK = 128
NULL_ROWS = 8
WEIGHT_RING = 3
SCATTER_GROUP = 16
VMEM_LIMIT = 48 * 1024 * 1024


def _cparams(sem, vmem=VMEM_LIMIT):
    return pltpu.CompilerParams(dimension_semantics=sem, vmem_limit_bytes=vmem)


def _sigmoid(v):
    return 1.0 / (1.0 + jnp.exp(-v))


def _silu(v):
    return v * _sigmoid(v)


def _dot(a, b):
    return jnp.dot(a, b, preferred_element_type=F32)


def _dot_nt(a, b):
    return lax.dot_general(a, b, (((1,), (1,)), ((), ())), preferred_element_type=F32)


def _lane_lo(shape):
    return lax.broadcasted_iota(I32, shape, len(shape) - 1) < HD


def _head_sums(v, ones_bd):
    hi = v.astype(BF16)
    lo = (v - hi.astype(F32)).astype(BF16)
    return _dot(hi, ones_bd) + _dot(lo, ones_bd)


def _head_rmsnorm(y, ones_bd, gain):
    ms = _head_sums(y * y, ones_bd) * (1.0 / HD)
    return y * lax.rsqrt(ms + EPS) * gain


def _rope(y, cos, sin_a, sin_b):
    return y * cos + pltpu.roll(y, LANES - ROT_HALF, 1) * sin_a + pltpu.roll(y, ROT_HALF, 1) * sin_b


def _ada_kernel(c_ref, w_ref, b_ref, o_ref):
    a = _silu(c_ref[...]).astype(BF16)
    o_ref[...] = _dot(a, w_ref[...].astype(BF16)) + b_ref[...]


def _ada_mod(c_pad, w_ada, b_ada):
    rows, d = c_pad.shape
    n = w_ada.shape[1]
    tn = 1536 if n % 1536 == 0 else n
    return pl.pallas_call(
        _ada_kernel,
        grid=(n // tn,),
        in_specs=[pl.BlockSpec((rows, d), lambda j: (0, 0)),
                  pl.BlockSpec((d, tn), lambda j: (0, j)),
                  pl.BlockSpec((1, tn), lambda j: (0, j))],
        out_specs=pl.BlockSpec((rows, tn), lambda j: (0, j)),
        out_shape=jax.ShapeDtypeStruct((rows, n), F32),
        compiler_params=_cparams(("arbitrary",)),
        name="ada_mod",
    )(c_pad, w_ada, b_ada)


def _inproj_kernel(x_ref, g_ref, sc_ref, sh_ref, w_ref, gcol_ref, cos_ref, sa_ref, sb_ref, ones_ref,
                   qa_ref, ka_ref, va_ref, qn_ref, qr_ref, kc_ref, vc_ref, ksl_ref, vsl_ref, kw_ref, vw_ref,
                   gate_ref):
    x = x_ref[0]
    ms = jnp.mean(x * x, axis=-1, keepdims=True)
    h = (x * lax.rsqrt(ms + EPS)) * g_ref[...]
    h = h * (1.0 + sc_ref[0]) + sh_ref[0]
    hb = h.astype(BF16)
    ones_bd = ones_ref[...]
    cos, sa, sb = cos_ref[...], sa_ref[...], sb_ref[...]
    lo = _lane_lo(cos.shape)

    def proj(c0, width):
        return _dot(hb, w_ref[:, c0:c0 + width])

    def normed(tile, c0):
        return _head_rmsnorm(tile, ones_bd, gcol_ref[:, c0:c0 + LANES])

    acc = proj(C_QA, 512)
    for p in range(4):
        y = _rope(normed(acc[:, p * LANES:(p + 1) * LANES], C_QA + p * LANES), cos, sa, sb) * QK_SCALE
        qa_ref[0, :, p * LANES:(p + 1) * LANES] = y
    acc = proj(C_KA, 512)
    for p in range(4):
        y = _rope(normed(acc[:, p * LANES:(p + 1) * LANES], C_KA + p * LANES), cos, sa, sb)
        ka_ref[0, :, p * LANES:(p + 1) * LANES] = y
    va_ref[0] = proj(C_VA, 512)

    acc = proj(C_QB, 512)
    for p in range(4):
        yn = normed(acc[:, p * LANES:(p + 1) * LANES], C_QB + p * LANES)
        yr = _rope(yn, cos, sa, sb)
        for y, ref in ((yn * QK_SCALE, qn_ref), (yr * QK_SCALE, qr_ref)):
            ysw = pltpu.roll(y, HD, 1)
            for half in range(2):
                head = 2 * p + half
                grp = head // NSA_REP
                src = y if grp == half else ysw
                keep = lo if grp == 0 else jnp.logical_not(lo)
                ref[0, :, head * LANES:(head + 1) * LANES] = jnp.where(keep, src, 0.0).astype(BF16)

    acc = proj(C_KC, 512)
    kc_ref[0] = acc[:, 0:LANES]
    vc_ref[0] = acc[:, LANES:2 * LANES]
    ksl_ref[0] = _rope(normed(acc[:, 2 * LANES:3 * LANES], C_KSL), cos, sa, sb).astype(BF16)
    vsl_ref[0] = _transposed(acc[:, 3 * LANES:4 * LANES]).astype(BF16)
    acc = proj(C_KW, 384)
    kw_ref[0] = _rope(normed(acc[:, 0:LANES], C_KW), cos, sa, sb).astype(BF16)
    vw_ref[0] = _transposed(acc[:, LANES:2 * LANES]).astype(BF16)
    gate_ref[0] = _sigmoid(acc[:, 2 * LANES:3 * LANES])


def _transposed(tile):
    return jnp.concatenate([tile[r0:r0 + LANES].T for r0 in range(0, tile.shape[0], LANES)], axis=1)


def _in_projection(x, g_mix, sc1, sh1, w_all, gcol, cos_t, sa_t, sb_t, ones_bd, tm):
    B, S, D = x.shape
    nc = w_all.shape[1]
    row = lambda b, i: (b, i, 0)
    bvec = pl.BlockSpec((1, 1, D), lambda b, i: (b, 0, 0))
    tab = pl.BlockSpec((tm, LANES), lambda b, i: (i, 0))
    const2 = lambda shape: pl.BlockSpec(shape, lambda b, i: (0, 0))
    widths_dt = [(512, F32), (512, F32), (512, F32), (1024, BF16), (1024, BF16), (LANES, F32), (LANES, F32),
                 (LANES, BF16), (LANES, BF16), (LANES, BF16), (LANES, BF16), (LANES, F32)]
    return pl.pallas_call(
        _inproj_kernel,
        grid=(B, S // tm),
        in_specs=[pl.BlockSpec((1, tm, D), row), const2((1, D)), bvec, bvec, const2((D, nc)), const2((1, nc)),
                  tab, tab, tab, const2((LANES, LANES))],
        out_specs=[pl.BlockSpec((1, tm, w), row) if n not in TRANSPOSED_OUTS else pl.BlockSpec((1, w, tm), lambda b, i: (b, 0, i))
                   for n, (w, _) in enumerate(widths_dt)],
        out_shape=[jax.ShapeDtypeStruct((B, S, w) if n not in TRANSPOSED_OUTS else (B, w, S), dt)
                   for n, (w, dt) in enumerate(widths_dt)],
        compiler_params=_cparams(("parallel", "parallel")),
        name="in_projection",
    )(x, g_mix, sc1, sh1, w_all, gcol, cos_t, sa_t, sb_t, ones_bd)


def _dot_tn(a, b):
    return lax.dot_general(a, b, (((0,), (0,)), ((), ())), preferred_element_type=F32)


def _dilated_kernel(q_ref, k_ref, v_ref, o_ref, oacc, lacc, *, sb_rows, pairs):
    sb = pl.program_id(2)
    base = sb * sb_rows
    t = DIL_TILE
    kk = lax.broadcasted_iota(I32, (t, t), 0)
    qq = lax.broadcasted_iota(I32, (t, t), 1)
    bias_cur = jnp.where(qq >= kk, 0.0, NEG)
    bias_prev = jnp.where(qq <= kk, 0.0, NEG)
    lane_lo = _lane_lo((t, LANES))
    row_lo = lax.broadcasted_iota(I32, (LANES, t), 0) < HD

    def block(d, qs, first, mode):
        qg = base + qs
        rows_q = pl.ds(qs, t, stride=d) if d > 1 else pl.ds(qs, t)

        def strided(ref, start):
            return ref[0, pl.ds(start, t, stride=d), :] if d > 1 else ref[0, pl.ds(start, t), :]

        prev_start = jnp.where(first, qg, qg - t * d)
        qt = q_ref[0, rows_q, :].astype(BF16)
        zero = jnp.zeros_like(qt)
        q2 = jnp.concatenate([jnp.where(lane_lo, qt, zero), jnp.where(lane_lo, zero, qt)], axis=0)
        k2 = jnp.concatenate([strided(k_ref, prev_start), strided(k_ref, qg)], axis=0).astype(BF16)
        v2 = jnp.concatenate([strided(v_ref, prev_start), strided(v_ref, qg)], axis=0).astype(BF16)
        bias = jnp.concatenate([jnp.where(first, NEG, bias_prev), bias_cur], axis=0)
        s = _dot_nt(k2, q2) + jnp.concatenate([bias, bias], axis=1)
        m = s.max(axis=0, keepdims=True)
        p = jnp.exp(s - m)
        l = p.sum(axis=0, keepdims=True)
        ov = _dot_tn(v2, p.astype(BF16)) / l
        lse2 = m + jnp.log(l)
        o = jnp.where(row_lo, ov[:, :t], ov[:, t:]).T
        lse = jnp.where(row_lo, lse2[:, :t], lse2[:, t:]).T
        if mode == "init":
            oacc[rows_q, :] = o
            lacc[rows_q, :] = lse
            return
        lp = lacc[rows_q, :]
        mx = jnp.maximum(lp, lse)
        wp, wn = jnp.exp(lp - mx), jnp.exp(lse - mx)
        den = wp + wn
        oacc[rows_q, :] = (oacc[rows_q, :] * wp + o * wn) / den
        if mode == "mid":
            lacc[rows_q, :] = mx + jnp.log(den)

    modes = ("init",) + ("mid",) * (len(pairs) - 2) + ("last",)
    for (window, d), mode in zip(pairs, modes):
        assert window // d == t and d & (d - 1) == 0
        n_j = sb_rows // (t * d)

        def body(i, c, d=d, mode=mode):
            r, j = i & (d - 1), i >> (d.bit_length() - 1)
            block(d, j * (t * d) + r, jnp.logical_and(sb == 0, j == 0), mode)
            return c

        lax.fori_loop(0, d * n_j, body, 0, unroll=DIL_UNROLL)
    o_ref[0] = oacc[...].astype(o_ref.dtype)


def _dilated_attention(qa, ka, va):
    B, S, w = qa.shape
    sb_rows = DIL_TILE * max(d for _, d in DIL_PAIRS)
    assert S % sb_rows == 0
    kern = functools.partial(_dilated_kernel, sb_rows=sb_rows, pairs=DIL_PAIRS)
    whole = pl.BlockSpec((1, S, LANES), lambda b, p, s: (b, 0, p))
    blk = pl.BlockSpec((1, sb_rows, LANES), lambda b, p, s: (b, s, p))
    acc = pltpu.VMEM((sb_rows, LANES), F32)
    return pl.pallas_call(
        kern, grid=(B, w // LANES, S // sb_rows),
        in_specs=[blk, whole, whole], out_specs=blk,
        out_shape=jax.ShapeDtypeStruct((B, S, w), BF16),
        scratch_shapes=[acc, acc],
        compiler_params=_cparams(("parallel", "parallel", "arbitrary")),
        name="dilated_attention",
    )(qa, ka, va)


def _compress_kernel(kc_ref, vc_ref, pak_ref, pbk_ref, wak_ref, wbk_ref, w2k_ref, gk_ref, ones_ref,
                     pav_ref, pbv_ref, wav_ref, wbv_ref, w2v_ref, ko_ref, vo_ref):
    ncb = kc_ref.shape[1]

    def branch(t_ref, pa, pb, wa, wb, w2):
        t = t_ref[0]
        ua = _dot((t + pa[...]).astype(BF16), wa[...])
        ub = _dot((t + pb[...]).astype(BF16), wb[...])
        pre = ua + pltpu.roll(ub, ncb - 1, 0)
        return _dot(jax.nn.gelu(pre).astype(BF16), w2[...])

    live = lax.broadcasted_iota(I32, (ncb, LANES), 0) < ncb - 1
    kcmp = _head_rmsnorm(branch(kc_ref, pak_ref, pbk_ref, wak_ref, wbk_ref, w2k_ref), ones_ref[...], gk_ref[...])
    ko_ref[0] = jnp.where(live, kcmp, 0.0).astype(BF16)
    vo_ref[0] = jnp.where(live, branch(vc_ref, pav_ref, pbv_ref, wav_ref, wbv_ref, w2v_ref), 0.0).astype(BF16)


def _compress(kc, vc, wk, wv, gk2, ones_bd):
    B, S, _ = kc.shape
    ncb = S // CMP_STRIDE
    cw = CMP_STRIDE * LANES
    view = lambda a: a.reshape(B, ncb, cw)
    blk = pl.BlockSpec((1, ncb, cw), lambda b: (b, 0, 0))
    full = lambda a: pl.BlockSpec(a.shape, lambda b: (0,) * a.ndim)
    consts = [*wk, gk2, ones_bd, *wv]
    o_spec = pl.BlockSpec((1, ncb, LANES), lambda b: (b, 0, 0))
    return pl.pallas_call(
        _compress_kernel, grid=(B,),
        in_specs=[blk, blk] + [full(a) for a in consts],
        out_specs=[o_spec, o_spec],
        out_shape=[jax.ShapeDtypeStruct((B, ncb, LANES), BF16)] * 2,
        compiler_params=_cparams(("parallel",)),
        name="nsa_compress",
    )(view(kc), view(vc), *consts)


def _pair_tile(lo, t0, half0, t1, half1):
    a = t0 if half0 == 0 else pltpu.roll(t0, HD, 1)
    b = t1 if half1 == 1 else pltpu.roll(t1, HD, 1)
    return jnp.where(lo, a, b)


def _cmpsel_kernel(qn_ref, kc_ref, vc_ref, ovl_ref, ovr_ref, gate_ref, o_ref, sel_ref, *, tq, n_sel, top):
    ncb = kc_ref.shape[1]
    i = pl.program_id(1)
    pos0 = i * tq
    qpos = pos0 + lax.broadcasted_iota(I32, (tq, ncb), 0)
    cend = lax.broadcasted_iota(I32, (tq, ncb), 1) * CMP_STRIDE + (CMP_LEN - 1)
    valid = cend <= qpos
    lo = _lane_lo((tq, LANES))
    kc = kc_ref[0]
    vc = vc_ref[0]

    outs = []
    psum = [None] * NKV_NSA
    for h in range(NH_NSA):
        g = h // NSA_REP
        s = _dot_nt(qn_ref[0, :, h * LANES:(h + 1) * LANES], kc)
        m = jnp.where(valid, s, NEG).max(axis=1, keepdims=True)
        m = jnp.where(m > 0.5 * NEG, m, 0.0)
        p = jnp.where(valid, jnp.exp(s - m), 0.0)
        l = p.sum(axis=1, keepdims=True)
        p = p / jnp.where(l > 0.0, l, 1.0)
        outs.append(_dot(p.astype(BF16), vc))
        psum[g] = p if psum[g] is None else psum[g] + p

    for p_ in range(NH_NSA // 2):
        h0, h1 = 2 * p_, 2 * p_ + 1
        o_pair = _pair_tile(lo, outs[h0], h0 // NSA_REP, outs[h1], h1 // NSA_REP)
        g = jnp.where(lo, gate_ref[0, :, 3 * h0:3 * h0 + 1], gate_ref[0, :, 3 * h1:3 * h1 + 1])
        o_ref[0, :, p_ * LANES:(p_ + 1) * LANES] = g * o_pair

    def hilo_dot(a, w):
        hi = a.astype(BF16)
        return _dot(hi, w) + _dot((a - hi.astype(F32)).astype(BF16), w)

    pslc = hilo_dot(psum[0], ovl_ref[...]) + hilo_dot(psum[1], ovr_ref[...])
    pslc_t = pslc.T
    jblk = lax.broadcasted_iota(I32, (HD, tq), 0)
    cur = (pos0 + lax.broadcasted_iota(I32, (HD, tq), 1)) >> SEL_SHIFT
    forced = (jblk == 0) | (jblk == cur) | (jblk == cur - 1)
    sel_t = []
    for g in range(NKV_NSA):
        sc = jnp.where(forced, BIG, jnp.where(jblk <= cur, pslc_t[g * HD:(g + 1) * HD, :], -BIG))
        rank = jnp.zeros((HD, tq), F32)
        for k in range(n_sel):
            rk = sc[k:k + 1, :]
            rank = rank + jnp.where(jblk > k, jnp.where(rk >= sc, 1.0, 0.0), jnp.where(rk > sc, 1.0, 0.0))
        sel_t.append(jnp.where(rank < top, 1.0, 0.0))
    sel_ref[0] = jnp.concatenate(sel_t, axis=0)


def _cmp_select(qn, kcmp, vcmp, ovl, ovr, gate, tq):
    B, S, qw = qn.shape
    ncb = kcmp.shape[1]
    n_sel = S // SEL_BLOCK
    assert n_sel <= HD and tq == LANES
    row = lambda b, i: (b, i, 0)
    cb = pl.BlockSpec((1, ncb, LANES), lambda b, i: (b, 0, 0))
    ovs = pl.BlockSpec((ncb, LANES), lambda b, i: (0, 0))
    kern = functools.partial(_cmpsel_kernel, tq=tq, n_sel=n_sel, top=min(SEL_TOP, n_sel))
    return pl.pallas_call(
        kern, grid=(B, S // tq),
        in_specs=[pl.BlockSpec((1, tq, qw), row), cb, cb, ovs, ovs, pl.BlockSpec((1, tq, LANES), row)],
        out_specs=[pl.BlockSpec((1, tq, NH_NSA * HD), row), pl.BlockSpec((1, LANES, tq), lambda b, i: (b, 0, i))],
        out_shape=[jax.ShapeDtypeStruct((B, S, NH_NSA * HD), F32), jax.ShapeDtypeStruct((B, LANES, S), F32)],
        compiler_params=_cparams(("parallel", "parallel")),
        name="nsa_cmp_select",
    )(qn, kcmp, vcmp, ovl, ovr, gate)


def _nsa_kernel(*refs, tq, tk, mode):
    if mode == "selected":
        q_ref, k_ref, vt_ref, selt_ref, prev_ref, gate_ref, o_ref, m_ref, l_ref, acc_ref = refs
    else:
        q_ref, k_ref, vt_ref, prev_ref, gate_ref, o_ref, m_ref, l_ref, acc_ref = refs
    i = pl.program_id(1)
    kk = pl.program_id(2)
    nkv = pl.num_programs(2)
    if mode == "selected":
        tile = kk
        active = kk <= (i * tq + tq - 1) // tk
    else:
        tile = i * (tq // tk) + kk - (nkv - tq // tk)
        active = tile >= 0

    @pl.when(kk == 0)
    def _():
        m_ref[...] = jnp.full(m_ref.shape, NEG, F32)
        l_ref[...] = jnp.zeros(l_ref.shape, F32)
        acc_ref[...] = jnp.zeros(acc_ref.shape, F32)

    @pl.when(active)
    def _():
        kpos = tile * tk + lax.broadcasted_iota(I32, (tk, tq), 0)
        qpos = i * tq + lax.broadcasted_iota(I32, (tk, tq), 1)
        k = k_ref[0]
        vt = vt_ref[0]
        if mode == "window":
            dist = qpos - kpos
            bias_w = jnp.where(dist >= 0, jnp.where(dist < NSA_WINDOW, 0.0, NEG), NEG)
        for g in range(NKV_NSA):
            if mode == "selected":
                nblk = tk // SEL_BLOCK
                rows = selt_ref[0, pl.ds(pl.multiple_of(g * HD + kk * nblk, nblk), nblk), :]
                chosen = jnp.broadcast_to(rows[:, None, :], (nblk, SEL_BLOCK, tq)).reshape(tk, tq) > 0.5
                bias = jnp.where(kpos <= qpos, jnp.where(chosen, 0.0, NEG), NEG)
            else:
                bias = bias_w
            heads = range(g * NSA_REP, (g + 1) * NSA_REP)
            q4 = jnp.concatenate([q_ref[0, :, h * LANES:(h + 1) * LANES] for h in heads], axis=0)
            s = _dot_nt(k, q4) + jnp.concatenate([bias] * NSA_REP, axis=1)
            m_prev = m_ref[g]
            m_new = jnp.maximum(m_prev, s.max(axis=0, keepdims=True))
            alpha = jnp.exp(m_prev - m_new)
            p = jnp.exp(s - m_new[0:1])
            l_ref[g] = alpha * l_ref[g] + p.sum(axis=0, keepdims=True)
            acc_ref[g] = alpha[0:1] * acc_ref[g] + _dot(vt, p.astype(BF16))
            m_ref[g] = m_new

    @pl.when(kk == nkv - 1)
    def _():
        lo = _lane_lo((tq, LANES))
        gate_col = 1 if mode == "selected" else 2

        def head_t(h):
            g, r = divmod(h, NSA_REP)
            cols = slice(r * tq, (r + 1) * tq)
            return acc_ref[g, g * HD:(g + 1) * HD, cols] / l_ref[g, 0:1, cols]

        for p_ in range(NH_NSA // 2):
            h0, h1 = 2 * p_, 2 * p_ + 1
            o_pair = jnp.concatenate([head_t(h0), head_t(h1)], axis=0).T
            c0, c1 = 3 * h0 + gate_col, 3 * h1 + gate_col
            gt = jnp.where(lo, gate_ref[0, :, c0:c0 + 1], gate_ref[0, :, c1:c1 + 1])
            sl = slice(p_ * LANES, (p_ + 1) * LANES)
            o_ref[0, :, sl] = (prev_ref[0, :, sl] + gt * o_pair).astype(o_ref.dtype)


def _nsa_branch(qr, k, v_t, sel_t, prev, gate, *, tq, tk, mode, out_dtype):
    B, S, qw = qr.shape
    row = lambda b, i, kk: (b, i, 0)
    ow = NH_NSA * HD
    if mode == "selected":
        n_steps = S // tk
        tile = lambda i, kk: jnp.minimum(kk, (i * tq + tq - 1) // tk)
    else:
        assert tq % tk == 0
        n_steps = pl.cdiv(NSA_WINDOW - 1, tk) + tq // tk
        tile = lambda i, kk: jnp.maximum(i * (tq // tk) + kk - (n_steps - tq // tk), 0)
    in_specs = [pl.BlockSpec((1, tq, qw), row),
                pl.BlockSpec((1, tk, LANES), lambda b, i, kk: (b, tile(i, kk), 0)),
                pl.BlockSpec((1, LANES, tk), lambda b, i, kk: (b, 0, tile(i, kk)))]
    args = [qr, k, v_t]
    if mode == "selected":
        in_specs.append(pl.BlockSpec((1, LANES, tq), lambda b, i, kk: (b, 0, i)))
        args.append(sel_t)
    in_specs += [pl.BlockSpec((1, tq, ow), row), pl.BlockSpec((1, tq, LANES), row)]
    args += [prev, gate]
    stat = pltpu.VMEM((NKV_NSA, 8, NSA_REP * tq), F32)
    return pl.pallas_call(
        functools.partial(_nsa_kernel, tq=tq, tk=tk, mode=mode), grid=(B, S // tq, n_steps),
        in_specs=in_specs,
        out_specs=pl.BlockSpec((1, tq, ow), row),
        out_shape=jax.ShapeDtypeStruct((B, S, ow), out_dtype),
        scratch_shapes=[stat, stat, pltpu.VMEM((NKV_NSA, LANES, NSA_REP * tq), F32)],
        compiler_params=_cparams(("parallel", "parallel", "arbitrary")),
        name=f"nsa_{mode}_attention",
    )(*args)


def _to_token_tiles(ref, value):
    m, d = value.shape
    rpt = d // LANES
    for j in range(rpt):
        ref[0, pl.ds(j, m, stride=rpt), :] = value[:, j * LANES:(j + 1) * LANES]


def _from_token_tiles(ref, m):
    rpt = ref.shape[1] // m
    return jnp.concatenate([ref[0, pl.ds(j, m, stride=rpt), :] for j in range(rpt)], axis=1)


def _outproj_kernel(oa_ref, ob_ref, wt_ref, wb_ref, x_ref, gt_ref, g_ref, sc_ref, sh_ref, x1_ref, h2_ref, h2t_ref):
    mix = _dot(oa_ref[0], wt_ref[...]) + _dot(ob_ref[0], wb_ref[...])
    x1 = x_ref[0] + gt_ref[0] * mix
    x1_ref[0] = x1
    ms = jnp.mean(x1 * x1, axis=-1, keepdims=True)
    h = (x1 * lax.rsqrt(ms + EPS)) * g_ref[...]
    h2 = h * (1.0 + sc_ref[0]) + sh_ref[0]
    h2_ref[0] = h2
    _to_token_tiles(h2t_ref, h2)


def _out_projection(oa, ob, w_top, w_bot, x, gt1, g_ffn, sc2, sh2, tm):
    B, S, D = x.shape
    row = lambda b, i: (b, i, 0)
    bvec = pl.BlockSpec((1, 1, D), lambda b, i: (b, 0, 0))
    const2 = lambda a: pl.BlockSpec(a.shape, lambda b, i: (0, 0))
    xs = pl.BlockSpec((1, tm, D), row)
    return pl.pallas_call(
        _outproj_kernel, grid=(B, S // tm),
        in_specs=[pl.BlockSpec((1, tm, oa.shape[2]), row), pl.BlockSpec((1, tm, ob.shape[2]), row),
                  const2(w_top), const2(w_bot), xs, bvec, const2(g_ffn), bvec, bvec],
        out_specs=[xs, xs, pl.BlockSpec((1, tm * (D // LANES), LANES), row)],
        out_shape=[jax.ShapeDtypeStruct((B, S, D), F32)] * 2 + [jax.ShapeDtypeStruct((B, S * (D // LANES), LANES), F32)],
        compiler_params=_cparams(("parallel", "parallel")),
        name="out_projection",
    )(oa, ob, w_top, w_bot, x, gt1, g_ffn, sc2, sh2)


def _router_kernel(h_ref, wr_ref, bias_ref, eidx_ref, gate_ref, cnt_ref):
    tm = h_ref.shape[1]
    logits = _dot_nt(wr_ref[...], h_ref[0].astype(BF16))
    aff = _sigmoid(logits)
    biased = aff + bias_ref[...]
    b3 = biased.reshape(N_GROUPS, GROUP_SIZE, tm)
    jj = lax.broadcasted_iota(I32, (N_GROUPS, GROUP_SIZE, tm), 1).astype(F32)
    m1 = b3.max(axis=1, keepdims=True)
    i1 = jnp.where(b3 == m1, jj, 1e9).min(axis=1, keepdims=True)
    m2 = jnp.where(jj == i1, -jnp.inf, b3).max(axis=1, keepdims=True)
    gs = (m1 + m2).reshape(N_GROUPS, tm)
    gi = lax.broadcasted_iota(I32, (N_GROUPS, tm), 0)
    rank = jnp.zeros((N_GROUPS, tm), F32)
    for k in range(N_GROUPS):
        rk = gs[k:k + 1, :]
        rank = rank + jnp.where(gi > k, jnp.where(rk >= gs, 1.0, 0.0), jnp.where(rk > gs, 1.0, 0.0))
    gsel = (rank < TOPK_GROUPS).reshape(N_GROUPS, 1, tm)
    masked = jnp.where(gsel, b3, -jnp.inf).reshape(N_EXPERTS, tm)
    ee = lax.broadcasted_iota(I32, (N_EXPERTS, tm), 0).astype(F32)
    idx_rows, sel_rows = [], []
    chosen = jnp.zeros((N_EXPERTS, tm), F32)
    for _ in range(TOP_K):
        m = masked.max(axis=0, keepdims=True)
        idx = jnp.where(masked == m, ee, 1e9).min(axis=0, keepdims=True)
        hit = ee == idx
        idx_rows.append(idx)
        sel_rows.append(jnp.where(hit, aff, 0.0).sum(axis=0, keepdims=True))
        masked = jnp.where(hit, -jnp.inf, masked)
        chosen = chosen + jnp.where(hit, 1.0, 0.0)
    sel = jnp.concatenate(sel_rows, axis=0)
    eidx_ref[...] = jnp.concatenate(idx_rows, axis=0).astype(I32)
    gate_ref[...] = sel / sel.sum(axis=0, keepdims=True) * ROUTED_SCALE

    @pl.when(pl.program_id(1) == 0)
    def _():
        cnt_ref[...] = jnp.zeros(cnt_ref.shape, F32)

    cnt_ref[0] = cnt_ref[0] + chosen.sum(axis=1, keepdims=True)


def _router(h2, wr_t, bias_b, tm):
    B, S, D = h2.shape
    nt = S // tm
    o_spec = pl.BlockSpec((TOP_K, tm), lambda b, i: (0, b * nt + i))
    return pl.pallas_call(
        _router_kernel, grid=(B, nt),
        in_specs=[pl.BlockSpec((1, tm, D), lambda b, i: (b, i, 0)), pl.BlockSpec(wr_t.shape, lambda b, i: (0, 0)),
                  pl.BlockSpec(bias_b.shape, lambda b, i: (0, 0))],
        out_specs=[o_spec, o_spec, pl.BlockSpec((1, N_EXPERTS, LANES), lambda b, i: (b, 0, 0))],
        out_shape=[jax.ShapeDtypeStruct((TOP_K, B * S), I32), jax.ShapeDtypeStruct((TOP_K, B * S), F32),
                   jax.ShapeDtypeStruct((B, N_EXPERTS, LANES), F32)],
        compiler_params=_cparams(("parallel", "arbitrary")),
        name="moe_router",
    )(h2, wr_t, bias_b)


def _experts_kernel(ord_ref, uexp_ref, nused_ref, row0_ref, nvalid_ref, nblk_ref, skey_ref, gflat_ref,
                    h_hbm, wg_hbm, wu_hbm, wd_hbm, o_hbm, h_vmem, acc_ref, xg0, xg1, y0, y1, wg_ring, wu_ring, wd_ring,
                    sem, wsem):
    b = pl.program_id(0)
    last_step = pl.num_programs(0) - 1
    nblk = nblk_ref[0]
    rpt = wg_hbm.shape[1] // LANES
    tc = h_vmem.shape[0] // rpt - 1
    n_assign = tc * TOP_K
    depth = wg_ring.shape[0]

    def weight_copies(j):
        e, slot = uexp_ref[j], j % depth
        return [pltpu.make_async_copy(hbm.at[e], ring.at[slot], wsem.at[n, slot])
                for n, (hbm, ring) in enumerate(((wg_hbm, wg_ring), (wu_hbm, wu_ring), (wd_hbm, wd_ring)))]

    def start_weights(j):
        @pl.when(j < nused_ref[0])
        def _():
            for cp in weight_copies(j):
                cp.start()

    def block_rows(blk):
        bc = jnp.clip(blk, 0, nblk - 1)
        return row0_ref[bc], nvalid_ref[bc]

    def assignment(rows, r, checked=True):
        row0, nv = rows
        a = skey_ref[row0 + r] & (n_assign - 1)
        return jnp.where(r < nv, a, n_assign) if checked else a

    def tile_of(a):
        off = (a & ~(TOP_K - 1)) if rpt == TOP_K else (a >> TOP_K_SHIFT) * rpt
        return pl.ds(pl.multiple_of(off, rpt), rpt)

    def gather(blk, xg):
        rows = block_rows(blk)
        for r in range(MOE_BLOCK):
            xg[r * rpt:(r + 1) * rpt, :] = h_vmem[tile_of(assignment(rows, r, checked=False)), :]

    def scatter(blk, y):
        blk_rows = block_rows(blk)
        for g0 in range(0, MOE_BLOCK, SCATTER_GROUP):
            updated = []
            for r in range(g0, g0 + SCATTER_GROUP):
                a = assignment(blk_rows, r)
                rows = tile_of(a)
                updated.append((rows, acc_ref[rows, :] + gflat_ref[a] * y[r * rpt:(r + 1) * rpt, :]))
            for rows, v in updated:
                acc_ref[rows, :] = v

    @pl.when(b == 0)
    def _():
        cp = pltpu.make_async_copy(h_hbm, h_vmem.at[pl.ds(0, tc * rpt)], sem)
        cp.start()
        for j in range(depth - 1):
            start_weights(j)
        h_vmem[pl.ds(tc * rpt, rpt), :] = jnp.zeros((rpt, LANES), F32)
        acc_ref[...] = jnp.zeros(acc_ref.shape, F32)
        y0[...] = jnp.zeros(y0.shape, F32)
        y1[...] = jnp.zeros(y1.shape, F32)
        cp.wait()
        gather(0, xg0)

    cur = ord_ref[jnp.minimum(b, nblk - 1)]

    @pl.when(jnp.logical_and(b < nblk, jnp.logical_or(b == 0, ord_ref[jnp.maximum(b, 1) - 1] != cur)))
    def _():
        for cp in weight_copies(cur):
            cp.wait()
        start_weights(cur + depth - 1)

    def step(xg_cur, xg_nxt, y_cur, y_prv):
        slot = cur % depth
        gather(b + 1, xg_nxt)
        xb = jnp.concatenate([xg_cur[pl.ds(j, MOE_BLOCK, stride=rpt), :] for j in range(rpt)], axis=1).astype(BF16)
        gt = _dot(xb, wg_ring[slot].astype(BF16))
        up = _dot(xb, wu_ring[slot].astype(BF16))
        y = _dot((_silu(gt) * up).astype(BF16), wd_ring[slot].astype(BF16))
        for j in range(rpt):
            y_cur[pl.ds(j, MOE_BLOCK, stride=rpt), :] = y[:, j * LANES:(j + 1) * LANES]
        scatter(b - 1, y_prv)

    @pl.when(jnp.logical_and(b <= nblk, (b & 1) == 0))
    def _():
        step(xg0, xg1, y0, y1)

    @pl.when(jnp.logical_and(b <= nblk, (b & 1) == 1))
    def _():
        step(xg1, xg0, y1, y0)

    @pl.when(b == last_step)
    def _():
        cp = pltpu.make_async_copy(acc_ref.at[pl.ds(0, tc * rpt)], o_hbm, sem)
        cp.start()
        cp.wait()


def _routed_experts(h_tiles, blk_ord, uexp, nused, row0, nvalid, nblk, skey, gflat, w_gate, w_up, w_down):
    D, ff = w_gate.shape[1], w_gate.shape[2]
    rpt = D // LANES
    tc = h_tiles.shape[0] // rpt
    nb = blk_ord.shape[0]
    hbm = pl.BlockSpec(memory_space=pl.ANY)
    rows = pltpu.VMEM((MOE_BLOCK * rpt, LANES), F32)
    resident = pltpu.VMEM(((tc + 1) * rpt, LANES), F32)
    grid_spec = pltpu.PrefetchScalarGridSpec(
        num_scalar_prefetch=8, grid=(nb + 1,),
        in_specs=[hbm, hbm, hbm, hbm],
        out_specs=hbm,
        scratch_shapes=[resident, resident, rows, rows, rows, rows,
                        pltpu.VMEM((WEIGHT_RING, D, ff), F32), pltpu.VMEM((WEIGHT_RING, D, ff), F32),
                        pltpu.VMEM((WEIGHT_RING, ff, D), F32),
                        pltpu.SemaphoreType.DMA(()), pltpu.SemaphoreType.DMA((3, WEIGHT_RING))],
    )
    return pl.pallas_call(
        _experts_kernel, grid_spec=grid_spec,
        out_shape=jax.ShapeDtypeStruct((tc * rpt, LANES), F32),
        compiler_params=_cparams(("arbitrary",), vmem=56 * 1024 * 1024),
        name="moe_routed_experts",
    )(blk_ord, uexp, nused, row0, nvalid, nblk, skey, gflat, h_tiles, w_gate, w_up, w_down)


def _dispatch_tables(eidx, gates, counts, B, S):
    n_assign = S * TOP_K
    assert n_assign & (n_assign - 1) == 0
    e = eidx.reshape(TOP_K, B, S)
    t = lax.broadcasted_iota(I32, e.shape, 2)
    k = lax.broadcasted_iota(I32, e.shape, 0)
    keys = (e * n_assign + t * TOP_K + k).transpose(1, 0, 2).reshape(B, n_assign)
    spare = jnp.zeros((MOE_BLOCK,), I32)
    skey = jnp.stack([jnp.concatenate([lax.sort(keys[b]), spare]) for b in range(B)], axis=0)
    seg = jnp.cumsum(counts, axis=1) - counts
    nblocks_e = (counts + MOE_BLOCK - 1) // MOE_BLOCK
    bend = jnp.cumsum(nblocks_e, axis=1)
    bstart = bend - nblocks_e
    nb = n_assign // MOE_BLOCK + N_EXPERTS
    bidx = jnp.arange(nb, dtype=I32)
    blk_e = jnp.minimum(jnp.sum((bend[:, None, :] <= bidx[None, :, None]).astype(I32), axis=-1), N_EXPERTS - 1)
    onehot = (blk_e[:, :, None] == jnp.arange(N_EXPERTS, dtype=I32)[None, None, :]).astype(I32)
    pick = lambda v: jnp.sum(onehot * v[:, None, :], axis=-1)
    off = (bidx[None, :] - pick(bstart)) * MOE_BLOCK
    row0 = pick(seg) + off
    nvalid = jnp.clip(pick(counts) - off, 0, MOE_BLOCK)
    used = (counts > 0).astype(I32)
    rank = jnp.cumsum(used, axis=1) - 1
    blk_ord = pick(rank)
    slots = jnp.arange(N_EXPERTS, dtype=I32)
    uexp = jnp.sum(jnp.where((rank[:, None, :] == slots[None, :, None]) & (used[:, None, :] > 0), slots[None, None, :], 0),
                   axis=-1)
    gflat = gates.reshape(TOP_K, B, S).transpose(1, 2, 0).reshape(B, n_assign)
    gflat = jnp.concatenate([gflat, jnp.zeros((B, TOP_K), F32)], axis=1)
    return blk_ord, uexp, jnp.sum(used, axis=1, keepdims=True), row0, nvalid, bend[:, -1:], skey, gflat


def _shared_kernel(h_ref, wg_ref, wu_ref, wd_ref, routed_ref, x1_ref, gt_ref, o_ref):
    hb = h_ref[0].astype(BF16)
    act = _silu(_dot(hb, wg_ref[...])) * _dot(hb, wu_ref[...])
    shared = _dot(act.astype(BF16), wd_ref[...])
    o_ref[0] = x1_ref[0] + gt_ref[0] * (_from_token_tiles(routed_ref, hb.shape[0]) + shared)


def _shared_and_residual(h2, wsg, wsu, wsd, routed, x1, gt2, tm):
    B, S, D = h2.shape
    row = lambda b, i: (b, i, 0)
    xs = pl.BlockSpec((1, tm, D), row)
    const2 = lambda a: pl.BlockSpec(a.shape, lambda b, i: (0, 0))
    return pl.pallas_call(
        _shared_kernel, grid=(B, S // tm),
        in_specs=[xs, const2(wsg), const2(wsu), const2(wsd), pl.BlockSpec((1, tm * (D // LANES), LANES), row), xs,
                  pl.BlockSpec((1, 1, D), lambda b, i: (b, 0, 0))],
        out_specs=xs,
        out_shape=jax.ShapeDtypeStruct((B, S, D), F32),
        compiler_params=_cparams(("parallel", "parallel")),
        name="shared_expert_residual",
    )(h2, wsg, wsu, wsd, routed, x1, gt2)


def _prep_in_weights(w_in, g_q_dil, g_k_dil, g_q_nsa, g_k_slc, g_k_win):
    D = w_in.shape[0]
    pad = jnp.zeros((D, IN_COLS_PADDED - w_in.shape[1]), w_in.dtype)
    w_all = jnp.concatenate([w_in, pad], axis=1).astype(BF16)
    one = jnp.ones((LANES,), F32)
    gcol = jnp.concatenate([
        jnp.tile(g_q_dil, NH_DIL), jnp.tile(g_k_dil, NH_DIL), jnp.ones((512,), F32), jnp.tile(g_q_nsa, NH_NSA),
        one, one, jnp.tile(g_k_slc, NKV_NSA), one, jnp.tile(g_k_win, NKV_NSA), one, one]).reshape(1, IN_COLS_PADDED)
    return w_all, gcol


def _rope_tables(S):
    inv_freq = ROPE_THETA ** (-jnp.arange(ROT_HALF, dtype=F32) / ROT_HALF)
    ang = jnp.arange(S).astype(F32)[:, None] * inv_freq[None, :]
    cos, sin = jnp.cos(ang), jnp.sin(ang)
    zeros = jnp.zeros((S, HD - ROT_DIM), F32)
    z8 = jnp.zeros((S, ROT_HALF), F32)
    cos_h = jnp.concatenate([cos, cos, jnp.ones((S, HD - ROT_DIM), F32)], axis=1)
    sa_h = jnp.concatenate([-sin, z8, zeros], axis=1)
    sb_h = jnp.concatenate([z8, sin, zeros], axis=1)
    two = lambda t: jnp.concatenate([t, t], axis=1)
    return two(cos_h), two(sa_h), two(sb_h)


def _prep_compress(pe, w1, w2):
    eye = jnp.eye(NKV_NSA, dtype=F32)
    w1r = w1.reshape(CMP_LEN, HD, CMP_HIDDEN)

    def half(w1h, peh):
        w = jnp.einsum("ldh,gk->lgdkh", w1h, eye).reshape(CMP_STRIDE * LANES, NKV_NSA * CMP_HIDDEN)
        p = jnp.broadcast_to(peh[:, None, :], (CMP_STRIDE, NKV_NSA, HD)).reshape(1, CMP_STRIDE * LANES)
        return p, w.astype(BF16)

    pa, wa = half(w1r[:CMP_STRIDE], pe[:CMP_STRIDE])
    pb, wb = half(w1r[CMP_STRIDE:], pe[CMP_STRIDE:])
    w2bd = jnp.einsum("hd,gk->ghkd", w2, eye).reshape(NKV_NSA * CMP_HIDDEN, LANES).astype(BF16)
    return pa, pb, wa, wb, w2bd


def _overlap_tables(S):
    ncb = S // CMP_STRIDE
    n_sel = S // SEL_BLOCK
    cs = np.arange(ncb) * CMP_STRIDE
    ss = np.arange(n_sel) * SEL_BLOCK
    ov = np.clip(np.minimum(cs[:, None] + CMP_LEN, ss[None, :] + SEL_BLOCK) - np.maximum(cs[:, None], ss[None, :]), 0, None)
    ov = ov.astype(np.float32) / CMP_STRIDE
    ovl = np.zeros((ncb, LANES), np.float32)
    ovr = np.zeros((ncb, LANES), np.float32)
    ovl[:, :n_sel] = ov
    ovr[:, HD:HD + n_sel] = ov
    return jnp.asarray(ovl, BF16), jnp.asarray(ovr, BF16)


def _block_ones():
    r = np.arange(LANES)
    return jnp.asarray((r[:, None] // HD == r[None, :] // HD).astype(np.float32), BF16)


def _layer(x, c, w_ada, b_ada, g_norm_mix, g_norm_ffn, w_in, g_q_dil, g_k_dil, g_q_nsa, g_k_cmp, g_k_slc, g_k_win,
           cmp_pe_k, cmp_w1_k, cmp_w2_k, cmp_pe_v, cmp_w1_v, cmp_w2_v, w_out, w_router, router_bias,
           w_gate, w_up, w_down, ws_gate, ws_up, ws_down):
    B, S, D = x.shape
    ones_bd = _block_ones()

    c_pad = jnp.zeros((8, D), F32).at[:B].set(c)
    mod = _ada_mod(c_pad, w_ada, b_ada.reshape(1, -1))[:B]
    sh1, sc1, gt1, sh2, sc2, gt2 = [m.reshape(B, 1, D) for m in jnp.split(mod, 6, axis=-1)]

    w_all, gcol = _prep_in_weights(w_in, g_q_dil, g_k_dil, g_q_nsa, g_k_slc, g_k_win)
    cos_t, sa_t, sb_t = _rope_tables(S)
    qa, ka, va, qn, qr, kc, vc, ksl, vsl, kw, vw, gate = _in_projection(
        x, g_norm_mix.reshape(1, D), sc1, sh1, w_all, gcol, cos_t, sa_t, sb_t, ones_bd, tm=256)

    o_a = _dilated_attention(qa, ka, va)

    kcmp, vcmp = _compress(kc, vc, _prep_compress(cmp_pe_k, cmp_w1_k, cmp_w2_k),
                           _prep_compress(cmp_pe_v, cmp_w1_v, cmp_w2_v), jnp.tile(g_k_cmp, NKV_NSA).reshape(1, LANES),
                           ones_bd)
    ovl, ovr = _overlap_tables(S)
    o_cmp, sel = _cmp_select(qn, kcmp, vcmp, ovl, ovr, gate, tq=128)
    o_cs = _nsa_branch(qr, ksl, vsl, sel, o_cmp, gate, tq=256, tk=512, mode="selected", out_dtype=F32)
    o_b = _nsa_branch(qr, kw, vw, None, o_cs, gate, tq=256, tk=256, mode="window", out_dtype=BF16)

    w_out_b = w_out.astype(BF16)
    x1, h2, h2_tiles = _out_projection(o_a, o_b, w_out_b[:NH_DIL * HD], w_out_b[NH_DIL * HD:], x, gt1,
                                       g_norm_ffn.reshape(1, D), sc2, sh2, tm=256)

    tm_r = 256
    eidx, gates, counts = _router(h2, w_router.T.astype(BF16),
                                  jnp.broadcast_to(router_bias.reshape(N_EXPERTS, 1), (N_EXPERTS, tm_r)), tm=tm_r)
    tables = _dispatch_tables(eidx, gates, counts[:, :, 0].astype(I32), B, S)
    routed = jnp.stack([_routed_experts(h2_tiles[b], *[tbl[b] for tbl in tables], w_gate, w_up, w_down)
                        for b in range(B)], axis=0)
    return _shared_and_residual(h2, ws_gate.astype(BF16), ws_up.astype(BF16), ws_down.astype(BF16), routed, x1, gt2,
                                tm=256)


def kernel(x, c, w_ada, b_ada, g_norm_mix, g_norm_ffn, w_in, g_q_dil, g_k_dil, g_q_nsa, g_k_cmp, g_k_slc, g_k_win, cmp_pe_k, cmp_w1_k, cmp_w2_k, cmp_pe_v, cmp_w1_v, cmp_w2_v, w_out, w_router, router_bias, w_gate, w_up, w_down, ws_gate, ws_up, ws_down):
    params = (w_ada, b_ada, g_norm_mix, g_norm_ffn, w_in, g_q_dil, g_k_dil, g_q_nsa, g_k_cmp, g_k_slc, g_k_win,
              cmp_pe_k, cmp_w1_k, cmp_w2_k, cmp_pe_v, cmp_w1_v, cmp_w2_v, w_out, w_router, router_bias,
              w_gate, w_up, w_down, ws_gate, ws_up, ws_down)
    for layer in range(w_ada.shape[0]):
        x = _layer(x, c, *[a[layer] for a in params])
    return x
```

```python
import functools
import math

import numpy as np
import jax
import jax.numpy as jnp
from jax import lax
from jax.experimental import pallas as pl
from jax.experimental.pallas import tpu as pltpu

F32 = jnp.float32
BF16 = jnp.bfloat16
I32 = jnp.int32

HD = 64
LANES = 128
NH_DIL = 8
NH_NSA = 8
NKV_NSA = 2
NSA_REP = NH_NSA // NKV_NSA
DIL_PAIRS = ((128, 1), (512, 4), (2048, 16))
DIL_TILE = 128
DIL_GROUP = 8
ROPE_THETA = 500000.0
ROT_DIM = HD // 4
ROT_HALF = ROT_DIM // 2
CMP_LEN = 32
CMP_STRIDE = 16
CMP_HIDDEN = 128
SEL_BLOCK = 64
SEL_SHIFT = 6
SEL_TOP = 16
NSA_WINDOW = 512
N_EXPERTS = 256
TOP_K = 8
TOP_K_SHIFT = 3
N_GROUPS = 8
GROUP_SIZE = N_EXPERTS // N_GROUPS
TOPK_GROUPS = 4
EXPERT_FF = 256
SHARED_FF = 256
ROUTED_SCALE = 2.5
EPS = 1e-6
QK_SCALE = 1.0 / math.sqrt(HD)

NEG = -1e30
BIG = 3e38

C_QA, C_KA, C_VA, C_QB = 0, 512, 1024, 1536
C_KC, C_VC, C_KSL, C_VSL, C_KW, C_VW, C_GB = 2048, 2176, 2304, 2432, 2560, 2688, 2816
IN_COLS_PADDED = 2944
TRANSPOSED_OUTS = (8, 10)

MOE_BLOCK = 128
NULL_ROWS = 8
WEIGHT_RING = 5
SCATTER_GROUP = 16
VMEM_LIMIT = 48 * 1024 * 1024


def _cparams(sem, vmem=VMEM_LIMIT):
    return pltpu.CompilerParams(dimension_semantics=sem, vmem_limit_bytes=vmem)


def _sigmoid(v):
    return 1.0 / (1.0 + jnp.exp(-v))


def _silu(v):
    return v * _sigmoid(v)


def _dot(a, b):
    return jnp.dot(a, b, preferred_element_type=F32)


def _dot_nt(a, b):
    return lax.dot_general(a, b, (((1,), (1,)), ((), ())), preferred_element_type=F32)


def _lane_lo(shape):
    return lax.broadcasted_iota(I32, shape, len(shape) - 1) < HD


def _head_sums(v, ones_bd):
    hi = v.astype(BF16)
    lo = (v - hi.astype(F32)).astype(BF16)
    return _dot(hi, ones_bd) + _dot(lo, ones_bd)


def _head_rmsnorm(y, ones_bd, gain):
    ms = _head_sums(y * y, ones_bd) * (1.0 / HD)
    return y * lax.rsqrt(ms + EPS) * gain


def _rope(y, cos, sin_a, sin_b):
    return y * cos + pltpu.roll(y, LANES - ROT_HALF, 1) * sin_a + pltpu.roll(y, ROT_HALF, 1) * sin_b


def _ada_kernel(c_ref, w_ref, b_ref, o_ref):
    a = _silu(c_ref[...]).astype(BF16)
    o_ref[...] = _dot(a, w_ref[...].astype(BF16)) + b_ref[...]


def _ada_mod(c_pad, w_ada, b_ada):
    rows, d = c_pad.shape
    n = w_ada.shape[1]
    tn = 1536 if n % 1536 == 0 else n
    return pl.pallas_call(
        _ada_kernel,
        grid=(n // tn,),
        in_specs=[pl.BlockSpec((rows, d), lambda j: (0, 0)),
                  pl.BlockSpec((d, tn), lambda j: (0, j)),
                  pl.BlockSpec((1, tn), lambda j: (0, j))],
        out_specs=pl.BlockSpec((rows, tn), lambda j: (0, j)),
        out_shape=jax.ShapeDtypeStruct((rows, n), F32),
        compiler_params=_cparams(("arbitrary",)),
        name="ada_mod",
    )(c_pad, w_ada, b_ada)


def _inproj_kernel(x_ref, g_ref, sc_ref, sh_ref, w_ref, gcol_ref, cos_ref, sa_ref, sb_ref, ones_ref,
                   qa_ref, ka_ref, va_ref, qn_ref, qr_ref, kc_ref, vc_ref, ksl_ref, vsl_ref, kw_ref, vw_ref,
                   gate_ref):
    x = x_ref[0]
    ms = jnp.mean(x * x, axis=-1, keepdims=True)
    h = (x * lax.rsqrt(ms + EPS)) * g_ref[...]
    h = h * (1.0 + sc_ref[0]) + sh_ref[0]
    hb = h.astype(BF16)
    ones_bd = ones_ref[...]
    cos, sa, sb = cos_ref[...], sa_ref[...], sb_ref[...]
    lo = _lane_lo(cos.shape)

    def proj(c0, width):
        return _dot(hb, w_ref[:, c0:c0 + width])

    def normed(tile, c0):
        return _head_rmsnorm(tile, ones_bd, gcol_ref[:, c0:c0 + LANES])

    acc = proj(C_QA, 512)
    for p in range(4):
        y = _rope(normed(acc[:, p * LANES:(p + 1) * LANES], C_QA + p * LANES), cos, sa, sb) * QK_SCALE
        qa_ref[0, :, p * LANES:(p + 1) * LANES] = y
    acc = proj(C_KA, 512)
    for p in range(4):
        y = _rope(normed(acc[:, p * LANES:(p + 1) * LANES], C_KA + p * LANES), cos, sa, sb)
        ka_ref[0, :, p * LANES:(p + 1) * LANES] = y
    va_ref[0] = proj(C_VA, 512)

    acc = proj(C_QB, 512)
    for p in range(4):
        yn = normed(acc[:, p * LANES:(p + 1) * LANES], C_QB + p * LANES)
        yr = _rope(yn, cos, sa, sb)
        for y, ref in ((yn * QK_SCALE, qn_ref), (yr * QK_SCALE, qr_ref)):
            ysw = pltpu.roll(y, HD, 1)
            for half in range(2):
                head = 2 * p + half
                grp = head // NSA_REP
                src = y if grp == half else ysw
                keep = lo if grp == 0 else jnp.logical_not(lo)
                ref[0, :, head * LANES:(head + 1) * LANES] = jnp.where(keep, src, 0.0).astype(BF16)

    acc = proj(C_KC, 512)
    kc_ref[0] = acc[:, 0:LANES]
    vc_ref[0] = acc[:, LANES:2 * LANES]
    ksl_ref[0] = _rope(normed(acc[:, 2 * LANES:3 * LANES], C_KSL), cos, sa, sb).astype(BF16)
    vsl_ref[0] = _transposed(acc[:, 3 * LANES:4 * LANES]).astype(BF16)
    acc = proj(C_KW, 384)
    kw_ref[0] = _rope(normed(acc[:, 0:LANES], C_KW), cos, sa, sb).astype(BF16)
    vw_ref[0] = _transposed(acc[:, LANES:2 * LANES]).astype(BF16)
    gate_ref[0] = _sigmoid(acc[:, 2 * LANES:3 * LANES])


def _transposed(tile):
    return jnp.concatenate([tile[r0:r0 + LANES].T for r0 in range(0, tile.shape[0], LANES)], axis=1)


def _in_projection(x, g_mix, sc1, sh1, w_all, gcol, cos_t, sa_t, sb_t, ones_bd, tm):
    B, S, D = x.shape
    nc = w_all.shape[1]
    row = lambda b, i: (b, i, 0)
    bvec = pl.BlockSpec((1, 1, D), lambda b, i: (b, 0, 0))
    tab = pl.BlockSpec((tm, LANES), lambda b, i: (i, 0))
    const2 = lambda shape: pl.BlockSpec(shape, lambda b, i: (0, 0))
    widths_dt = [(512, F32), (512, F32), (512, F32), (1024, BF16), (1024, BF16), (LANES, F32), (LANES, F32),
                 (LANES, BF16), (LANES, BF16), (LANES, BF16), (LANES, BF16), (LANES, F32)]
    return pl.pallas_call(
        _inproj_kernel,
        grid=(B, S // tm),
        in_specs=[pl.BlockSpec((1, tm, D), row), const2((1, D)), bvec, bvec, const2((D, nc)), const2((1, nc)),
                  tab, tab, tab, const2((LANES, LANES))],
        out_specs=[pl.BlockSpec((1, tm, w), row) if n not in TRANSPOSED_OUTS else pl.BlockSpec((1, w, tm), lambda b, i: (b, 0, i))
                   for n, (w, _) in enumerate(widths_dt)],
        out_shape=[jax.ShapeDtypeStruct((B, S, w) if n not in TRANSPOSED_OUTS else (B, w, S), dt)
                   for n, (w, dt) in enumerate(widths_dt)],
        compiler_params=_cparams(("parallel", "parallel")),
        name="in_projection",
    )(x, g_mix, sc1, sh1, w_all, gcol, cos_t, sa_t, sb_t, ones_bd)


def _dot_tn(a, b):
    return lax.dot_general(a, b, (((0,), (0,)), ((), ())), preferred_element_type=F32)


def _dilated_kernel(q_ref, k_ref, v_ref, o_ref, oacc, lacc, *, sb_rows, pairs):
    sb = pl.program_id(2)
    base = sb * sb_rows
    t = DIL_TILE
    kk = lax.broadcasted_iota(I32, (t, t), 0)
    qq = lax.broadcasted_iota(I32, (t, t), 1)
    bias_cur = jnp.where(qq >= kk, 0.0, NEG)
    bias_prev = jnp.where(qq <= kk, 0.0, NEG)
    lane_lo = _lane_lo((t, LANES))
    row_lo = lax.broadcasted_iota(I32, (LANES, t), 0) < HD

    def blocks(d, starts, mode):
        def strided(ref, start):
            return ref[0, pl.ds(start, t, stride=d), :] if d > 1 else ref[0, pl.ds(start, t), :]

        work = []
        for qs, first in starts:
            qg = base + qs
            rows_q = pl.ds(qs, t, stride=d) if d > 1 else pl.ds(qs, t)
            prev_start = jnp.where(first, qg, qg - t * d)
            qt = q_ref[0, rows_q, :].astype(BF16)
            zero = jnp.zeros_like(qt)
            q2 = jnp.concatenate([jnp.where(lane_lo, qt, zero), jnp.where(lane_lo, zero, qt)], axis=0)
            k2 = jnp.concatenate([strided(k_ref, prev_start), strided(k_ref, qg)], axis=0).astype(BF16)
            v2 = jnp.concatenate([strided(v_ref, prev_start), strided(v_ref, qg)], axis=0).astype(BF16)
            bias = jnp.concatenate([jnp.where(first, NEG, bias_prev), bias_cur], axis=0)
            s = _dot_nt(k2, q2) + jnp.concatenate([bias, bias], axis=1)
            work.append([rows_q, v2, s])
        for w in work:
            s = w[2]
            m = s.max(axis=0, keepdims=True)
            p = jnp.exp(s - m)
            l = p.sum(axis=0, keepdims=True)
            w[2:] = [p.astype(BF16), l, m + jnp.log(l)]
        for w in work:
            w[1] = _dot_tn(w[1], w[2]) / w[3]
        for w in work:
            ov, lse2 = w[1], w[4]
            w[1:] = [jnp.where(row_lo, ov[:, :t], ov[:, t:]).T, jnp.where(row_lo, lse2[:, :t], lse2[:, t:]).T]
        for rows_q, o, lse in work:
            if mode == "init":
                oacc[rows_q, :] = o
                lacc[rows_q, :] = lse
                continue
            lp = lacc[rows_q, :]
            mx = jnp.maximum(lp, lse)
            wp, wn = jnp.exp(lp - mx), jnp.exp(lse - mx)
            den = wp + wn
            oacc[rows_q, :] = (oacc[rows_q, :] * wp + o * wn) / den
            if mode == "mid":
                lacc[rows_q, :] = mx + jnp.log(den)

    modes = ("init",) + ("mid",) * (len(pairs) - 2) + ("last",)
    for (window, d), mode in zip(pairs, modes):
        assert window // d == t and d & (d - 1) == 0
        n_blocks = sb_rows // t

        def body(i, c, d=d, mode=mode):
            starts = []
            for u in range(DIL_GROUP):
                n = i * DIL_GROUP + u
                r, j = n & (d - 1), n >> (d.bit_length() - 1)
                starts.append((j * (t * d) + r, jnp.logical_and(sb == 0, j == 0)))
            blocks(d, starts, mode)
            return c

        lax.fori_loop(0, n_blocks // DIL_GROUP, body, 0)
    o_ref[0] = oacc[...].astype(o_ref.dtype)


def _dilated_attention(qa, ka, va):
    B, S, w = qa.shape
    sb_rows = DIL_TILE * max(d for _, d in DIL_PAIRS)
    assert S % sb_rows == 0
    kern = functools.partial(_dilated_kernel, sb_rows=sb_rows, pairs=DIL_PAIRS)
    whole = pl.BlockSpec((1, S, LANES), lambda b, p, s: (b, 0, p))
    blk = pl.BlockSpec((1, sb_rows, LANES), lambda b, p, s: (b, s, p))
    acc = pltpu.VMEM((sb_rows, LANES), F32)
    return pl.pallas_call(
        kern, grid=(B, w // LANES, S // sb_rows),
        in_specs=[blk, whole, whole], out_specs=blk,
        out_shape=jax.ShapeDtypeStruct((B, S, w), BF16),
        scratch_shapes=[acc, acc],
        compiler_params=_cparams(("parallel", "parallel", "arbitrary")),
        name="dilated_attention",
    )(qa, ka, va)


def _compress_kernel(kc_ref, vc_ref, pak_ref, pbk_ref, wak_ref, wbk_ref, w2k_ref, gk_ref, ones_ref,
                     pav_ref, pbv_ref, wav_ref, wbv_ref, w2v_ref, ko_ref, vo_ref):
    ncb = kc_ref.shape[1]

    def branch(t_ref, pa, pb, wa, wb, w2):
        t = t_ref[0]
        ua = _dot((t + pa[...]).astype(BF16), wa[...])
        ub = _dot((t + pb[...]).astype(BF16), wb[...])
        pre = ua + pltpu.roll(ub, ncb - 1, 0)
        return _dot(jax.nn.gelu(pre).astype(BF16), w2[...])

    live = lax.broadcasted_iota(I32, (ncb, LANES), 0) < ncb - 1
    kcmp = _head_rmsnorm(branch(kc_ref, pak_ref, pbk_ref, wak_ref, wbk_ref, w2k_ref), ones_ref[...], gk_ref[...])
    ko_ref[0] = jnp.where(live, kcmp, 0.0).astype(BF16)
    vo_ref[0] = jnp.where(live, branch(vc_ref, pav_ref, pbv_ref, wav_ref, wbv_ref, w2v_ref), 0.0).astype(BF16)


def _compress(kc, vc, wk, wv, gk2, ones_bd):
    B, S, _ = kc.shape
    ncb = S // CMP_STRIDE
    cw = CMP_STRIDE * LANES
    view = lambda a: a.reshape(B, ncb, cw)
    blk = pl.BlockSpec((1, ncb, cw), lambda b: (b, 0, 0))
    full = lambda a: pl.BlockSpec(a.shape, lambda b: (0,) * a.ndim)
    consts = [*wk, gk2, ones_bd, *wv]
    o_spec = pl.BlockSpec((1, ncb, LANES), lambda b: (b, 0, 0))
    return pl.pallas_call(
        _compress_kernel, grid=(B,),
        in_specs=[blk, blk] + [full(a) for a in consts],
        out_specs=[o_spec, o_spec],
        out_shape=[jax.ShapeDtypeStruct((B, ncb, LANES), BF16)] * 2,
        compiler_params=_cparams(("parallel",)),
        name="nsa_compress",
    )(view(kc), view(vc), *consts)


def _pair_tile(lo, t0, half0, t1, half1):
    a = t0 if half0 == 0 else pltpu.roll(t0, HD, 1)
    b = t1 if half1 == 1 else pltpu.roll(t1, HD, 1)
    return jnp.where(lo, a, b)


def _cmpsel_kernel(qn_ref, kc_ref, vc_ref, ovl_ref, ovr_ref, gate_ref, o_ref, sel_ref, *, tq, n_sel, top):
    ncb = kc_ref.shape[1]
    i = pl.program_id(1)
    pos0 = i * tq
    qpos = pos0 + lax.broadcasted_iota(I32, (tq, ncb), 0)
    cend = lax.broadcasted_iota(I32, (tq, ncb), 1) * CMP_STRIDE + (CMP_LEN - 1)
    valid = cend <= qpos
    lo = _lane_lo((tq, LANES))
    kc = kc_ref[0]
    vc = vc_ref[0]

    psum = [None] * NKV_NSA
    scores = [_dot_nt(qn_ref[0, :, h * LANES:(h + 1) * LANES], kc) for h in range(NH_NSA)]
    probs = []
    for h, s in enumerate(scores):
        g = h // NSA_REP
        m = jnp.where(valid, s, NEG).max(axis=1, keepdims=True)
        m = jnp.where(m > 0.5 * NEG, m, 0.0)
        p = jnp.where(valid, jnp.exp(s - m), 0.0)
        l = p.sum(axis=1, keepdims=True)
        p = p / jnp.where(l > 0.0, l, 1.0)
        probs.append(p.astype(BF16))
        psum[g] = p if psum[g] is None else psum[g] + p
    outs = [_dot(p, vc) for p in probs]

    for p_ in range(NH_NSA // 2):
        h0, h1 = 2 * p_, 2 * p_ + 1
        o_pair = _pair_tile(lo, outs[h0], h0 // NSA_REP, outs[h1], h1 // NSA_REP)
        g = jnp.where(lo, gate_ref[0, :, 3 * h0:3 * h0 + 1], gate_ref[0, :, 3 * h1:3 * h1 + 1])
        o_ref[0, :, p_ * LANES:(p_ + 1) * LANES] = g * o_pair

    def hilo_dot(a, w):
        hi = a.astype(BF16)
        return _dot(hi, w) + _dot((a - hi.astype(F32)).astype(BF16), w)

    pslc = hilo_dot(psum[0], ovl_ref[...]) + hilo_dot(psum[1], ovr_ref[...])
    pslc_t = pslc.T
    jblk = lax.broadcasted_iota(I32, (HD, tq), 0)
    cur = (pos0 + lax.broadcasted_iota(I32, (HD, tq), 1)) >> SEL_SHIFT
    forced = (jblk == 0) | (jblk == cur) | (jblk == cur - 1)
    sel_t = []
    for g in range(NKV_NSA):
        sc = jnp.where(forced, BIG, jnp.where(jblk <= cur, pslc_t[g * HD:(g + 1) * HD, :], -BIG))
        rank = jnp.zeros((HD, tq), F32)
        for k in range(n_sel):
            rk = sc[k:k + 1, :]
            rank = rank + jnp.where(jblk > k, jnp.where(rk >= sc, 1.0, 0.0), jnp.where(rk > sc, 1.0, 0.0))
        sel_t.append(jnp.where(rank < top, 1.0, 0.0))
    sel_ref[0] = jnp.concatenate(sel_t, axis=0)


def _cmp_select(qn, kcmp, vcmp, ovl, ovr, gate, tq):
    B, S, qw = qn.shape
    ncb = kcmp.shape[1]
    n_sel = S // SEL_BLOCK
    assert n_sel <= HD and tq == LANES
    row = lambda b, i: (b, i, 0)
    cb = pl.BlockSpec((1, ncb, LANES), lambda b, i: (b, 0, 0))
    ovs = pl.BlockSpec((ncb, LANES), lambda b, i: (0, 0))
    kern = functools.partial(_cmpsel_kernel, tq=tq, n_sel=n_sel, top=min(SEL_TOP, n_sel))
    return pl.pallas_call(
        kern, grid=(B, S // tq),
        in_specs=[pl.BlockSpec((1, tq, qw), row), cb, cb, ovs, ovs, pl.BlockSpec((1, tq, LANES), row)],
        out_specs=[pl.BlockSpec((1, tq, NH_NSA * HD), row), pl.BlockSpec((1, LANES, tq), lambda b, i: (b, 0, i))],
        out_shape=[jax.ShapeDtypeStruct((B, S, NH_NSA * HD), F32), jax.ShapeDtypeStruct((B, LANES, S), F32)],
        compiler_params=_cparams(("parallel", "parallel")),
        name="nsa_cmp_select",
    )(qn, kcmp, vcmp, ovl, ovr, gate)


def _nsa_kernel(*refs, tq, tk, mode):
    if mode == "selected":
        q_ref, k_ref, vt_ref, selt_ref, prev_ref, gate_ref, o_ref, m_ref, l_ref, acc_ref = refs
    else:
        q_ref, k_ref, vt_ref, prev_ref, gate_ref, o_ref, m_ref, l_ref, acc_ref = refs
    i = pl.program_id(1)
    kk = pl.program_id(2)
    nkv = pl.num_programs(2)
    if mode == "selected":
        tile = kk
        active = kk <= (i * tq + tq - 1) // tk
    else:
        tile = i * (tq // tk) + kk - (nkv - tq // tk)
        active = tile >= 0

    @pl.when(kk == 0)
    def _():
        m_ref[...] = jnp.full(m_ref.shape, NEG, F32)
        l_ref[...] = jnp.zeros(l_ref.shape, F32)
        acc_ref[...] = jnp.zeros(acc_ref.shape, F32)

    @pl.when(active)
    def _():
        kpos = tile * tk + lax.broadcasted_iota(I32, (tk, tq), 0)
        qpos = i * tq + lax.broadcasted_iota(I32, (tk, tq), 1)
        k = k_ref[0]
        vt = vt_ref[0]
        if mode == "window":
            dist = qpos - kpos
            bias_w = jnp.where(dist >= 0, jnp.where(dist < NSA_WINDOW, 0.0, NEG), NEG)
        groups = range(NKV_NSA)
        scores = []
        for g in groups:
            if mode == "selected":
                nblk = tk // SEL_BLOCK
                rows = selt_ref[0, pl.ds(pl.multiple_of(g * HD + kk * nblk, nblk), nblk), :]
                chosen = jnp.broadcast_to(rows[:, None, :], (nblk, SEL_BLOCK, tq)).reshape(tk, tq) > 0.5
                bias = jnp.where(kpos <= qpos, jnp.where(chosen, 0.0, NEG), NEG)
            else:
                bias = bias_w
            heads = range(g * NSA_REP, (g + 1) * NSA_REP)
            q4 = jnp.concatenate([q_ref[0, :, h * LANES:(h + 1) * LANES] for h in heads], axis=0)
            scores.append(_dot_nt(k, q4) + jnp.concatenate([bias] * NSA_REP, axis=1))
        probs = []
        for g in groups:
            m_prev = m_ref[g]
            m_new = jnp.maximum(m_prev, scores[g].max(axis=0, keepdims=True))
            alpha = jnp.exp(m_prev - m_new)
            p = jnp.exp(scores[g] - m_new[0:1])
            l_ref[g] = alpha * l_ref[g] + p.sum(axis=0, keepdims=True)
            m_ref[g] = m_new
            probs.append((alpha[0:1], p.astype(BF16)))
        for g in groups:
            alpha, p = probs[g]
            acc_ref[g] = alpha * acc_ref[g] + _dot(vt, p)

    @pl.when(kk == nkv - 1)
    def _():
        lo = _lane_lo((tq, LANES))
        gate_col = 1 if mode == "selected" else 2

        def head_t(h):
            g, r = divmod(h, NSA_REP)
            cols = slice(r * tq, (r + 1) * tq)
            return acc_ref[g, g * HD:(g + 1) * HD, cols] / l_ref[g, 0:1, cols]

        for p_ in range(NH_NSA // 2):
            h0, h1 = 2 * p_, 2 * p_ + 1
            o_pair = jnp.concatenate([head_t(h0), head_t(h1)], axis=0).T
            c0, c1 = 3 * h0 + gate_col, 3 * h1 + gate_col
            gt = jnp.where(lo, gate_ref[0, :, c0:c0 + 1], gate_ref[0, :, c1:c1 + 1])
            sl = slice(p_ * LANES, (p_ + 1) * LANES)
            o_ref[0, :, sl] = (prev_ref[0, :, sl] + gt * o_pair).astype(o_ref.dtype)


def _nsa_branch(qr, k, v_t, sel_t, prev, gate, *, tq, tk, mode, out_dtype):
    B, S, qw = qr.shape
    row = lambda b, i, kk: (b, i, 0)
    ow = NH_NSA * HD
    if mode == "selected":
        n_steps = S // tk
        tile = lambda i, kk: jnp.minimum(kk, (i * tq + tq - 1) // tk)
    else:
        assert tq % tk == 0
        n_steps = pl.cdiv(NSA_WINDOW - 1, tk) + tq // tk
        tile = lambda i, kk: jnp.maximum(i * (tq // tk) + kk - (n_steps - tq // tk), 0)
    in_specs = [pl.BlockSpec((1, tq, qw), row),
                pl.BlockSpec((1, tk, LANES), lambda b, i, kk: (b, tile(i, kk), 0)),
                pl.BlockSpec((1, LANES, tk), lambda b, i, kk: (b, 0, tile(i, kk)))]
    args = [qr, k, v_t]
    if mode == "selected":
        in_specs.append(pl.BlockSpec((1, LANES, tq), lambda b, i, kk: (b, 0, i)))
        args.append(sel_t)
    in_specs += [pl.BlockSpec((1, tq, ow), row), pl.BlockSpec((1, tq, LANES), row)]
    args += [prev, gate]
    stat = pltpu.VMEM((NKV_NSA, 8, NSA_REP * tq), F32)
    return pl.pallas_call(
        functools.partial(_nsa_kernel, tq=tq, tk=tk, mode=mode), grid=(B, S // tq, n_steps),
        in_specs=in_specs,
        out_specs=pl.BlockSpec((1, tq, ow), row),
        out_shape=jax.ShapeDtypeStruct((B, S, ow), out_dtype),
        scratch_shapes=[stat, stat, pltpu.VMEM((NKV_NSA, LANES, NSA_REP * tq), F32)],
        compiler_params=_cparams(("parallel", "parallel", "arbitrary")),
        name=f"nsa_{mode}_attention",
    )(*args)


def _to_token_tiles(ref, value):
    m, d = value.shape
    rpt = d // LANES
    for j in range(rpt):
        ref[0, pl.ds(j, m, stride=rpt), :] = value[:, j * LANES:(j + 1) * LANES]


def _from_token_tiles(ref, m):
    rpt = ref.shape[1] // m
    return jnp.concatenate([ref[0, pl.ds(j, m, stride=rpt), :] for j in range(rpt)], axis=1)


def _outproj_kernel(oa_ref, ob_ref, wt_ref, wb_ref, x_ref, gt_ref, g_ref, sc_ref, sh_ref, x1_ref, h2_ref, h2t_ref):
    mix = _dot(oa_ref[0], wt_ref[...]) + _dot(ob_ref[0], wb_ref[...])
    x1 = x_ref[0] + gt_ref[0] * mix
    x1_ref[0] = x1
    ms = jnp.mean(x1 * x1, axis=-1, keepdims=True)
    h = (x1 * lax.rsqrt(ms + EPS)) * g_ref[...]
    h2 = h * (1.0 + sc_ref[0]) + sh_ref[0]
    h2_ref[0] = h2
    _to_token_tiles(h2t_ref, h2)


def _out_projection(oa, ob, w_top, w_bot, x, gt1, g_ffn, sc2, sh2, tm):
    B, S, D = x.shape
    row = lambda b, i: (b, i, 0)
    bvec = pl.BlockSpec((1, 1, D), lambda b, i: (b, 0, 0))
    const2 = lambda a: pl.BlockSpec(a.shape, lambda b, i: (0, 0))
    xs = pl.BlockSpec((1, tm, D), row)
    return pl.pallas_call(
        _outproj_kernel, grid=(B, S // tm),
        in_specs=[pl.BlockSpec((1, tm, oa.shape[2]), row), pl.BlockSpec((1, tm, ob.shape[2]), row),
                  const2(w_top), const2(w_bot), xs, bvec, const2(g_ffn), bvec, bvec],
        out_specs=[xs, xs, pl.BlockSpec((1, tm * (D // LANES), LANES), row)],
        out_shape=[jax.ShapeDtypeStruct((B, S, D), F32)] * 2 + [jax.ShapeDtypeStruct((B, S * (D // LANES), LANES), F32)],
        compiler_params=_cparams(("parallel", "parallel")),
        name="out_projection",
    )(oa, ob, w_top, w_bot, x, gt1, g_ffn, sc2, sh2)


def _router_kernel(h_ref, wr_ref, bias_ref, eidx_ref, gate_ref, cnt_ref):
    tm = h_ref.shape[1]
    logits = _dot_nt(wr_ref[...], h_ref[0].astype(BF16))
    aff = _sigmoid(logits)
    biased = aff + bias_ref[...]
    b3 = biased.reshape(N_GROUPS, GROUP_SIZE, tm)
    jj = lax.broadcasted_iota(I32, (N_GROUPS, GROUP_SIZE, tm), 1).astype(F32)
    m1 = b3.max(axis=1, keepdims=True)
    i1 = jnp.where(b3 == m1, jj, 1e9).min(axis=1, keepdims=True)
    m2 = jnp.where(jj == i1, -jnp.inf, b3).max(axis=1, keepdims=True)
    gs = (m1 + m2).reshape(N_GROUPS, tm)
    gi = lax.broadcasted_iota(I32, (N_GROUPS, tm), 0)
    rank = jnp.zeros((N_GROUPS, tm), F32)
    for k in range(N_GROUPS):
        rk = gs[k:k + 1, :]
        rank = rank + jnp.where(gi > k, jnp.where(rk >= gs, 1.0, 0.0), jnp.where(rk > gs, 1.0, 0.0))
    gsel = (rank < TOPK_GROUPS).reshape(N_GROUPS, 1, tm)
    masked = jnp.where(gsel, b3, -jnp.inf).reshape(N_EXPERTS, tm)
    ee = lax.broadcasted_iota(I32, (N_EXPERTS, tm), 0).astype(F32)
    idx_rows, sel_rows = [], []
    chosen = jnp.zeros((N_EXPERTS, tm), F32)
    for _ in range(TOP_K):
        m = masked.max(axis=0, keepdims=True)
        idx = jnp.where(masked == m, ee, 1e9).min(axis=0, keepdims=True)
        hit = ee == idx
        idx_rows.append(idx)
        sel_rows.append(jnp.where(hit, aff, 0.0).sum(axis=0, keepdims=True))
        masked = jnp.where(hit, -jnp.inf, masked)
        chosen = chosen + jnp.where(hit, 1.0, 0.0)
    sel = jnp.concatenate(sel_rows, axis=0)
    eidx_ref[...] = jnp.concatenate(idx_rows, axis=0).astype(I32)
    gate_ref[...] = sel / sel.sum(axis=0, keepdims=True) * ROUTED_SCALE

    @pl.when(pl.program_id(1) == 0)
    def _():
        cnt_ref[...] = jnp.zeros(cnt_ref.shape, F32)

    cnt_ref[0] = cnt_ref[0] + chosen.sum(axis=1, keepdims=True)


def _router(h2, wr_t, bias_b, tm):
    B, S, D = h2.shape
    nt = S // tm
    o_spec = pl.BlockSpec((TOP_K, tm), lambda b, i: (0, b * nt + i))
    return pl.pallas_call(
        _router_kernel, grid=(B, nt),
        in_specs=[pl.BlockSpec((1, tm, D), lambda b, i: (b, i, 0)), pl.BlockSpec(wr_t.shape, lambda b, i: (0, 0)),
                  pl.BlockSpec(bias_b.shape, lambda b, i: (0, 0))],
        out_specs=[o_spec, o_spec, pl.BlockSpec((1, N_EXPERTS, LANES), lambda b, i: (b, 0, 0))],
        out_shape=[jax.ShapeDtypeStruct((TOP_K, B * S), I32), jax.ShapeDtypeStruct((TOP_K, B * S), F32),
                   jax.ShapeDtypeStruct((B, N_EXPERTS, LANES), F32)],
        compiler_params=_cparams(("parallel", "arbitrary")),
        name="moe_router",
    )(h2, wr_t, bias_b)


def _experts_kernel(ord_ref, uexp_ref, nused_ref, row0_ref, nvalid_ref, nblk_ref, skey_ref, gflat_ref,
                    h_hbm, wg_hbm, wu_hbm, wd_hbm, o_hbm, h_vmem, acc_ref, xg0, xg1, y0, y1, wg_ring, wu_ring, wd_ring,
                    sem, wsem):
    b = pl.program_id(0)
    last_step = pl.num_programs(0) - 1
    nblk = nblk_ref[0]
    rpt = wg_hbm.shape[1] // LANES
    tc = h_vmem.shape[0] // rpt - 1
    n_assign = tc * TOP_K
    depth = wg_ring.shape[0]

    def weight_copies(j):
        e, slot = uexp_ref[j], j % depth
        return [pltpu.make_async_copy(hbm.at[e], ring.at[slot], wsem.at[n, slot])
                for n, (hbm, ring) in enumerate(((wg_hbm, wg_ring), (wu_hbm, wu_ring), (wd_hbm, wd_ring)))]

    def start_weights(j):
        @pl.when(j < nused_ref[0])
        def _():
            for cp in weight_copies(j):
                cp.start()

    def block_rows(blk):
        bc = jnp.clip(blk, 0, nblk - 1)
        return row0_ref[bc], nvalid_ref[bc]

    def assignment(rows, r, checked=True):
        row0, nv = rows
        a = skey_ref[row0 + r] & (n_assign - 1)
        return jnp.where(r < nv, a, n_assign) if checked else a

    def tile_of(a):
        off = (a & ~(TOP_K - 1)) if rpt == TOP_K else (a >> TOP_K_SHIFT) * rpt
        return pl.ds(pl.multiple_of(off, rpt), rpt)

    def gather(blk, xg):
        rows = block_rows(blk)
        for r in range(MOE_BLOCK):
            xg[r * rpt:(r + 1) * rpt, :] = h_vmem[tile_of(assignment(rows, r, checked=False)), :]

    def scatter(blk, y):
        blk_rows = block_rows(blk)
        for g0 in range(0, MOE_BLOCK, SCATTER_GROUP):
            updated = []
            for r in range(g0, g0 + SCATTER_GROUP):
                a = assignment(blk_rows, r)
                rows = tile_of(a)
                updated.append((rows, acc_ref[rows, :] + gflat_ref[a] * y[r * rpt:(r + 1) * rpt, :]))
            for rows, v in updated:
                acc_ref[rows, :] = v

    @pl.when(b == 0)
    def _():
        cp = pltpu.make_async_copy(h_hbm, h_vmem.at[pl.ds(0, tc * rpt)], sem)
        cp.start()
        for j in range(depth - 1):
            start_weights(j)
        h_vmem[pl.ds(tc * rpt, rpt), :] = jnp.zeros((rpt, LANES), F32)
        acc_ref[...] = jnp.zeros(acc_ref.shape, F32)
        y0[...] = jnp.zeros(y0.shape, F32)
        y1[...] = jnp.zeros(y1.shape, F32)
        cp.wait()
        gather(0, xg0)

    cur = ord_ref[jnp.minimum(b, nblk - 1)]

    @pl.when(jnp.logical_and(b < nblk, jnp.logical_or(b == 0, ord_ref[jnp.maximum(b, 1) - 1] != cur)))
    def _():
        for cp in weight_copies(cur):
            cp.wait()
        start_weights(cur + depth - 1)

    def step(xg_cur, xg_nxt, y_cur, y_prv):
        slot = cur % depth
        gather(b + 1, xg_nxt)
        xb = jnp.concatenate([xg_cur[pl.ds(j, MOE_BLOCK, stride=rpt), :] for j in range(rpt)], axis=1).astype(BF16)
        gt = _dot(xb, wg_ring[slot].astype(BF16))
        up = _dot(xb, wu_ring[slot].astype(BF16))
        y = _dot((_silu(gt) * up).astype(BF16), wd_ring[slot].astype(BF16))
        for j in range(rpt):
            y_cur[pl.ds(j, MOE_BLOCK, stride=rpt), :] = y[:, j * LANES:(j + 1) * LANES]
        scatter(b - 1, y_prv)

    @pl.when(jnp.logical_and(b <= nblk, (b & 1) == 0))
    def _():
        step(xg0, xg1, y0, y1)

    @pl.when(jnp.logical_and(b <= nblk, (b & 1) == 1))
    def _():
        step(xg1, xg0, y1, y0)

    @pl.when(b == last_step)
    def _():
        cp = pltpu.make_async_copy(acc_ref.at[pl.ds(0, tc * rpt)], o_hbm, sem)
        cp.start()
        cp.wait()


def _routed_experts(h_tiles, blk_ord, uexp, nused, row0, nvalid, nblk, skey, gflat, w_gate, w_up, w_down):
    D, ff = w_gate.shape[1], w_gate.shape[2]
    rpt = D // LANES
    tc = h_tiles.shape[0] // rpt
    nb = blk_ord.shape[0]
    hbm = pl.BlockSpec(memory_space=pl.ANY)
    rows = pltpu.VMEM((MOE_BLOCK * rpt, LANES), F32)
    resident = pltpu.VMEM(((tc + 1) * rpt, LANES), F32)
    grid_spec = pltpu.PrefetchScalarGridSpec(
        num_scalar_prefetch=8, grid=(nb + 1,),
        in_specs=[hbm, hbm, hbm, hbm],
        out_specs=hbm,
        scratch_shapes=[resident, resident, rows, rows, rows, rows,
                        pltpu.VMEM((WEIGHT_RING, D, ff), F32), pltpu.VMEM((WEIGHT_RING, D, ff), F32),
                        pltpu.VMEM((WEIGHT_RING, ff, D), F32),
                        pltpu.SemaphoreType.DMA(()), pltpu.SemaphoreType.DMA((3, WEIGHT_RING))],
    )
    return pl.pallas_call(
        _experts_kernel, grid_spec=grid_spec,
        out_shape=jax.ShapeDtypeStruct((tc * rpt, LANES), F32),
        compiler_params=_cparams(("arbitrary",), vmem=56 * 1024 * 1024),
        name="moe_routed_experts",
    )(blk_ord, uexp, nused, row0, nvalid, nblk, skey, gflat, h_tiles, w_gate, w_up, w_down)


def _dispatch_tables(eidx, gates, counts, B, S):
    n_assign = S * TOP_K
    assert n_assign & (n_assign - 1) == 0
    e = eidx.reshape(TOP_K, B, S)
    t = lax.broadcasted_iota(I32, e.shape, 2)
    k = lax.broadcasted_iota(I32, e.shape, 0)
    keys = (e * n_assign + t * TOP_K + k).transpose(1, 0, 2).reshape(B, n_assign)
    spare = jnp.zeros((MOE_BLOCK,), I32)
    skey = jnp.stack([jnp.concatenate([lax.sort(keys[b]), spare]) for b in range(B)], axis=0)
    seg = jnp.cumsum(counts, axis=1) - counts
    nblocks_e = (counts + MOE_BLOCK - 1) // MOE_BLOCK
    bend = jnp.cumsum(nblocks_e, axis=1)
    bstart = bend - nblocks_e
    nb = n_assign // MOE_BLOCK + N_EXPERTS
    bidx = jnp.arange(nb, dtype=I32)
    blk_e = jnp.minimum(jnp.sum((bend[:, None, :] <= bidx[None, :, None]).astype(I32), axis=-1), N_EXPERTS - 1)
    onehot = (blk_e[:, :, None] == jnp.arange(N_EXPERTS, dtype=I32)[None, None, :]).astype(I32)
    pick = lambda v: jnp.sum(onehot * v[:, None, :], axis=-1)
    off = (bidx[None, :] - pick(bstart)) * MOE_BLOCK
    row0 = pick(seg) + off
    nvalid = jnp.clip(pick(counts) - off, 0, MOE_BLOCK)
    used = (counts > 0).astype(I32)
    rank = jnp.cumsum(used, axis=1) - 1
    blk_ord = pick(rank)
    slots = jnp.arange(N_EXPERTS, dtype=I32)
    uexp = jnp.sum(jnp.where((rank[:, None, :] == slots[None, :, None]) & (used[:, None, :] > 0), slots[None, None, :], 0),
                   axis=-1)
    gflat = gates.reshape(TOP_K, B, S).transpose(1, 2, 0).reshape(B, n_assign)
    gflat = jnp.concatenate([gflat, jnp.zeros((B, TOP_K), F32)], axis=1)
    return blk_ord, uexp, jnp.sum(used, axis=1, keepdims=True), row0, nvalid, bend[:, -1:], skey, gflat


def _shared_kernel(h_ref, wg_ref, wu_ref, wd_ref, routed_ref, x1_ref, gt_ref, o_ref):
    hb = h_ref[0].astype(BF16)
    act = _silu(_dot(hb, wg_ref[...])) * _dot(hb, wu_ref[...])
    shared = _dot(act.astype(BF16), wd_ref[...])
    o_ref[0] = x1_ref[0] + gt_ref[0] * (_from_token_tiles(routed_ref, hb.shape[0]) + shared)


def _shared_and_residual(h2, wsg, wsu, wsd, routed, x1, gt2, tm):
    B, S, D = h2.shape
    row = lambda b, i: (b, i, 0)
    xs = pl.BlockSpec((1, tm, D), row)
    const2 = lambda a: pl.BlockSpec(a.shape, lambda b, i: (0, 0))
    return pl.pallas_call(
        _shared_kernel, grid=(B, S // tm),
        in_specs=[xs, const2(wsg), const2(wsu), const2(wsd), pl.BlockSpec((1, tm * (D // LANES), LANES), row), xs,
                  pl.BlockSpec((1, 1, D), lambda b, i: (b, 0, 0))],
        out_specs=xs,
        out_shape=jax.ShapeDtypeStruct((B, S, D), F32),
        compiler_params=_cparams(("parallel", "parallel")),
        name="shared_expert_residual",
    )(h2, wsg, wsu, wsd, routed, x1, gt2)


def _prep_in_weights(w_in, g_q_dil, g_k_dil, g_q_nsa, g_k_slc, g_k_win):
    D = w_in.shape[0]
    pad = jnp.zeros((D, IN_COLS_PADDED - w_in.shape[1]), w_in.dtype)
    w_all = jnp.concatenate([w_in, pad], axis=1).astype(BF16)
    one = jnp.ones((LANES,), F32)
    gcol = jnp.concatenate([
        jnp.tile(g_q_dil, NH_DIL), jnp.tile(g_k_dil, NH_DIL), jnp.ones((512,), F32), jnp.tile(g_q_nsa, NH_NSA),
        one, one, jnp.tile(g_k_slc, NKV_NSA), one, jnp.tile(g_k_win, NKV_NSA), one, one]).reshape(1, IN_COLS_PADDED)
    return w_all, gcol


def _rope_tables(S):
    inv_freq = ROPE_THETA ** (-jnp.arange(ROT_HALF, dtype=F32) / ROT_HALF)
    ang = jnp.arange(S).astype(F32)[:, None] * inv_freq[None, :]
    cos, sin = jnp.cos(ang), jnp.sin(ang)
    zeros = jnp.zeros((S, HD - ROT_DIM), F32)
    z8 = jnp.zeros((S, ROT_HALF), F32)
    cos_h = jnp.concatenate([cos, cos, jnp.ones((S, HD - ROT_DIM), F32)], axis=1)
    sa_h = jnp.concatenate([-sin, z8, zeros], axis=1)
    sb_h = jnp.concatenate([z8, sin, zeros], axis=1)
    two = lambda t: jnp.concatenate([t, t], axis=1)
    return two(cos_h), two(sa_h), two(sb_h)


def _prep_compress(pe, w1, w2):
    eye = jnp.eye(NKV_NSA, dtype=F32)
    w1r = w1.reshape(CMP_LEN, HD, CMP_HIDDEN)

    def half(w1h, peh):
        w = jnp.einsum("ldh,gk->lgdkh", w1h, eye).reshape(CMP_STRIDE * LANES, NKV_NSA * CMP_HIDDEN)
        p = jnp.broadcast_to(peh[:, None, :], (CMP_STRIDE, NKV_NSA, HD)).reshape(1, CMP_STRIDE * LANES)
        return p, w.astype(BF16)

    pa, wa = half(w1r[:CMP_STRIDE], pe[:CMP_STRIDE])
    pb, wb = half(w1r[CMP_STRIDE:], pe[CMP_STRIDE:])
    w2bd = jnp.einsum("hd,gk->ghkd", w2, eye).reshape(NKV_NSA * CMP_HIDDEN, LANES).astype(BF16)
    return pa, pb, wa, wb, w2bd


def _overlap_tables(S):
    ncb = S // CMP_STRIDE
    n_sel = S // SEL_BLOCK
    cs = np.arange(ncb) * CMP_STRIDE
    ss = np.arange(n_sel) * SEL_BLOCK
    ov = np.clip(np.minimum(cs[:, None] + CMP_LEN, ss[None, :] + SEL_BLOCK) - np.maximum(cs[:, None], ss[None, :]), 0, None)
    ov = ov.astype(np.float32) / CMP_STRIDE
    ovl = np.zeros((ncb, LANES), np.float32)
    ovr = np.zeros((ncb, LANES), np.float32)
    ovl[:, :n_sel] = ov
    ovr[:, HD:HD + n_sel] = ov
    return jnp.asarray(ovl, BF16), jnp.asarray(ovr, BF16)


def _block_ones():
    r = np.arange(LANES)
    return jnp.asarray((r[:, None] // HD == r[None, :] // HD).astype(np.float32), BF16)


def _layer(x, c, w_ada, b_ada, g_norm_mix, g_norm_ffn, w_in, g_q_dil, g_k_dil, g_q_nsa, g_k_cmp, g_k_slc, g_k_win,
           cmp_pe_k, cmp_w1_k, cmp_w2_k, cmp_pe_v, cmp_w1_v, cmp_w2_v, w_out, w_router, router_bias,
           w_gate, w_up, w_down, ws_gate, ws_up, ws_down):
    B, S, D = x.shape
    ones_bd = _block_ones()

    c_pad = jnp.zeros((8, D), F32).at[:B].set(c)
    mod = _ada_mod(c_pad, w_ada, b_ada.reshape(1, -1))[:B]
    sh1, sc1, gt1, sh2, sc2, gt2 = [m.reshape(B, 1, D) for m in jnp.split(mod, 6, axis=-1)]

    w_all, gcol = _prep_in_weights(w_in, g_q_dil, g_k_dil, g_q_nsa, g_k_slc, g_k_win)
    cos_t, sa_t, sb_t = _rope_tables(S)
    qa, ka, va, qn, qr, kc, vc, ksl, vsl, kw, vw, gate = _in_projection(
        x, g_norm_mix.reshape(1, D), sc1, sh1, w_all, gcol, cos_t, sa_t, sb_t, ones_bd, tm=512)

    o_a = _dilated_attention(qa, ka, va)

    kcmp, vcmp = _compress(kc, vc, _prep_compress(cmp_pe_k, cmp_w1_k, cmp_w2_k),
                           _prep_compress(cmp_pe_v, cmp_w1_v, cmp_w2_v), jnp.tile(g_k_cmp, NKV_NSA).reshape(1, LANES),
                           ones_bd)
    ovl, ovr = _overlap_tables(S)
    o_cmp, sel = _cmp_select(qn, kcmp, vcmp, ovl, ovr, gate, tq=128)
    o_cs = _nsa_branch(qr, ksl, vsl, sel, o_cmp, gate, tq=256, tk=512, mode="selected", out_dtype=F32)
    o_b = _nsa_branch(qr, kw, vw, None, o_cs, gate, tq=256, tk=256, mode="window", out_dtype=BF16)

    w_out_b = w_out.astype(BF16)
    x1, h2, h2_tiles = _out_projection(o_a, o_b, w_out_b[:NH_DIL * HD], w_out_b[NH_DIL * HD:], x, gt1,
                                       g_norm_ffn.reshape(1, D), sc2, sh2, tm=256)

    tm_r = 256
    eidx, gates, counts = _router(h2, w_router.T.astype(BF16),
                                  jnp.broadcast_to(router_bias.reshape(N_EXPERTS, 1), (N_EXPERTS, tm_r)), tm=tm_r)
    tables = _dispatch_tables(eidx, gates, counts[:, :, 0].astype(I32), B, S)
    routed = jnp.stack([_routed_experts(h2_tiles[b], *[tbl[b] for tbl in tables], w_gate, w_up, w_down)
                        for b in range(B)], axis=0)
    return _shared_and_residual(h2, ws_gate.astype(BF16), ws_up.astype(BF16), ws_down.astype(BF16), routed, x1, gt2,
                                tm=256)


def kernel(x, c, w_ada, b_ada, g_norm_mix, g_norm_ffn, w_in, g_q_dil, g_k_dil, g_q_nsa, g_k_cmp, g_k_slc, g_k_win, cmp_pe_k, cmp_w1_k, cmp_w2_k, cmp_pe_v, cmp_w1_v, cmp_w2_v, w_out, w_router, router_bias, w_gate, w_up, w_down, ws_gate, ws_up, ws_down):
    params = (w_ada, b_ada, g_norm_mix, g_norm_ffn, w_in, g_q_dil, g_k_dil, g_q_nsa, g_k_cmp, g_k_slc, g_k_win,
              cmp_pe_k, cmp_w1_k, cmp_w2_k, cmp_pe_v, cmp_w1_v, cmp_w2_v, w_out, w_router, router_bias,
              w_gate, w_up, w_down, ws_gate, ws_up, ws_down)
    for layer in range(w_ada.shape[0]):
        x = _layer(x, c, *[a[layer] for a in params])
    return x
```

```python
import functools
import math

import numpy as np
import jax
import jax.numpy as jnp
from jax import lax
from jax.experimental import pallas as pl
from jax.experimental.pallas import tpu as pltpu

F32 = jnp.float32
BF16 = jnp.bfloat16
I32 = jnp.int32

HD = 64
LANES = 128
NH_DIL = 8
NH_NSA = 8
NKV_NSA = 2
NSA_REP = NH_NSA // NKV_NSA
DIL_PAIRS = ((128, 1), (512, 4), (2048, 16))
DIL_TILE = 128
DIL_GROUP = 8
ROPE_THETA = 500000.0
ROT_DIM = HD // 4
ROT_HALF = ROT_DIM // 2
CMP_LEN = 32
CMP_STRIDE = 16
CMP_HIDDEN = 128
SEL_BLOCK = 64
SEL_SHIFT = 6
SEL_TOP = 16
NSA_WINDOW = 512
N_EXPERTS = 256
TOP_K = 8
TOP_K_SHIFT = 3
N_GROUPS = 8
GROUP_SIZE = N_EXPERTS // N_GROUPS
TOPK_GROUPS = 4
EXPERT_FF = 256
SHARED_FF = 256
ROUTED_SCALE = 2.5
EPS = 1e-6
QK_SCALE = 1.0 / math.sqrt(HD)

NEG = -1e30
BIG = 3e38

C_QA, C_KA, C_VA, C_QB = 0, 512, 1024, 1536
C_KC, C_VC, C_KSL, C_VSL, C_KW, C_VW, C_GB = 2048, 2176, 2304, 2432, 2560, 2688, 2816
IN_COLS_PADDED = 2944
TRANSPOSED_OUTS = (8, 10)

MOE_BLOCK = 128
NULL_ROWS = 8
WEIGHT_RING = 5
SCATTER_GROUP = 16
VMEM_LIMIT = 48 * 1024 * 1024


def _cparams(sem, vmem=VMEM_LIMIT):
    return pltpu.CompilerParams(dimension_semantics=sem, vmem_limit_bytes=vmem)


def _sigmoid(v):
    return 1.0 / (1.0 + jnp.exp(-v))


def _silu(v):
    return v * _sigmoid(v)


def _dot(a, b):
    return jnp.dot(a, b, preferred_element_type=F32)


def _dot_nt(a, b):
    return lax.dot_general(a, b, (((1,), (1,)), ((), ())), preferred_element_type=F32)


def _lane_lo(shape):
    return lax.broadcasted_iota(I32, shape, len(shape) - 1) < HD


def _head_sums(v, ones_bd):
    hi = v.astype(BF16)
    lo = (v - hi.astype(F32)).astype(BF16)
    return _dot(hi, ones_bd) + _dot(lo, ones_bd)


def _head_rmsnorm(y, ones_bd, gain):
    ms = _head_sums(y * y, ones_bd) * (1.0 / HD)
    return y * lax.rsqrt(ms + EPS) * gain


def _rope(y, cos, sin_a, sin_b):
    return y * cos + pltpu.roll(y, LANES - ROT_HALF, 1) * sin_a + pltpu.roll(y, ROT_HALF, 1) * sin_b


def _ada_kernel(c_ref, w_ref, b_ref, o_ref):
    a = _silu(c_ref[...]).astype(BF16)
    o_ref[...] = _dot(a, w_ref[...].astype(BF16)) + b_ref[...]


def _ada_mod(c_pad, w_ada, b_ada):
    rows, d = c_pad.shape
    n = w_ada.shape[1]
    tn = 1536 if n % 1536 == 0 else n
    return pl.pallas_call(
        _ada_kernel,
        grid=(n // tn,),
        in_specs=[pl.BlockSpec((rows, d), lambda j: (0, 0)),
                  pl.BlockSpec((d, tn), lambda j: (0, j)),
                  pl.BlockSpec((1, tn), lambda j: (0, j))],
        out_specs=pl.BlockSpec((rows, tn), lambda j: (0, j)),
        out_shape=jax.ShapeDtypeStruct((rows, n), F32),
        compiler_params=_cparams(("arbitrary",)),
        name="ada_mod",
    )(c_pad, w_ada, b_ada)


def _inproj_kernel(x_ref, g_ref, sc_ref, sh_ref, w_ref, gcol_ref, cos_ref, sa_ref, sb_ref, ones_ref,
                   qa_ref, ka_ref, va_ref, qn_ref, qr_ref, kc_ref, vc_ref, ksl_ref, vsl_ref, kw_ref, vw_ref,
                   gate_ref):
    x = x_ref[0]
    ms = jnp.mean(x * x, axis=-1, keepdims=True)
    h = (x * lax.rsqrt(ms + EPS)) * g_ref[...]
    h = h * (1.0 + sc_ref[0]) + sh_ref[0]
    hb = h.astype(BF16)
    ones_bd = ones_ref[...]
    cos, sa, sb = cos_ref[...], sa_ref[...], sb_ref[...]
    lo = _lane_lo(cos.shape)

    def proj(c0, width):
        return _dot(hb, w_ref[:, c0:c0 + width])

    def normed(tile, c0):
        return _head_rmsnorm(tile, ones_bd, gcol_ref[:, c0:c0 + LANES])

    acc = proj(C_QA, 512)
    for p in range(4):
        y = _rope(normed(acc[:, p * LANES:(p + 1) * LANES], C_QA + p * LANES), cos, sa, sb) * QK_SCALE
        qa_ref[0, :, p * LANES:(p + 1) * LANES] = y
    acc = proj(C_KA, 512)
    for p in range(4):
        y = _rope(normed(acc[:, p * LANES:(p + 1) * LANES], C_KA + p * LANES), cos, sa, sb)
        ka_ref[0, :, p * LANES:(p + 1) * LANES] = y
    va_ref[0] = proj(C_VA, 512)

    acc = proj(C_QB, 512)
    for p in range(4):
        yn = normed(acc[:, p * LANES:(p + 1) * LANES], C_QB + p * LANES)
        yr = _rope(yn, cos, sa, sb)
        for y, ref in ((yn * QK_SCALE, qn_ref), (yr * QK_SCALE, qr_ref)):
            ysw = pltpu.roll(y, HD, 1)
            for half in range(2):
                head = 2 * p + half
                grp = head // NSA_REP
                src = y if grp == half else ysw
                keep = lo if grp == 0 else jnp.logical_not(lo)
                ref[0, :, head * LANES:(head + 1) * LANES] = jnp.where(keep, src, 0.0).astype(BF16)

    acc = proj(C_KC, 512)
    kc_ref[0] = acc[:, 0:LANES]
    vc_ref[0] = acc[:, LANES:2 * LANES]
    ksl_ref[0] = _rope(normed(acc[:, 2 * LANES:3 * LANES], C_KSL), cos, sa, sb).astype(BF16)
    vsl_ref[0] = _transposed(acc[:, 3 * LANES:4 * LANES]).astype(BF16)
    acc = proj(C_KW, 384)
    kw_ref[0] = _rope(normed(acc[:, 0:LANES], C_KW), cos, sa, sb).astype(BF16)
    vw_ref[0] = _transposed(acc[:, LANES:2 * LANES]).astype(BF16)
    gate_ref[0] = _sigmoid(acc[:, 2 * LANES:3 * LANES])


def _transposed(tile):
    return jnp.concatenate([tile[r0:r0 + LANES].T for r0 in range(0, tile.shape[0], LANES)], axis=1)


def _in_projection(x, g_mix, sc1, sh1, w_all, gcol, cos_t, sa_t, sb_t, ones_bd, tm):
    B, S, D = x.shape
    nc = w_all.shape[1]
    row = lambda b, i: (b, i, 0)
    bvec = pl.BlockSpec((1, 1, D), lambda b, i: (b, 0, 0))
    tab = pl.BlockSpec((tm, LANES), lambda b, i: (i, 0))
    const2 = lambda shape: pl.BlockSpec(shape, lambda b, i: (0, 0))
    widths_dt = [(512, F32), (512, F32), (512, F32), (1024, BF16), (1024, BF16), (LANES, F32), (LANES, F32),
                 (LANES, BF16), (LANES, BF16), (LANES, BF16), (LANES, BF16), (LANES, F32)]
    return pl.pallas_call(
        _inproj_kernel,
        grid=(B, S // tm),
        in_specs=[pl.BlockSpec((1, tm, D), row), const2((1, D)), bvec, bvec, const2((D, nc)), const2((1, nc)),
                  tab, tab, tab, const2((LANES, LANES))],
        out_specs=[pl.BlockSpec((1, tm, w), row) if n not in TRANSPOSED_OUTS else pl.BlockSpec((1, w, tm), lambda b, i: (b, 0, i))
                   for n, (w, _) in enumerate(widths_dt)],
        out_shape=[jax.ShapeDtypeStruct((B, S, w) if n not in TRANSPOSED_OUTS else (B, w, S), dt)
                   for n, (w, dt) in enumerate(widths_dt)],
        compiler_params=_cparams(("parallel", "parallel")),
        name="in_projection",
    )(x, g_mix, sc1, sh1, w_all, gcol, cos_t, sa_t, sb_t, ones_bd)


def _dot_tn(a, b):
    return lax.dot_general(a, b, (((0,), (0,)), ((), ())), preferred_element_type=F32)


def _dilated_kernel(q_ref, k_ref, v_ref, o_ref, oacc, lacc, *, sb_rows, pairs):
    sb = pl.program_id(2)
    base = sb * sb_rows
    t = DIL_TILE
    kk = lax.broadcasted_iota(I32, (t, t), 0)
    qq = lax.broadcasted_iota(I32, (t, t), 1)
    bias_cur = jnp.where(qq >= kk, 0.0, NEG)
    bias_prev = jnp.where(qq <= kk, 0.0, NEG)
    lane_lo = _lane_lo((t, LANES))
    row_lo = lax.broadcasted_iota(I32, (LANES, t), 0) < HD

    def blocks(d, starts, mode):
        def strided(ref, start):
            return ref[0, pl.ds(start, t, stride=d), :] if d > 1 else ref[0, pl.ds(start, t), :]

        work = []
        for qs, first in starts:
            qg = base + qs
            rows_q = pl.ds(qs, t, stride=d) if d > 1 else pl.ds(qs, t)
            prev_start = jnp.where(first, qg, qg - t * d)
            qt = q_ref[0, rows_q, :].astype(BF16)
            zero = jnp.zeros_like(qt)
            q2 = jnp.concatenate([jnp.where(lane_lo, qt, zero), jnp.where(lane_lo, zero, qt)], axis=0)
            k2 = jnp.concatenate([strided(k_ref, prev_start), strided(k_ref, qg)], axis=0).astype(BF16)
            v2 = jnp.concatenate([strided(v_ref, prev_start), strided(v_ref, qg)], axis=0).astype(BF16)
            bias = jnp.concatenate([jnp.where(first, NEG, bias_prev), bias_cur], axis=0)
            s = _dot_nt(k2, q2) + jnp.concatenate([bias, bias], axis=1)
            work.append([rows_q, v2, s])
        for w in work:
            s = w[2]
            m = s.max(axis=0, keepdims=True)
            p = jnp.exp(s - m)
            l = p.sum(axis=0, keepdims=True)
            w[2:] = [p.astype(BF16), l, m + jnp.log(l)]
        for w in work:
            w[1] = _dot_tn(w[1], w[2]) / w[3]
        for w in work:
            ov, lse2 = w[1], w[4]
            w[1:] = [jnp.where(row_lo, ov[:, :t], ov[:, t:]).T, jnp.where(row_lo, lse2[:, :t], lse2[:, t:]).T]
        for rows_q, o, lse in work:
            if mode == "init":
                oacc[rows_q, :] = o
                lacc[rows_q, :] = lse
                continue
            lp = lacc[rows_q, :]
            mx = jnp.maximum(lp, lse)
            wp, wn = jnp.exp(lp - mx), jnp.exp(lse - mx)
            den = wp + wn
            oacc[rows_q, :] = (oacc[rows_q, :] * wp + o * wn) / den
            if mode == "mid":
                lacc[rows_q, :] = mx + jnp.log(den)

    modes = ("init",) + ("mid",) * (len(pairs) - 2) + ("last",)
    for (window, d), mode in zip(pairs, modes):
        assert window // d == t and d & (d - 1) == 0
        n_blocks = sb_rows // t

        def body(i, c, d=d, mode=mode):
            starts = []
            for u in range(DIL_GROUP):
                n = i * DIL_GROUP + u
                r, j = n & (d - 1), n >> (d.bit_length() - 1)
                starts.append((j * (t * d) + r, jnp.logical_and(sb == 0, j == 0)))
            blocks(d, starts, mode)
            return c

        lax.fori_loop(0, n_blocks // DIL_GROUP, body, 0)
    o_ref[0] = oacc[...].astype(o_ref.dtype)


def _dilated_attention(qa, ka, va):
    B, S, w = qa.shape
    sb_rows = DIL_TILE * max(d for _, d in DIL_PAIRS)
    assert S % sb_rows == 0
    kern = functools.partial(_dilated_kernel, sb_rows=sb_rows, pairs=DIL_PAIRS)
    whole = pl.BlockSpec((1, S, LANES), lambda b, p, s: (b, 0, p))
    blk = pl.BlockSpec((1, sb_rows, LANES), lambda b, p, s: (b, s, p))
    acc = pltpu.VMEM((sb_rows, LANES), F32)
    return pl.pallas_call(
        kern, grid=(B, w // LANES, S // sb_rows),
        in_specs=[blk, whole, whole], out_specs=blk,
        out_shape=jax.ShapeDtypeStruct((B, S, w), BF16),
        scratch_shapes=[acc, acc],
        compiler_params=_cparams(("parallel", "parallel", "arbitrary")),
        name="dilated_attention",
    )(qa, ka, va)


def _compress_kernel(kc_ref, vc_ref, pak_ref, pbk_ref, wak_ref, wbk_ref, w2k_ref, gk_ref, ones_ref,
                     pav_ref, pbv_ref, wav_ref, wbv_ref, w2v_ref, ko_ref, vo_ref):
    ncb = kc_ref.shape[1]

    def branch(t_ref, pa, pb, wa, wb, w2):
        t = t_ref[0]
        ua = _dot((t + pa[...]).astype(BF16), wa[...])
        ub = _dot((t + pb[...]).astype(BF16), wb[...])
        pre = ua + pltpu.roll(ub, ncb - 1, 0)
        return _dot(jax.nn.gelu(pre).astype(BF16), w2[...])

    live = lax.broadcasted_iota(I32, (ncb, LANES), 0) < ncb - 1
    kcmp = _head_rmsnorm(branch(kc_ref, pak_ref, pbk_ref, wak_ref, wbk_ref, w2k_ref), ones_ref[...], gk_ref[...])
    ko_ref[0] = jnp.where(live, kcmp, 0.0).astype(BF16)
    vo_ref[0] = jnp.where(live, branch(vc_ref, pav_ref, pbv_ref, wav_ref, wbv_ref, w2v_ref), 0.0).astype(BF16)


def _compress(kc, vc, wk, wv, gk2, ones_bd):
    B, S, _ = kc.shape
    ncb = S // CMP_STRIDE
    cw = CMP_STRIDE * LANES
    view = lambda a: a.reshape(B, ncb, cw)
    blk = pl.BlockSpec((1, ncb, cw), lambda b: (b, 0, 0))
    full = lambda a: pl.BlockSpec(a.shape, lambda b: (0,) * a.ndim)
    consts = [*wk, gk2, ones_bd, *wv]
    o_spec = pl.BlockSpec((1, ncb, LANES), lambda b: (b, 0, 0))
    return pl.pallas_call(
        _compress_kernel, grid=(B,),
        in_specs=[blk, blk] + [full(a) for a in consts],
        out_specs=[o_spec, o_spec],
        out_shape=[jax.ShapeDtypeStruct((B, ncb, LANES), BF16)] * 2,
        compiler_params=_cparams(("parallel",)),
        name="nsa_compress",
    )(view(kc), view(vc), *consts)


def _pair_tile(lo, t0, half0, t1, half1):
    a = t0 if half0 == 0 else pltpu.roll(t0, HD, 1)
    b = t1 if half1 == 1 else pltpu.roll(t1, HD, 1)
    return jnp.where(lo, a, b)


def _cmpsel_kernel(qn_ref, kc_ref, vc_ref, ovl_ref, ovr_ref, gate_ref, o_ref, sel_ref, *, tq, n_sel, top):
    ncb = kc_ref.shape[1]
    i = pl.program_id(1)
    pos0 = i * tq
    qpos = pos0 + lax.broadcasted_iota(I32, (tq, ncb), 0)
    cend = lax.broadcasted_iota(I32, (tq, ncb), 1) * CMP_STRIDE + (CMP_LEN - 1)
    valid = cend <= qpos
    lo = _lane_lo((tq, LANES))
    kc = kc_ref[0]
    vc = vc_ref[0]

    psum = [None] * NKV_NSA
    scores = [_dot_nt(qn_ref[0, :, h * LANES:(h + 1) * LANES], kc) for h in range(NH_NSA)]
    probs = []
    for h, s in enumerate(scores):
        g = h // NSA_REP
        m = jnp.where(valid, s, NEG).max(axis=1, keepdims=True)
        m = jnp.where(m > 0.5 * NEG, m, 0.0)
        p = jnp.where(valid, jnp.exp(s - m), 0.0)
        l = p.sum(axis=1, keepdims=True)
        p = p / jnp.where(l > 0.0, l, 1.0)
        probs.append(p.astype(BF16))
        psum[g] = p if psum[g] is None else psum[g] + p
    outs = [_dot(p, vc) for p in probs]

    for p_ in range(NH_NSA // 2):
        h0, h1 = 2 * p_, 2 * p_ + 1
        o_pair = _pair_tile(lo, outs[h0], h0 // NSA_REP, outs[h1], h1 // NSA_REP)
        g = jnp.where(lo, gate_ref[0, :, 3 * h0:3 * h0 + 1], gate_ref[0, :, 3 * h1:3 * h1 + 1])
        o_ref[0, :, p_ * LANES:(p_ + 1) * LANES] = g * o_pair

    def hilo_dot(a, w):
        hi = a.astype(BF16)
        return _dot(hi, w) + _dot((a - hi.astype(F32)).astype(BF16), w)

    pslc = hilo_dot(psum[0], ovl_ref[...]) + hilo_dot(psum[1], ovr_ref[...])
    pslc_t = pslc.T
    jblk = lax.broadcasted_iota(I32, (HD, tq), 0)
    cur = (pos0 + lax.broadcasted_iota(I32, (HD, tq), 1)) >> SEL_SHIFT
    forced = (jblk == 0) | (jblk == cur) | (jblk == cur - 1)
    sel_t = []
    for g in range(NKV_NSA):
        sc = jnp.where(forced, BIG, jnp.where(jblk <= cur, pslc_t[g * HD:(g + 1) * HD, :], -BIG))
        rank = jnp.zeros((HD, tq), F32)
        for k in range(n_sel):
            rk = sc[k:k + 1, :]
            rank = rank + jnp.where(jblk > k, jnp.where(rk >= sc, 1.0, 0.0), jnp.where(rk > sc, 1.0, 0.0))
        sel_t.append(jnp.where(rank < top, 1.0, 0.0))
    sel_ref[0] = jnp.concatenate(sel_t, axis=0)


def _cmp_select(qn, kcmp, vcmp, ovl, ovr, gate, tq):
    B, S, qw = qn.shape
    ncb = kcmp.shape[1]
    n_sel = S // SEL_BLOCK
    assert n_sel <= HD and tq == LANES
    row = lambda b, i: (b, i, 0)
    cb = pl.BlockSpec((1, ncb, LANES), lambda b, i: (b, 0, 0))
    ovs = pl.BlockSpec((ncb, LANES), lambda b, i: (0, 0))
    kern = functools.partial(_cmpsel_kernel, tq=tq, n_sel=n_sel, top=min(SEL_TOP, n_sel))
    return pl.pallas_call(
        kern, grid=(B, S // tq),
        in_specs=[pl.BlockSpec((1, tq, qw), row), cb, cb, ovs, ovs, pl.BlockSpec((1, tq, LANES), row)],
        out_specs=[pl.BlockSpec((1, tq, NH_NSA * HD), row), pl.BlockSpec((1, LANES, tq), lambda b, i: (b, 0, i))],
        out_shape=[jax.ShapeDtypeStruct((B, S, NH_NSA * HD), F32), jax.ShapeDtypeStruct((B, LANES, S), F32)],
        compiler_params=_cparams(("parallel", "parallel")),
        name="nsa_cmp_select",
    )(qn, kcmp, vcmp, ovl, ovr, gate)


def _nsa_kernel(*refs, tq, tk, mode):
    qi_ref, tile_ref, edge_ref = refs[:3]
    if mode == "selected":
        q_ref, k_ref, vt_ref, selt_ref, prev_ref, gate_ref, o_ref, m_ref, l_ref, acc_ref = refs[3:]
    else:
        q_ref, k_ref, vt_ref, prev_ref, gate_ref, o_ref, m_ref, l_ref, acc_ref = refs[3:]
    step = pl.program_id(1)
    i = qi_ref[step]
    tile = tile_ref[step]
    edge = edge_ref[step]

    @pl.when((edge & 1) != 0)
    def _():
        m_ref[...] = jnp.full(m_ref.shape, NEG, F32)
        l_ref[...] = jnp.zeros(l_ref.shape, F32)
        acc_ref[...] = jnp.zeros(acc_ref.shape, F32)

    def accumulate_tile():
        kpos = tile * tk + lax.broadcasted_iota(I32, (tk, tq), 0)
        qpos = i * tq + lax.broadcasted_iota(I32, (tk, tq), 1)
        k = k_ref[0]
        vt = vt_ref[0]
        if mode == "window":
            dist = qpos - kpos
            bias_w = jnp.where(dist >= 0, jnp.where(dist < NSA_WINDOW, 0.0, NEG), NEG)
        groups = range(NKV_NSA)
        scores = []
        for g in groups:
            if mode == "selected":
                nblk = tk // SEL_BLOCK
                rows = selt_ref[0, pl.ds(pl.multiple_of(g * HD + tile * nblk, nblk), nblk), :]
                chosen = jnp.broadcast_to(rows[:, None, :], (nblk, SEL_BLOCK, tq)).reshape(tk, tq) > 0.5
                bias = jnp.where(kpos <= qpos, jnp.where(chosen, 0.0, NEG), NEG)
            else:
                bias = bias_w
            heads = range(g * NSA_REP, (g + 1) * NSA_REP)
            q4 = jnp.concatenate([q_ref[0, :, h * LANES:(h + 1) * LANES] for h in heads], axis=0)
            scores.append(_dot_nt(k, q4) + jnp.concatenate([bias] * NSA_REP, axis=1))
        probs = []
        for g in groups:
            m_prev = m_ref[g]
            m_new = jnp.maximum(m_prev, scores[g].max(axis=0, keepdims=True))
            alpha = jnp.exp(m_prev - m_new)
            p = jnp.exp(scores[g] - m_new[0:1])
            l_ref[g] = alpha * l_ref[g] + p.sum(axis=0, keepdims=True)
            m_ref[g] = m_new
            probs.append((alpha[0:1], p.astype(BF16)))
        for g in groups:
            alpha, p = probs[g]
            acc_ref[g] = alpha * acc_ref[g] + _dot(vt, p)

    accumulate_tile()

    @pl.when((edge & 2) != 0)
    def _():
        lo = _lane_lo((tq, LANES))
        gate_col = 1 if mode == "selected" else 2

        def head_t(h):
            g, r = divmod(h, NSA_REP)
            cols = slice(r * tq, (r + 1) * tq)
            return acc_ref[g, g * HD:(g + 1) * HD, cols] / l_ref[g, 0:1, cols]

        for p_ in range(NH_NSA // 2):
            h0, h1 = 2 * p_, 2 * p_ + 1
            o_pair = jnp.concatenate([head_t(h0), head_t(h1)], axis=0).T
            c0, c1 = 3 * h0 + gate_col, 3 * h1 + gate_col
            gt = jnp.where(lo, gate_ref[0, :, c0:c0 + 1], gate_ref[0, :, c1:c1 + 1])
            sl = slice(p_ * LANES, (p_ + 1) * LANES)
            o_ref[0, :, sl] = (prev_ref[0, :, sl] + gt * o_pair).astype(o_ref.dtype)


def _nsa_branch(qr, k, v_t, sel_t, prev, gate, *, tq, tk, mode, out_dtype):
    B, S, qw = qr.shape
    ow = NH_NSA * HD
    steps = []
    for i in range(S // tq):
        hi = (i * tq + tq - 1) // tk
        lo = 0 if mode == "selected" else max((i * tq - (NSA_WINDOW - 1)) // tk, 0)
        steps += [(i, t, (t == lo) + 2 * (t == hi)) for t in range(lo, hi + 1)]
    tables = [jnp.asarray(np.array(col, np.int32)) for col in zip(*steps)]
    row = lambda b, s, qi, tl, ed: (b, qi[s], 0)
    in_specs = [pl.BlockSpec((1, tq, qw), row),
                pl.BlockSpec((1, tk, LANES), lambda b, s, qi, tl, ed: (b, tl[s], 0)),
                pl.BlockSpec((1, LANES, tk), lambda b, s, qi, tl, ed: (b, 0, tl[s]))]
    args = [qr, k, v_t]
    if mode == "selected":
        in_specs.append(pl.BlockSpec((1, LANES, tq), lambda b, s, qi, tl, ed: (b, 0, qi[s])))
        args.append(sel_t)
    in_specs += [pl.BlockSpec((1, tq, ow), row), pl.BlockSpec((1, tq, LANES), row)]
    args += [prev, gate]
    stat = pltpu.VMEM((NKV_NSA, 8, NSA_REP * tq), F32)
    grid_spec = pltpu.PrefetchScalarGridSpec(
        num_scalar_prefetch=3, grid=(B, len(steps)), in_specs=in_specs,
        out_specs=pl.BlockSpec((1, tq, ow), row),
        scratch_shapes=[stat, stat, pltpu.VMEM((NKV_NSA, LANES, NSA_REP * tq), F32)])
    return pl.pallas_call(
        functools.partial(_nsa_kernel, tq=tq, tk=tk, mode=mode), grid_spec=grid_spec,
        out_shape=jax.ShapeDtypeStruct((B, S, ow), out_dtype),
        compiler_params=_cparams(("parallel", "arbitrary")),
        name=f"nsa_{mode}_attention",
    )(*tables, *args)


def _to_token_tiles(ref, value):
    m, d = value.shape
    rpt = d // LANES
    for j in range(rpt):
        ref[0, pl.ds(j, m, stride=rpt), :] = value[:, j * LANES:(j + 1) * LANES]


def _from_token_tiles(ref, m):
    rpt = ref.shape[1] // m
    return jnp.concatenate([ref[0, pl.ds(j, m, stride=rpt), :] for j in range(rpt)], axis=1)


def _outproj_kernel(oa_ref, ob_ref, wt_ref, wb_ref, x_ref, gt_ref, g_ref, sc_ref, sh_ref, x1_ref, h2_ref, h2t_ref):
    mix = _dot(oa_ref[0], wt_ref[...]) + _dot(ob_ref[0], wb_ref[...])
    x1 = x_ref[0] + gt_ref[0] * mix
    x1_ref[0] = x1
    ms = jnp.mean(x1 * x1, axis=-1, keepdims=True)
    h = (x1 * lax.rsqrt(ms + EPS)) * g_ref[...]
    h2 = h * (1.0 + sc_ref[0]) + sh_ref[0]
    h2_ref[0] = h2.astype(h2_ref.dtype)
    _to_token_tiles(h2t_ref, h2)


def _out_projection(oa, ob, w_top, w_bot, x, gt1, g_ffn, sc2, sh2, tm):
    B, S, D = x.shape
    row = lambda b, i: (b, i, 0)
    bvec = pl.BlockSpec((1, 1, D), lambda b, i: (b, 0, 0))
    const2 = lambda a: pl.BlockSpec(a.shape, lambda b, i: (0, 0))
    xs = pl.BlockSpec((1, tm, D), row)
    return pl.pallas_call(
        _outproj_kernel, grid=(B, S // tm),
        in_specs=[pl.BlockSpec((1, tm, oa.shape[2]), row), pl.BlockSpec((1, tm, ob.shape[2]), row),
                  const2(w_top), const2(w_bot), xs, bvec, const2(g_ffn), bvec, bvec],
        out_specs=[xs, xs, pl.BlockSpec((1, tm * (D // LANES), LANES), row)],
        out_shape=[jax.ShapeDtypeStruct((B, S, D), F32), jax.ShapeDtypeStruct((B, S, D), BF16),
                   jax.ShapeDtypeStruct((B, S * (D // LANES), LANES), F32)],
        compiler_params=_cparams(("parallel", "parallel")),
        name="out_projection",
    )(oa, ob, w_top, w_bot, x, gt1, g_ffn, sc2, sh2)


def _router_kernel(h_ref, wr_ref, bias_ref, eidx_ref, gate_ref, cnt_ref):
    tm = h_ref.shape[1]
    logits = _dot_nt(wr_ref[...], h_ref[0])
    aff = _sigmoid(logits)
    biased = aff + bias_ref[...]
    b3 = biased.reshape(N_GROUPS, GROUP_SIZE, tm)
    jj = lax.broadcasted_iota(I32, (N_GROUPS, GROUP_SIZE, tm), 1).astype(F32)
    m1 = b3.max(axis=1, keepdims=True)
    i1 = jnp.where(b3 == m1, jj, 1e9).min(axis=1, keepdims=True)
    m2 = jnp.where(jj == i1, -jnp.inf, b3).max(axis=1, keepdims=True)
    gs = (m1 + m2).reshape(N_GROUPS, tm)
    gi = lax.broadcasted_iota(I32, (N_GROUPS, tm), 0)
    rank = jnp.zeros((N_GROUPS, tm), F32)
    for k in range(N_GROUPS):
        rk = gs[k:k + 1, :]
        rank = rank + jnp.where(gi > k, jnp.where(rk >= gs, 1.0, 0.0), jnp.where(rk > gs, 1.0, 0.0))
    gsel = (rank < TOPK_GROUPS).reshape(N_GROUPS, 1, tm)
    masked = jnp.where(gsel, b3, -jnp.inf).reshape(N_EXPERTS, tm)
    ee = lax.broadcasted_iota(I32, (N_EXPERTS, tm), 0).astype(F32)
    idx_rows, sel_rows = [], []
    chosen = jnp.zeros((N_EXPERTS, tm), F32)
    for _ in range(TOP_K):
        m = masked.max(axis=0, keepdims=True)
        idx = jnp.where(masked == m, ee, 1e9).min(axis=0, keepdims=True)
        hit = ee == idx
        idx_rows.append(idx)
        sel_rows.append(jnp.where(hit, aff, 0.0).sum(axis=0, keepdims=True))
        masked = jnp.where(hit, -jnp.inf, masked)
        chosen = chosen + jnp.where(hit, 1.0, 0.0)
    sel = jnp.concatenate(sel_rows, axis=0)
    eidx_ref[...] = jnp.concatenate(idx_rows, axis=0).astype(I32)
    gate_ref[...] = sel / sel.sum(axis=0, keepdims=True) * ROUTED_SCALE

    @pl.when(pl.program_id(1) == 0)
    def _():
        cnt_ref[...] = jnp.zeros(cnt_ref.shape, F32)

    cnt_ref[0] = cnt_ref[0] + chosen.sum(axis=1, keepdims=True)


def _router(h2, wr_t, bias_b, tm):
    B, S, D = h2.shape
    nt = S // tm
    o_spec = pl.BlockSpec((TOP_K, tm), lambda b, i: (0, b * nt + i))
    return pl.pallas_call(
        _router_kernel, grid=(B, nt),
        in_specs=[pl.BlockSpec((1, tm, D), lambda b, i: (b, i, 0)), pl.BlockSpec(wr_t.shape, lambda b, i: (0, 0)),
                  pl.BlockSpec(bias_b.shape, lambda b, i: (0, 0))],
        out_specs=[o_spec, o_spec, pl.BlockSpec((1, N_EXPERTS, LANES), lambda b, i: (b, 0, 0))],
        out_shape=[jax.ShapeDtypeStruct((TOP_K, B * S), I32), jax.ShapeDtypeStruct((TOP_K, B * S), F32),
                   jax.ShapeDtypeStruct((B, N_EXPERTS, LANES), F32)],
        compiler_params=_cparams(("parallel", "arbitrary")),
        name="moe_router",
    )(h2, wr_t, bias_b)


def _experts_kernel(ord_ref, uexp_ref, nused_ref, row0_ref, nvalid_ref, nblk_ref, skey_ref, gflat_ref,
                    h_all, wg_hbm, wu_hbm, wd_hbm, o_all, h_vmem, acc_ref, xg0, xg1, y0, y1, wg_ring, wu_ring, wd_ring,
                    sem, wsem, *, batch):
    h_hbm, o_hbm = h_all.at[batch], o_all.at[batch]
    b = pl.program_id(0)
    last_step = pl.num_programs(0) - 1
    nblk = nblk_ref[0]
    rpt = wg_hbm.shape[1] // LANES
    tc = h_vmem.shape[0] // rpt - 1
    n_assign = tc * TOP_K
    depth = wg_ring.shape[0]

    def weight_copies(j):
        e, slot = uexp_ref[j], j % depth
        return [pltpu.make_async_copy(hbm.at[e], ring.at[slot], wsem.at[n, slot])
                for n, (hbm, ring) in enumerate(((wg_hbm, wg_ring), (wu_hbm, wu_ring), (wd_hbm, wd_ring)))]

    def start_weights(j):
        @pl.when(j < nused_ref[0])
        def _():
            for cp in weight_copies(j):
                cp.start()

    def block_rows(blk):
        bc = jnp.clip(blk, 0, nblk - 1)
        return row0_ref[bc], nvalid_ref[bc]

    def assignment(rows, r, checked=True):
        row0, nv = rows
        a = skey_ref[row0 + r] & (n_assign - 1)
        return jnp.where(r < nv, a, n_assign) if checked else a

    def tile_of(a):
        off = (a & ~(TOP_K - 1)) if rpt == TOP_K else (a >> TOP_K_SHIFT) * rpt
        return pl.ds(pl.multiple_of(off, rpt), rpt)

    def gather(blk, xg):
        rows = block_rows(blk)
        for r in range(MOE_BLOCK):
            xg[r * rpt:(r + 1) * rpt, :] = h_vmem[tile_of(assignment(rows, r, checked=False)), :]

    def scatter(blk, y):
        blk_rows = block_rows(blk)
        for g0 in range(0, MOE_BLOCK, SCATTER_GROUP):
            updated = []
            for r in range(g0, g0 + SCATTER_GROUP):
                a = assignment(blk_rows, r)
                rows = tile_of(a)
                updated.append((rows, acc_ref[rows, :] + gflat_ref[a] * y[r * rpt:(r + 1) * rpt, :]))
            for rows, v in updated:
                acc_ref[rows, :] = v

    @pl.when(b == 0)
    def _():
        cp = pltpu.make_async_copy(h_hbm, h_vmem.at[pl.ds(0, tc * rpt)], sem)
        cp.start()
        for j in range(depth - 1):
            start_weights(j)
        h_vmem[pl.ds(tc * rpt, rpt), :] = jnp.zeros((rpt, LANES), F32)
        acc_ref[...] = jnp.zeros(acc_ref.shape, F32)
        y0[...] = jnp.zeros(y0.shape, F32)
        y1[...] = jnp.zeros(y1.shape, F32)
        cp.wait()
        gather(0, xg0)

    cur = ord_ref[jnp.minimum(b, nblk - 1)]

    @pl.when(jnp.logical_and(b < nblk, jnp.logical_or(b == 0, ord_ref[jnp.maximum(b, 1) - 1] != cur)))
    def _():
        for cp in weight_copies(cur):
            cp.wait()
        start_weights(cur + depth - 1)

    def step(xg_cur, xg_nxt, y_cur, y_prv):
        slot = cur % depth
        gather(b + 1, xg_nxt)
        xb = jnp.concatenate([xg_cur[pl.ds(j, MOE_BLOCK, stride=rpt), :] for j in range(rpt)], axis=1).astype(BF16)
        gt = _dot(xb, wg_ring[slot].astype(BF16))
        up = _dot(xb, wu_ring[slot].astype(BF16))
        y = _dot((_silu(gt) * up).astype(BF16), wd_ring[slot].astype(BF16))
        for j in range(rpt):
            y_cur[pl.ds(j, MOE_BLOCK, stride=rpt), :] = y[:, j * LANES:(j + 1) * LANES]
        scatter(b - 1, y_prv)

    @pl.when(jnp.logical_and(b <= nblk, (b & 1) == 0))
    def _():
        step(xg0, xg1, y0, y1)

    @pl.when(jnp.logical_and(b <= nblk, (b & 1) == 1))
    def _():
        step(xg1, xg0, y1, y0)

    @pl.when(b == last_step)
    def _():
        cp = pltpu.make_async_copy(acc_ref.at[pl.ds(0, tc * rpt)], o_hbm, sem)
        cp.start()
        cp.wait()


def _routed_experts(h_tiles, batch, blk_ord, uexp, nused, row0, nvalid, nblk, skey, gflat, w_gate, w_up, w_down):
    D, ff = w_gate.shape[1], w_gate.shape[2]
    rpt = D // LANES
    tc = h_tiles.shape[1] // rpt
    nb = blk_ord.shape[0]
    n_prefetch = 8
    hbm = pl.BlockSpec(memory_space=pl.ANY)
    rows = pltpu.VMEM((MOE_BLOCK * rpt, LANES), F32)
    resident = pltpu.VMEM(((tc + 1) * rpt, LANES), F32)
    grid_spec = pltpu.PrefetchScalarGridSpec(
        num_scalar_prefetch=n_prefetch, grid=(nb + 1,),
        in_specs=[hbm, hbm, hbm, hbm],
        out_specs=hbm,
        scratch_shapes=[resident, resident, rows, rows, rows, rows,
                        pltpu.VMEM((WEIGHT_RING, D, ff), F32), pltpu.VMEM((WEIGHT_RING, D, ff), F32),
                        pltpu.VMEM((WEIGHT_RING, ff, D), F32),
                        pltpu.SemaphoreType.DMA(()), pltpu.SemaphoreType.DMA((3, WEIGHT_RING))],
    )
    return pl.pallas_call(
        functools.partial(_experts_kernel, batch=batch), grid_spec=grid_spec,
        out_shape=jax.ShapeDtypeStruct(h_tiles.shape, F32),
        input_output_aliases={n_prefetch: 0},
        compiler_params=_cparams(("arbitrary",), vmem=56 * 1024 * 1024),
        name="moe_routed_experts",
    )(blk_ord, uexp, nused, row0, nvalid, nblk, skey, gflat, h_tiles, w_gate, w_up, w_down)


def _dispatch_tables(eidx, gates, counts, B, S):
    n_assign = S * TOP_K
    assert n_assign & (n_assign - 1) == 0
    e = eidx.reshape(TOP_K, B, S)
    t = lax.broadcasted_iota(I32, e.shape, 2)
    k = lax.broadcasted_iota(I32, e.shape, 0)
    keys = (e * n_assign + t * TOP_K + k).transpose(1, 0, 2).reshape(B, n_assign)
    spare = jnp.zeros((MOE_BLOCK,), I32)
    skey = jnp.stack([jnp.concatenate([lax.sort(keys[b]), spare]) for b in range(B)], axis=0)
    seg = jnp.cumsum(counts, axis=1) - counts
    nblocks_e = (counts + MOE_BLOCK - 1) // MOE_BLOCK
    bend = jnp.cumsum(nblocks_e, axis=1)
    bstart = bend - nblocks_e
    nb = n_assign // MOE_BLOCK + N_EXPERTS
    bidx = jnp.arange(nb, dtype=I32)
    blk_e = jnp.minimum(jnp.sum((bend[:, None, :] <= bidx[None, :, None]).astype(I32), axis=-1), N_EXPERTS - 1)
    onehot = (blk_e[:, :, None] == jnp.arange(N_EXPERTS, dtype=I32)[None, None, :]).astype(I32)
    pick = lambda v: jnp.sum(onehot * v[:, None, :], axis=-1)
    off = (bidx[None, :] - pick(bstart)) * MOE_BLOCK
    row0 = pick(seg) + off
    nvalid = jnp.clip(pick(counts) - off, 0, MOE_BLOCK)
    used = (counts > 0).astype(I32)
    rank = jnp.cumsum(used, axis=1) - 1
    blk_ord = pick(rank)
    slots = jnp.arange(N_EXPERTS, dtype=I32)
    uexp = jnp.sum(jnp.where((rank[:, None, :] == slots[None, :, None]) & (used[:, None, :] > 0), slots[None, None, :], 0),
                   axis=-1)
    gflat = gates.reshape(TOP_K, B, S).transpose(1, 2, 0).reshape(B, n_assign)
    gflat = jnp.concatenate([gflat, jnp.zeros((B, TOP_K), F32)], axis=1)
    return blk_ord, uexp, jnp.sum(used, axis=1, keepdims=True), row0, nvalid, bend[:, -1:], skey, gflat


def _shared_kernel(h_ref, wg_ref, wu_ref, wd_ref, routed_ref, x1_ref, gt_ref, o_ref):
    hb = h_ref[0]
    act = _silu(_dot(hb, wg_ref[...])) * _dot(hb, wu_ref[...])
    shared = _dot(act.astype(BF16), wd_ref[...])
    o_ref[0] = x1_ref[0] + gt_ref[0] * (_from_token_tiles(routed_ref, hb.shape[0]) + shared)


def _shared_and_residual(h2, wsg, wsu, wsd, routed, x1, gt2, tm):
    B, S, D = h2.shape
    row = lambda b, i: (b, i, 0)
    xs = pl.BlockSpec((1, tm, D), row)
    const2 = lambda a: pl.BlockSpec(a.shape, lambda b, i: (0, 0))
    return pl.pallas_call(
        _shared_kernel, grid=(B, S // tm),
        in_specs=[xs, const2(wsg), const2(wsu), const2(wsd), pl.BlockSpec((1, tm * (D // LANES), LANES), row), xs,
                  pl.BlockSpec((1, 1, D), lambda b, i: (b, 0, 0))],
        out_specs=xs,
        out_shape=jax.ShapeDtypeStruct((B, S, D), F32),
        compiler_params=_cparams(("parallel", "parallel")),
        name="shared_expert_residual",
    )(h2, wsg, wsu, wsd, routed, x1, gt2)


def _prep_in_weights(w_in, g_q_dil, g_k_dil, g_q_nsa, g_k_slc, g_k_win):
    D = w_in.shape[0]
    pad = jnp.zeros((D, IN_COLS_PADDED - w_in.shape[1]), w_in.dtype)
    w_all = jnp.concatenate([w_in, pad], axis=1).astype(BF16)
    one = jnp.ones((LANES,), F32)
    gcol = jnp.concatenate([
        jnp.tile(g_q_dil, NH_DIL), jnp.tile(g_k_dil, NH_DIL), jnp.ones((512,), F32), jnp.tile(g_q_nsa, NH_NSA),
        one, one, jnp.tile(g_k_slc, NKV_NSA), one, jnp.tile(g_k_win, NKV_NSA), one, one]).reshape(1, IN_COLS_PADDED)
    return w_all, gcol


def _rope_tables(S):
    inv_freq = ROPE_THETA ** (-jnp.arange(ROT_HALF, dtype=F32) / ROT_HALF)
    ang = jnp.arange(S).astype(F32)[:, None] * inv_freq[None, :]
    cos, sin = jnp.cos(ang), jnp.sin(ang)
    zeros = jnp.zeros((S, HD - ROT_DIM), F32)
    z8 = jnp.zeros((S, ROT_HALF), F32)
    cos_h = jnp.concatenate([cos, cos, jnp.ones((S, HD - ROT_DIM), F32)], axis=1)
    sa_h = jnp.concatenate([-sin, z8, zeros], axis=1)
    sb_h = jnp.concatenate([z8, sin, zeros], axis=1)
    two = lambda t: jnp.concatenate([t, t], axis=1)
    return two(cos_h), two(sa_h), two(sb_h)


def _prep_compress(pe, w1, w2):
    eye = jnp.eye(NKV_NSA, dtype=F32)
    w1r = w1.reshape(CMP_LEN, HD, CMP_HIDDEN)

    def half(w1h, peh):
        w = jnp.einsum("ldh,gk->lgdkh", w1h, eye).reshape(CMP_STRIDE * LANES, NKV_NSA * CMP_HIDDEN)
        p = jnp.broadcast_to(peh[:, None, :], (CMP_STRIDE, NKV_NSA, HD)).reshape(1, CMP_STRIDE * LANES)
        return p, w.astype(BF16)

    pa, wa = half(w1r[:CMP_STRIDE], pe[:CMP_STRIDE])
    pb, wb = half(w1r[CMP_STRIDE:], pe[CMP_STRIDE:])
    w2bd = jnp.einsum("hd,gk->ghkd", w2, eye).reshape(NKV_NSA * CMP_HIDDEN, LANES).astype(BF16)
    return pa, pb, wa, wb, w2bd


def _overlap_tables(S):
    ncb = S // CMP_STRIDE
    n_sel = S // SEL_BLOCK
    cs = np.arange(ncb) * CMP_STRIDE
    ss = np.arange(n_sel) * SEL_BLOCK
    ov = np.clip(np.minimum(cs[:, None] + CMP_LEN, ss[None, :] + SEL_BLOCK) - np.maximum(cs[:, None], ss[None, :]), 0, None)
    ov = ov.astype(np.float32) / CMP_STRIDE
    ovl = np.zeros((ncb, LANES), np.float32)
    ovr = np.zeros((ncb, LANES), np.float32)
    ovl[:, :n_sel] = ov
    ovr[:, HD:HD + n_sel] = ov
    return jnp.asarray(ovl, BF16), jnp.asarray(ovr, BF16)


def _block_ones():
    r = np.arange(LANES)
    return jnp.asarray((r[:, None] // HD == r[None, :] // HD).astype(np.float32), BF16)


def _layer(x, c, w_ada, b_ada, g_norm_mix, g_norm_ffn, w_in, g_q_dil, g_k_dil, g_q_nsa, g_k_cmp, g_k_slc, g_k_win,
           cmp_pe_k, cmp_w1_k, cmp_w2_k, cmp_pe_v, cmp_w1_v, cmp_w2_v, w_out, w_router, router_bias,
           w_gate, w_up, w_down, ws_gate, ws_up, ws_down):
    B, S, D = x.shape
    ones_bd = _block_ones()

    c_pad = jnp.zeros((8, D), F32).at[:B].set(c)
    mod = _ada_mod(c_pad, w_ada, b_ada.reshape(1, -1))[:B]
    sh1, sc1, gt1, sh2, sc2, gt2 = [m.reshape(B, 1, D) for m in jnp.split(mod, 6, axis=-1)]

    w_all, gcol = _prep_in_weights(w_in, g_q_dil, g_k_dil, g_q_nsa, g_k_slc, g_k_win)
    cos_t, sa_t, sb_t = _rope_tables(S)
    qa, ka, va, qn, qr, kc, vc, ksl, vsl, kw, vw, gate = _in_projection(
        x, g_norm_mix.reshape(1, D), sc1, sh1, w_all, gcol, cos_t, sa_t, sb_t, ones_bd, tm=512)

    o_a = _dilated_attention(qa, ka, va)

    kcmp, vcmp = _compress(kc, vc, _prep_compress(cmp_pe_k, cmp_w1_k, cmp_w2_k),
                           _prep_compress(cmp_pe_v, cmp_w1_v, cmp_w2_v), jnp.tile(g_k_cmp, NKV_NSA).reshape(1, LANES),
                           ones_bd)
    ovl, ovr = _overlap_tables(S)
    o_cmp, sel = _cmp_select(qn, kcmp, vcmp, ovl, ovr, gate, tq=128)
    o_cs = _nsa_branch(qr, ksl, vsl, sel, o_cmp, gate, tq=256, tk=512, mode="selected", out_dtype=F32)
    o_b = _nsa_branch(qr, kw, vw, None, o_cs, gate, tq=256, tk=256, mode="window", out_dtype=BF16)

    w_out_b = w_out.astype(BF16)
    x1, h2, h2_tiles = _out_projection(o_a, o_b, w_out_b[:NH_DIL * HD], w_out_b[NH_DIL * HD:], x, gt1,
                                       g_norm_ffn.reshape(1, D), sc2, sh2, tm=256)

    tm_r = 256
    eidx, gates, counts = _router(h2, w_router.T.astype(BF16),
                                  jnp.broadcast_to(router_bias.reshape(N_EXPERTS, 1), (N_EXPERTS, tm_r)), tm=tm_r)
    tables = _dispatch_tables(eidx, gates, counts[:, :, 0].astype(I32), B, S)
    routed = h2_tiles
    for b in range(B):
        routed = _routed_experts(routed, b, *[tbl[b] for tbl in tables], w_gate, w_up, w_down)
    return _shared_and_residual(h2, ws_gate.astype(BF16), ws_up.astype(BF16), ws_down.astype(BF16), routed, x1, gt2,
                                tm=256)


def kernel(x, c, w_ada, b_ada, g_norm_mix, g_norm_ffn, w_in, g_q_dil, g_k_dil, g_q_nsa, g_k_cmp, g_k_slc, g_k_win, cmp_pe_k, cmp_w1_k, cmp_w2_k, cmp_pe_v, cmp_w1_v, cmp_w2_v, w_out, w_router, router_bias, w_gate, w_up, w_down, ws_gate, ws_up, ws_down):
    params = (w_ada, b_ada, g_norm_mix, g_norm_ffn, w_in, g_q_dil, g_k_dil, g_q_nsa, g_k_cmp, g_k_slc, g_k_win,
              cmp_pe_k, cmp_w1_k, cmp_w2_k, cmp_pe_v, cmp_w1_v, cmp_w2_v, w_out, w_router, router_bias,
              w_gate, w_up, w_down, ws_gate, ws_up, ws_down)
    for layer in range(w_ada.shape[0]):
        x = _layer(x, c, *[a[layer] for a in params])
    return x
```

```python
import functools
import math

import numpy as np
import jax
import jax.numpy as jnp
from jax import lax
from jax.experimental import pallas as pl
from jax.experimental.pallas import tpu as pltpu

F32 = jnp.float32
BF16 = jnp.bfloat16
I32 = jnp.int32

HD = 64
LANES = 128
NH_DIL = 8
NH_NSA = 8
NKV_NSA = 2
NSA_REP = NH_NSA // NKV_NSA
DIL_PAIRS = ((128, 1), (512, 4), (2048, 16))
DIL_TILE = 128
DIL_GROUP = 8
ROPE_THETA = 500000.0
ROT_DIM = HD // 4
ROT_HALF = ROT_DIM // 2
CMP_LEN = 32
CMP_STRIDE = 16
CMP_HIDDEN = 128
SEL_BLOCK = 64
SEL_SHIFT = 6
SEL_TOP = 16
NSA_WINDOW = 512
N_EXPERTS = 256
TOP_K = 8
TOP_K_SHIFT = 3
N_GROUPS = 8
GROUP_SIZE = N_EXPERTS // N_GROUPS
TOPK_GROUPS = 4
EXPERT_FF = 256
SHARED_FF = 256
ROUTED_SCALE = 2.5
EPS = 1e-6
QK_SCALE = 1.0 / math.sqrt(HD)

NEG = -1e30
BIG = 3e38

C_QA, C_KA, C_VA, C_QB = 0, 512, 1024, 1536
C_KC, C_VC, C_KSL, C_VSL, C_KW, C_VW, C_GB = 2048, 2176, 2304, 2432, 2560, 2688, 2816
IN_COLS_PADDED = 2944
TRANSPOSED_OUTS = (8, 10)

MOE_BLOCK = 128
NULL_ROWS = 8
WEIGHT_RING = 5
SCATTER_GROUP = 16
VMEM_LIMIT = 48 * 1024 * 1024


def _cparams(sem, vmem=VMEM_LIMIT):
    return pltpu.CompilerParams(dimension_semantics=sem, vmem_limit_bytes=vmem)


def _sigmoid(v):
    return 1.0 / (1.0 + jnp.exp(-v))


def _silu(v):
    return v * _sigmoid(v)


def _dot(a, b):
    return jnp.dot(a, b, preferred_element_type=F32)


def _dot_nt(a, b):
    return lax.dot_general(a, b, (((1,), (1,)), ((), ())), preferred_element_type=F32)


def _lane_lo(shape):
    return lax.broadcasted_iota(I32, shape, len(shape) - 1) < HD


def _head_sums(v, ones_bd):
    hi = v.astype(BF16)
    lo = (v - hi.astype(F32)).astype(BF16)
    return _dot(hi, ones_bd) + _dot(lo, ones_bd)


def _head_rmsnorm(y, ones_bd, gain):
    ms = _head_sums(y * y, ones_bd) * (1.0 / HD)
    return y * lax.rsqrt(ms + EPS) * gain


def _rope(y, cos, sin_a, sin_b):
    return y * cos + pltpu.roll(y, LANES - ROT_HALF, 1) * sin_a + pltpu.roll(y, ROT_HALF, 1) * sin_b


def _ada_kernel(c_ref, w_ref, b_ref, o_ref):
    a = _silu(c_ref[...]).astype(BF16)
    o_ref[...] = _dot(a, w_ref[...].astype(BF16)) + b_ref[...]


def _ada_mod(c_pad, w_ada, b_ada):
    rows, d = c_pad.shape
    n = w_ada.shape[1]
    tn = 1536 if n % 1536 == 0 else n
    return pl.pallas_call(
        _ada_kernel,
        grid=(n // tn,),
        in_specs=[pl.BlockSpec((rows, d), lambda j: (0, 0)),
                  pl.BlockSpec((d, tn), lambda j: (0, j)),
                  pl.BlockSpec((1, tn), lambda j: (0, j))],
        out_specs=pl.BlockSpec((rows, tn), lambda j: (0, j)),
        out_shape=jax.ShapeDtypeStruct((rows, n), F32),
        compiler_params=_cparams(("arbitrary",)),
        name="ada_mod",
    )(c_pad, w_ada, b_ada)


def _inproj_kernel(x_ref, g_ref, sc_ref, sh_ref, w_ref, gcol_ref, cos_ref, sa_ref, sb_ref, ones_ref,
                   qa_ref, ka_ref, va_ref, qn_ref, qr_ref, kc_ref, vc_ref, ksl_ref, vsl_ref, kw_ref, vw_ref,
                   gate_ref):
    x = x_ref[0]
    ms = jnp.mean(x * x, axis=-1, keepdims=True)
    h = (x * lax.rsqrt(ms + EPS)) * g_ref[...]
    h = h * (1.0 + sc_ref[0]) + sh_ref[0]
    hb = h.astype(BF16)
    ones_bd = ones_ref[...]
    cos, sa, sb = cos_ref[...], sa_ref[...], sb_ref[...]
    lo = _lane_lo(cos.shape)

    def proj(c0, width):
        return _dot(hb, w_ref[:, c0:c0 + width])

    def normed(tile, c0):
        return _head_rmsnorm(tile, ones_bd, gcol_ref[:, c0:c0 + LANES])

    acc = proj(C_QA, 512)
    for p in range(4):
        y = _rope(normed(acc[:, p * LANES:(p + 1) * LANES], C_QA + p * LANES), cos, sa, sb) * QK_SCALE
        qa_ref[0, :, p * LANES:(p + 1) * LANES] = y
    acc = proj(C_KA, 512)
    for p in range(4):
        y = _rope(normed(acc[:, p * LANES:(p + 1) * LANES], C_KA + p * LANES), cos, sa, sb)
        ka_ref[0, :, p * LANES:(p + 1) * LANES] = y
    va_ref[0] = proj(C_VA, 512)

    acc = proj(C_QB, 512)
    for p in range(4):
        yn = normed(acc[:, p * LANES:(p + 1) * LANES], C_QB + p * LANES)
        yr = _rope(yn, cos, sa, sb)
        for y, ref in ((yn * QK_SCALE, qn_ref), (yr * QK_SCALE, qr_ref)):
            ysw = pltpu.roll(y, HD, 1)
            for half in range(2):
                head = 2 * p + half
                grp = head // NSA_REP
                src = y if grp == half else ysw
                keep = lo if grp == 0 else jnp.logical_not(lo)
                ref[0, :, head * LANES:(head + 1) * LANES] = jnp.where(keep, src, 0.0).astype(BF16)

    acc = proj(C_KC, 512)
    kc_ref[0] = acc[:, 0:LANES]
    vc_ref[0] = acc[:, LANES:2 * LANES]
    ksl_ref[0] = _rope(normed(acc[:, 2 * LANES:3 * LANES], C_KSL), cos, sa, sb).astype(BF16)
    vsl_ref[0] = _transposed(acc[:, 3 * LANES:4 * LANES]).astype(BF16)
    acc = proj(C_KW, 384)
    kw_ref[0] = _rope(normed(acc[:, 0:LANES], C_KW), cos, sa, sb).astype(BF16)
    vw_ref[0] = _transposed(acc[:, LANES:2 * LANES]).astype(BF16)
    gate_ref[0] = _sigmoid(acc[:, 2 * LANES:3 * LANES])


def _transposed(tile):
    return jnp.concatenate([tile[r0:r0 + LANES].T for r0 in range(0, tile.shape[0], LANES)], axis=1)


def _in_projection(x, g_mix, sc1, sh1, w_all, gcol, cos_t, sa_t, sb_t, ones_bd, tm):
    B, S, D = x.shape
    nc = w_all.shape[1]
    row = lambda b, i: (b, i, 0)
    bvec = pl.BlockSpec((1, 1, D), lambda b, i: (b, 0, 0))
    tab = pl.BlockSpec((tm, LANES), lambda b, i: (i, 0))
    const2 = lambda shape: pl.BlockSpec(shape, lambda b, i: (0, 0))
    widths_dt = [(512, F32), (512, F32), (512, F32), (1024, BF16), (1024, BF16), (LANES, F32), (LANES, F32),
                 (LANES, BF16), (LANES, BF16), (LANES, BF16), (LANES, BF16), (LANES, F32)]
    return pl.pallas_call(
        _inproj_kernel,
        grid=(B, S // tm),
        in_specs=[pl.BlockSpec((1, tm, D), row), const2((1, D)), bvec, bvec, const2((D, nc)), const2((1, nc)),
                  tab, tab, tab, const2((LANES, LANES))],
        out_specs=[pl.BlockSpec((1, tm, w), row) if n not in TRANSPOSED_OUTS else pl.BlockSpec((1, w, tm), lambda b, i: (b, 0, i))
                   for n, (w, _) in enumerate(widths_dt)],
        out_shape=[jax.ShapeDtypeStruct((B, S, w) if n not in TRANSPOSED_OUTS else (B, w, S), dt)
                   for n, (w, dt) in enumerate(widths_dt)],
        compiler_params=_cparams(("parallel", "parallel")),
        name="in_projection",
    )(x, g_mix, sc1, sh1, w_all, gcol, cos_t, sa_t, sb_t, ones_bd)


def _dot_tn(a, b):
    return lax.dot_general(a, b, (((0,), (0,)), ((), ())), preferred_element_type=F32)


def _dilated_kernel(q_ref, k_ref, v_ref, o_ref, oacc, lacc, *, sb_rows, pairs):
    sb = pl.program_id(2)
    base = sb * sb_rows
    t = DIL_TILE
    kk = lax.broadcasted_iota(I32, (t, t), 0)
    qq = lax.broadcasted_iota(I32, (t, t), 1)
    bias_cur = jnp.where(qq >= kk, 0.0, NEG)
    bias_prev = jnp.where(qq <= kk, 0.0, NEG)
    lane_lo = _lane_lo((t, LANES))
    row_lo = lax.broadcasted_iota(I32, (LANES, t), 0) < HD

    def blocks(d, starts, mode):
        def strided(ref, start):
            return ref[0, pl.ds(start, t, stride=d), :] if d > 1 else ref[0, pl.ds(start, t), :]

        work = []
        for qs, first in starts:
            qg = base + qs
            rows_q = pl.ds(qs, t, stride=d) if d > 1 else pl.ds(qs, t)
            prev_start = jnp.where(first, qg, qg - t * d)
            qt = q_ref[0, rows_q, :].astype(BF16)
            zero = jnp.zeros_like(qt)
            q2 = jnp.concatenate([jnp.where(lane_lo, qt, zero), jnp.where(lane_lo, zero, qt)], axis=0)
            k2 = jnp.concatenate([strided(k_ref, prev_start), strided(k_ref, qg)], axis=0).astype(BF16)
            v2 = jnp.concatenate([strided(v_ref, prev_start), strided(v_ref, qg)], axis=0).astype(BF16)
            bias = jnp.concatenate([jnp.where(first, NEG, bias_prev), bias_cur], axis=0)
            s = _dot_nt(k2, q2) + jnp.concatenate([bias, bias], axis=1)
            work.append([rows_q, v2, s])
        for w in work:
            s = w[2]
            m = s.max(axis=0, keepdims=True)
            p = jnp.exp(s - m)
            l = p.sum(axis=0, keepdims=True)
            w[2:] = [p.astype(BF16), l, m + jnp.log(l)]
        for w in work:
            w[1] = _dot_tn(w[1], w[2]) / w[3]
        for w in work:
            ov, lse2 = w[1], w[4]
            w[1:] = [jnp.where(row_lo, ov[:, :t], ov[:, t:]).T, jnp.where(row_lo, lse2[:, :t], lse2[:, t:]).T]
        for rows_q, o, lse in work:
            if mode == "init":
                oacc[rows_q, :] = o
                lacc[rows_q, :] = lse
                continue
            lp = lacc[rows_q, :]
            mx = jnp.maximum(lp, lse)
            wp, wn = jnp.exp(lp - mx), jnp.exp(lse - mx)
            den = wp + wn
            oacc[rows_q, :] = (oacc[rows_q, :] * wp + o * wn) / den
            if mode == "mid":
                lacc[rows_q, :] = mx + jnp.log(den)

    modes = ("init",) + ("mid",) * (len(pairs) - 2) + ("last",)
    for (window, d), mode in zip(pairs, modes):
        assert window // d == t and d & (d - 1) == 0
        n_blocks = sb_rows // t

        def body(i, c, d=d, mode=mode):
            starts = []
            for u in range(DIL_GROUP):
                n = i * DIL_GROUP + u
                r, j = n & (d - 1), n >> (d.bit_length() - 1)
                starts.append((j * (t * d) + r, jnp.logical_and(sb == 0, j == 0)))
            blocks(d, starts, mode)
            return c

        lax.fori_loop(0, n_blocks // DIL_GROUP, body, 0)
    o_ref[0] = oacc[...].astype(o_ref.dtype)


def _dilated_attention(qa, ka, va):
    B, S, w = qa.shape
    sb_rows = DIL_TILE * max(d for _, d in DIL_PAIRS)
    assert S % sb_rows == 0
    kern = functools.partial(_dilated_kernel, sb_rows=sb_rows, pairs=DIL_PAIRS)
    whole = pl.BlockSpec((1, S, LANES), lambda b, p, s: (b, 0, p))
    blk = pl.BlockSpec((1, sb_rows, LANES), lambda b, p, s: (b, s, p))
    acc = pltpu.VMEM((sb_rows, LANES), F32)
    return pl.pallas_call(
        kern, grid=(B, w // LANES, S // sb_rows),
        in_specs=[blk, whole, whole], out_specs=blk,
        out_shape=jax.ShapeDtypeStruct((B, S, w), BF16),
        scratch_shapes=[acc, acc],
        compiler_params=_cparams(("parallel", "parallel", "arbitrary")),
        name="dilated_attention",
    )(qa, ka, va)


def _compress_kernel(kc_ref, vc_ref, pak_ref, pbk_ref, wak_ref, wbk_ref, w2k_ref, gk_ref, ones_ref,
                     pav_ref, pbv_ref, wav_ref, wbv_ref, w2v_ref, ko_ref, vo_ref):
    ncb = kc_ref.shape[1]

    def branch(t_ref, pa, pb, wa, wb, w2):
        t = t_ref[0]
        ua = _dot((t + pa[...]).astype(BF16), wa[...])
        ub = _dot((t + pb[...]).astype(BF16), wb[...])
        pre = ua + pltpu.roll(ub, ncb - 1, 0)
        return _dot(jax.nn.gelu(pre).astype(BF16), w2[...])

    live = lax.broadcasted_iota(I32, (ncb, LANES), 0) < ncb - 1
    kcmp = _head_rmsnorm(branch(kc_ref, pak_ref, pbk_ref, wak_ref, wbk_ref, w2k_ref), ones_ref[...], gk_ref[...])
    ko_ref[0] = jnp.where(live, kcmp, 0.0).astype(BF16)
    vo_ref[0] = jnp.where(live, branch(vc_ref, pav_ref, pbv_ref, wav_ref, wbv_ref, w2v_ref), 0.0).astype(BF16)


def _compress(kc, vc, wk, wv, gk2, ones_bd):
    B, S, _ = kc.shape
    ncb = S // CMP_STRIDE
    cw = CMP_STRIDE * LANES
    view = lambda a: a.reshape(B, ncb, cw)
    blk = pl.BlockSpec((1, ncb, cw), lambda b: (b, 0, 0))
    full = lambda a: pl.BlockSpec(a.shape, lambda b: (0,) * a.ndim)
    consts = [*wk, gk2, ones_bd, *wv]
    o_spec = pl.BlockSpec((1, ncb, LANES), lambda b: (b, 0, 0))
    return pl.pallas_call(
        _compress_kernel, grid=(B,),
        in_specs=[blk, blk] + [full(a) for a in consts],
        out_specs=[o_spec, o_spec],
        out_shape=[jax.ShapeDtypeStruct((B, ncb, LANES), BF16)] * 2,
        compiler_params=_cparams(("parallel",)),
        name="nsa_compress",
    )(view(kc), view(vc), *consts)


def _pair_tile(lo, t0, half0, t1, half1):
    a = t0 if half0 == 0 else pltpu.roll(t0, HD, 1)
    b = t1 if half1 == 1 else pltpu.roll(t1, HD, 1)
    return jnp.where(lo, a, b)


def _cmpsel_kernel(qn_ref, kc_ref, vc_ref, ovl_ref, ovr_ref, gate_ref, o_ref, sel_ref, *, tq, n_sel, top):
    ncb = kc_ref.shape[1]
    i = pl.program_id(1)
    pos0 = i * tq
    qpos = pos0 + lax.broadcasted_iota(I32, (tq, ncb), 0)
    cend = lax.broadcasted_iota(I32, (tq, ncb), 1) * CMP_STRIDE + (CMP_LEN - 1)
    valid = cend <= qpos
    lo = _lane_lo((tq, LANES))
    kc = kc_ref[0]
    vc = vc_ref[0]

    psum = [None] * NKV_NSA
    scores = [_dot_nt(qn_ref[0, :, h * LANES:(h + 1) * LANES], kc) for h in range(NH_NSA)]
    probs = []
    for h, s in enumerate(scores):
        g = h // NSA_REP
        m = jnp.where(valid, s, NEG).max(axis=1, keepdims=True)
        m = jnp.where(m > 0.5 * NEG, m, 0.0)
        p = jnp.where(valid, jnp.exp(s - m), 0.0)
        l = p.sum(axis=1, keepdims=True)
        p = p / jnp.where(l > 0.0, l, 1.0)
        probs.append(p.astype(BF16))
        psum[g] = p if psum[g] is None else psum[g] + p
    outs = [_dot(p, vc) for p in probs]

    for p_ in range(NH_NSA // 2):
        h0, h1 = 2 * p_, 2 * p_ + 1
        o_pair = _pair_tile(lo, outs[h0], h0 // NSA_REP, outs[h1], h1 // NSA_REP)
        g = jnp.where(lo, gate_ref[0, :, 3 * h0:3 * h0 + 1], gate_ref[0, :, 3 * h1:3 * h1 + 1])
        o_ref[0, :, p_ * LANES:(p_ + 1) * LANES] = g * o_pair

    def hilo_dot(a, w):
        hi = a.astype(BF16)
        return _dot(hi, w) + _dot((a - hi.astype(F32)).astype(BF16), w)

    pslc = hilo_dot(psum[0], ovl_ref[...]) + hilo_dot(psum[1], ovr_ref[...])
    pslc_t = pslc.T
    jblk = lax.broadcasted_iota(I32, (HD, tq), 0)
    cur = (pos0 + lax.broadcasted_iota(I32, (HD, tq), 1)) >> SEL_SHIFT
    forced = (jblk == 0) | (jblk == cur) | (jblk == cur - 1)
    sel_t = []
    for g in range(NKV_NSA):
        sc = jnp.where(forced, BIG, jnp.where(jblk <= cur, pslc_t[g * HD:(g + 1) * HD, :], -BIG))
        rank = jnp.zeros((HD, tq), F32)
        for k in range(n_sel):
            rk = sc[k:k + 1, :]
            rank = rank + jnp.where(jblk > k, jnp.where(rk >= sc, 1.0, 0.0), jnp.where(rk > sc, 1.0, 0.0))
        sel_t.append(jnp.where(rank < top, 1.0, 0.0))
    sel_ref[0] = jnp.concatenate(sel_t, axis=0)


def _cmp_select(qn, kcmp, vcmp, ovl, ovr, gate, tq):
    B, S, qw = qn.shape
    ncb = kcmp.shape[1]
    n_sel = S // SEL_BLOCK
    assert n_sel <= HD and tq == LANES
    row = lambda b, i: (b, i, 0)
    cb = pl.BlockSpec((1, ncb, LANES), lambda b, i: (b, 0, 0))
    ovs = pl.BlockSpec((ncb, LANES), lambda b, i: (0, 0))
    kern = functools.partial(_cmpsel_kernel, tq=tq, n_sel=n_sel, top=min(SEL_TOP, n_sel))
    return pl.pallas_call(
        kern, grid=(B, S // tq),
        in_specs=[pl.BlockSpec((1, tq, qw), row), cb, cb, ovs, ovs, pl.BlockSpec((1, tq, LANES), row)],
        out_specs=[pl.BlockSpec((1, tq, NH_NSA * HD), row), pl.BlockSpec((1, LANES, tq), lambda b, i: (b, 0, i))],
        out_shape=[jax.ShapeDtypeStruct((B, S, NH_NSA * HD), F32), jax.ShapeDtypeStruct((B, LANES, S), F32)],
        compiler_params=_cparams(("parallel", "parallel")),
        name="nsa_cmp_select",
    )(qn, kcmp, vcmp, ovl, ovr, gate)


def _nsa_kernel(*refs, tq, tk, mode):
    qi_ref, tile_ref, edge_ref = refs[:3]
    if mode == "selected":
        q_ref, k_ref, vt_ref, selt_ref, prev_ref, gate_ref, o_ref, m_ref, l_ref, acc_ref = refs[3:]
    else:
        q_ref, k_ref, vt_ref, prev_ref, gate_ref, o_ref, m_ref, l_ref, acc_ref = refs[3:]
    step = pl.program_id(1)
    i = qi_ref[step]
    tile = tile_ref[step]
    edge = edge_ref[step]

    @pl.when((edge & 1) != 0)
    def _():
        m_ref[...] = jnp.full(m_ref.shape, NEG, F32)
        l_ref[...] = jnp.zeros(l_ref.shape, F32)
        acc_ref[...] = jnp.zeros(acc_ref.shape, F32)

    def accumulate_tile():
        kpos = tile * tk + lax.broadcasted_iota(I32, (tk, tq), 0)
        qpos = i * tq + lax.broadcasted_iota(I32, (tk, tq), 1)
        k = k_ref[0]
        vt = vt_ref[0]
        if mode == "window":
            dist = qpos - kpos
            bias_w = jnp.where(dist >= 0, jnp.where(dist < NSA_WINDOW, 0.0, NEG), NEG)
        groups = range(NKV_NSA)
        scores = []
        for g in groups:
            if mode == "selected":
                nblk = tk // SEL_BLOCK
                rows = selt_ref[0, pl.ds(pl.multiple_of(g * HD + tile * nblk, nblk), nblk), :]
                chosen = jnp.broadcast_to(rows[:, None, :], (nblk, SEL_BLOCK, tq)).reshape(tk, tq) > 0.5
                bias = jnp.where(kpos <= qpos, jnp.where(chosen, 0.0, NEG), NEG)
            else:
                bias = bias_w
            heads = range(g * NSA_REP, (g + 1) * NSA_REP)
            q4 = jnp.concatenate([q_ref[0, :, h * LANES:(h + 1) * LANES] for h in heads], axis=0)
            scores.append(_dot_nt(k, q4) + jnp.concatenate([bias] * NSA_REP, axis=1))
        probs = []
        for g in groups:
            m_prev = m_ref[g]
            m_new = jnp.maximum(m_prev, scores[g].max(axis=0, keepdims=True))
            alpha = jnp.exp(m_prev - m_new)
            p = jnp.exp(scores[g] - m_new[0:1])
            l_ref[g] = alpha * l_ref[g] + p.sum(axis=0, keepdims=True)
            m_ref[g] = m_new
            probs.append((alpha[0:1], p.astype(BF16)))
        for g in groups:
            alpha, p = probs[g]
            acc_ref[g] = alpha * acc_ref[g] + _dot(vt, p)

    accumulate_tile()

    @pl.when((edge & 2) != 0)
    def _():
        lo = _lane_lo((tq, LANES))
        gate_col = 1 if mode == "selected" else 2

        def head_t(h):
            g, r = divmod(h, NSA_REP)
            cols = slice(r * tq, (r + 1) * tq)
            return acc_ref[g, g * HD:(g + 1) * HD, cols] / l_ref[g, 0:1, cols]

        for p_ in range(NH_NSA // 2):
            h0, h1 = 2 * p_, 2 * p_ + 1
            o_pair = jnp.concatenate([head_t(h0), head_t(h1)], axis=0).T
            c0, c1 = 3 * h0 + gate_col, 3 * h1 + gate_col
            gt = jnp.where(lo, gate_ref[0, :, c0:c0 + 1], gate_ref[0, :, c1:c1 + 1])
            sl = slice(p_ * LANES, (p_ + 1) * LANES)
            o_ref[0, :, sl] = (prev_ref[0, :, sl] + gt * o_pair).astype(o_ref.dtype)


def _nsa_branch(qr, k, v_t, sel_t, prev, gate, *, tq, tk, mode, out_dtype):
    B, S, qw = qr.shape
    ow = NH_NSA * HD
    steps = []
    for i in range(S // tq):
        hi = (i * tq + tq - 1) // tk
        lo = 0 if mode == "selected" else max((i * tq - (NSA_WINDOW - 1)) // tk, 0)
        steps += [(i, t, (t == lo) + 2 * (t == hi)) for t in range(lo, hi + 1)]
    tables = [jnp.asarray(np.array(col, np.int32)) for col in zip(*steps)]
    row = lambda b, s, qi, tl, ed: (b, qi[s], 0)
    in_specs = [pl.BlockSpec((1, tq, qw), row),
                pl.BlockSpec((1, tk, LANES), lambda b, s, qi, tl, ed: (b, tl[s], 0)),
                pl.BlockSpec((1, LANES, tk), lambda b, s, qi, tl, ed: (b, 0, tl[s]))]
    args = [qr, k, v_t]
    if mode == "selected":
        in_specs.append(pl.BlockSpec((1, LANES, tq), lambda b, s, qi, tl, ed: (b, 0, qi[s])))
        args.append(sel_t)
    in_specs += [pl.BlockSpec((1, tq, ow), row), pl.BlockSpec((1, tq, LANES), row)]
    args += [prev, gate]
    stat = pltpu.VMEM((NKV_NSA, 8, NSA_REP * tq), F32)
    grid_spec = pltpu.PrefetchScalarGridSpec(
        num_scalar_prefetch=3, grid=(B, len(steps)), in_specs=in_specs,
        out_specs=pl.BlockSpec((1, tq, ow), row),
        scratch_shapes=[stat, stat, pltpu.VMEM((NKV_NSA, LANES, NSA_REP * tq), F32)])
    return pl.pallas_call(
        functools.partial(_nsa_kernel, tq=tq, tk=tk, mode=mode), grid_spec=grid_spec,
        out_shape=jax.ShapeDtypeStruct((B, S, ow), out_dtype),
        compiler_params=_cparams(("parallel", "arbitrary")),
        name=f"nsa_{mode}_attention",
    )(*tables, *args)


def _to_token_tiles(ref, value):
    m, d = value.shape
    rpt = d // LANES
    for j in range(rpt):
        ref[0, pl.ds(j, m, stride=rpt), :] = value[:, j * LANES:(j + 1) * LANES]


def _from_token_tiles(ref, m):
    rpt = ref.shape[1] // m
    return jnp.concatenate([ref[0, pl.ds(j, m, stride=rpt), :] for j in range(rpt)], axis=1)


def _outproj_kernel(oa_ref, ob_ref, wt_ref, wb_ref, x_ref, gt_ref, g_ref, sc_ref, sh_ref, x1_ref, h2_ref, h2t_ref):
    mix = _dot(oa_ref[0], wt_ref[...]) + _dot(ob_ref[0], wb_ref[...])
    x1 = x_ref[0] + gt_ref[0] * mix
    x1_ref[0] = x1
    ms = jnp.mean(x1 * x1, axis=-1, keepdims=True)
    h = (x1 * lax.rsqrt(ms + EPS)) * g_ref[...]
    h2 = h * (1.0 + sc_ref[0]) + sh_ref[0]
    h2_ref[0] = h2.astype(h2_ref.dtype)
    _to_token_tiles(h2t_ref, h2)


def _out_projection(oa, ob, w_top, w_bot, x, gt1, g_ffn, sc2, sh2, tm):
    B, S, D = x.shape
    row = lambda b, i: (b, i, 0)
    bvec = pl.BlockSpec((1, 1, D), lambda b, i: (b, 0, 0))
    const2 = lambda a: pl.BlockSpec(a.shape, lambda b, i: (0, 0))
    xs = pl.BlockSpec((1, tm, D), row)
    return pl.pallas_call(
        _outproj_kernel, grid=(B, S // tm),
        in_specs=[pl.BlockSpec((1, tm, oa.shape[2]), row), pl.BlockSpec((1, tm, ob.shape[2]), row),
                  const2(w_top), const2(w_bot), xs, bvec, const2(g_ffn), bvec, bvec],
        out_specs=[xs, xs, pl.BlockSpec((1, tm * (D // LANES), LANES), row)],
        out_shape=[jax.ShapeDtypeStruct((B, S, D), F32), jax.ShapeDtypeStruct((B, S, D), BF16),
                   jax.ShapeDtypeStruct((B, S * (D // LANES), LANES), F32)],
        compiler_params=_cparams(("parallel", "parallel")),
        name="out_projection",
    )(oa, ob, w_top, w_bot, x, gt1, g_ffn, sc2, sh2)


def _router_kernel(h_ref, wr_ref, bias_ref, eidx_ref, gate_ref, cnt_ref):
    tm = h_ref.shape[1]
    logits = _dot_nt(wr_ref[...], h_ref[0])
    aff = _sigmoid(logits)
    biased = aff + bias_ref[...]
    b3 = biased.reshape(N_GROUPS, GROUP_SIZE, tm)
    jj = lax.broadcasted_iota(I32, (N_GROUPS, GROUP_SIZE, tm), 1).astype(F32)
    m1 = b3.max(axis=1, keepdims=True)
    i1 = jnp.where(b3 == m1, jj, 1e9).min(axis=1, keepdims=True)
    m2 = jnp.where(jj == i1, -jnp.inf, b3).max(axis=1, keepdims=True)
    gs = (m1 + m2).reshape(N_GROUPS, tm)
    gi = lax.broadcasted_iota(I32, (N_GROUPS, tm), 0)
    rank = jnp.zeros((N_GROUPS, tm), F32)
    for k in range(N_GROUPS):
        rk = gs[k:k + 1, :]
        rank = rank + jnp.where(gi > k, jnp.where(rk >= gs, 1.0, 0.0), jnp.where(rk > gs, 1.0, 0.0))
    gsel = (rank < TOPK_GROUPS).reshape(N_GROUPS, 1, tm)
    masked = jnp.where(gsel, b3, -jnp.inf).reshape(N_EXPERTS, tm)
    ee = lax.broadcasted_iota(I32, (N_EXPERTS, tm), 0).astype(F32)
    idx_rows, sel_rows = [], []
    chosen = jnp.zeros((N_EXPERTS, tm), F32)
    for _ in range(TOP_K):
        m = masked.max(axis=0, keepdims=True)
        idx = jnp.where(masked == m, ee, 1e9).min(axis=0, keepdims=True)
        hit = ee == idx
        idx_rows.append(idx)
        sel_rows.append(jnp.where(hit, aff, 0.0).sum(axis=0, keepdims=True))
        masked = jnp.where(hit, -jnp.inf, masked)
        chosen = chosen + jnp.where(hit, 1.0, 0.0)
    sel = jnp.concatenate(sel_rows, axis=0)
    eidx_ref[...] = jnp.concatenate(idx_rows, axis=0).astype(I32)
    gate_ref[...] = sel / sel.sum(axis=0, keepdims=True) * ROUTED_SCALE

    @pl.when(pl.program_id(1) == 0)
    def _():
        cnt_ref[...] = jnp.zeros(cnt_ref.shape, F32)

    cnt_ref[0] = cnt_ref[0] + chosen.sum(axis=1, keepdims=True)


def _router(h2, wr_t, bias_b, tm):
    B, S, D = h2.shape
    nt = S // tm
    o_spec = pl.BlockSpec((TOP_K, tm), lambda b, i: (0, b * nt + i))
    return pl.pallas_call(
        _router_kernel, grid=(B, nt),
        in_specs=[pl.BlockSpec((1, tm, D), lambda b, i: (b, i, 0)), pl.BlockSpec(wr_t.shape, lambda b, i: (0, 0)),
                  pl.BlockSpec(bias_b.shape, lambda b, i: (0, 0))],
        out_specs=[o_spec, o_spec, pl.BlockSpec((1, N_EXPERTS, LANES), lambda b, i: (b, 0, 0))],
        out_shape=[jax.ShapeDtypeStruct((TOP_K, B * S), I32), jax.ShapeDtypeStruct((TOP_K, B * S), F32),
                   jax.ShapeDtypeStruct((B, N_EXPERTS, LANES), F32)],
        compiler_params=_cparams(("parallel", "arbitrary")),
        name="moe_router",
    )(h2, wr_t, bias_b)


def _experts_kernel(*refs, batch, emit_bf16):
    (ord_ref, uexp_ref, nused_ref, row0_ref, nvalid_ref, nblk_ref, skey_ref, gflat_ref,
     h_all, wg_hbm, wu_hbm, wd_hbm, o_all) = refs[:13]
    n_out = 3 if emit_bf16 else 0
    bf16_out = refs[13:13 + n_out]
    h_vmem, acc_ref, xg0, xg1, y0, y1, wg_ring, wu_ring, wd_ring, sem, wsem = refs[13 + n_out:24 + n_out]
    staging = refs[24 + n_out:]
    h_hbm, o_hbm = h_all.at[batch], o_all.at[batch]
    b = pl.program_id(0)
    last_step = pl.num_programs(0) - 1
    nblk = nblk_ref[0]
    rpt = wg_hbm.shape[1] // LANES
    tc = h_vmem.shape[0] // rpt - 1
    n_assign = tc * TOP_K
    depth = wg_ring.shape[0]

    def weight_copies(j):
        e, slot = uexp_ref[j], j % depth
        return [pltpu.make_async_copy(hbm.at[e], ring.at[slot], wsem.at[n, slot])
                for n, (hbm, ring) in enumerate(((wg_hbm, wg_ring), (wu_hbm, wu_ring), (wd_hbm, wd_ring)))]

    def start_weights(j):
        @pl.when(j < nused_ref[0])
        def _():
            for cp in weight_copies(j):
                cp.start()

    def bf16_copies(j):
        e, st = uexp_ref[j], j % 2
        return [pltpu.make_async_copy(staging[n].at[st], bf16_out[n].at[e], staging[3].at[n, st]) for n in range(3)]

    def block_rows(blk):
        bc = jnp.clip(blk, 0, nblk - 1)
        return row0_ref[bc], nvalid_ref[bc]

    def assignment(rows, r, checked=True):
        row0, nv = rows
        a = skey_ref[row0 + r] & (n_assign - 1)
        return jnp.where(r < nv, a, n_assign) if checked else a

    def tile_of(a):
        off = (a & ~(TOP_K - 1)) if rpt == TOP_K else (a >> TOP_K_SHIFT) * rpt
        return pl.ds(pl.multiple_of(off, rpt), rpt)

    def gather(blk, xg, r0=0, r1=MOE_BLOCK):
        rows = block_rows(blk)
        for r in range(r0, r1):
            xg[r * rpt:(r + 1) * rpt, :] = h_vmem[tile_of(assignment(rows, r, checked=False)), :]

    def scatter(blk, y, r0=0, r1=MOE_BLOCK):
        blk_rows = block_rows(blk)
        for g0 in range(r0, r1, SCATTER_GROUP):
            updated = []
            for r in range(g0, g0 + SCATTER_GROUP):
                a = assignment(blk_rows, r)
                rows = tile_of(a)
                updated.append((rows, acc_ref[rows, :] + gflat_ref[a] * y[r * rpt:(r + 1) * rpt, :]))
            for rows, v in updated:
                acc_ref[rows, :] = v

    @pl.when(b == 0)
    def _():
        cp = pltpu.make_async_copy(h_hbm, h_vmem.at[pl.ds(0, tc * rpt)], sem)
        cp.start()
        for j in range(depth - 1):
            start_weights(j)
        h_vmem[pl.ds(tc * rpt, rpt), :] = jnp.zeros((rpt, LANES), F32)
        acc_ref[...] = jnp.zeros(acc_ref.shape, F32)
        y0[...] = jnp.zeros(y0.shape, F32)
        y1[...] = jnp.zeros(y1.shape, F32)
        cp.wait()
        gather(0, xg0)

    cur = ord_ref[jnp.minimum(b, nblk - 1)]

    @pl.when(jnp.logical_and(b < nblk, jnp.logical_or(b == 0, ord_ref[jnp.maximum(b, 1) - 1] != cur)))
    def _():
        for cp in weight_copies(cur):
            cp.wait()
        start_weights(cur + depth - 1)
        if emit_bf16:
            @pl.when(cur >= 2)
            def _():
                for cp in bf16_copies(cur - 2):
                    cp.wait()

            for n, ring in enumerate((wg_ring, wu_ring, wd_ring)):
                staging[n][cur % 2] = ring[cur % depth].astype(BF16)
            for cp in bf16_copies(cur):
                cp.start()

    def step(xg_cur, xg_nxt, y_cur, y_prv):
        slot = cur % depth
        half = MOE_BLOCK // 2
        xb = jnp.concatenate([xg_cur[pl.ds(j, MOE_BLOCK, stride=rpt), :] for j in range(rpt)], axis=1).astype(BF16)
        gt = _dot(xb, wg_ring[slot].astype(BF16))
        up = _dot(xb, wu_ring[slot].astype(BF16))
        gather(b + 1, xg_nxt, 0, half)
        scatter(b - 1, y_prv, 0, half)
        y = _dot((_silu(gt) * up).astype(BF16), wd_ring[slot].astype(BF16))
        gather(b + 1, xg_nxt, half, MOE_BLOCK)
        scatter(b - 1, y_prv, half, MOE_BLOCK)
        for j in range(rpt):
            y_cur[pl.ds(j, MOE_BLOCK, stride=rpt), :] = y[:, j * LANES:(j + 1) * LANES]

    @pl.when(jnp.logical_and(b <= nblk, (b & 1) == 0))
    def _():
        step(xg0, xg1, y0, y1)

    @pl.when(jnp.logical_and(b <= nblk, (b & 1) == 1))
    def _():
        step(xg1, xg0, y1, y0)

    @pl.when(b == last_step)
    def _():
        cp = pltpu.make_async_copy(acc_ref.at[pl.ds(0, tc * rpt)], o_hbm, sem)
        cp.start()
        if emit_bf16:
            for back in (2, 1):
                @pl.when(nused_ref[0] >= back)
                def _(back=back):
                    for c in bf16_copies(nused_ref[0] - back):
                        c.wait()
        cp.wait()


def _routed_experts(h_tiles, batch, blk_ord, uexp, nused, row0, nvalid, nblk, skey, gflat, w_gate, w_up, w_down,
                    emit_bf16=False):
    D, ff = w_gate.shape[1], w_gate.shape[2]
    rpt = D // LANES
    tc = h_tiles.shape[1] // rpt
    nb = blk_ord.shape[0]
    n_prefetch = 8
    wdt = w_gate.dtype
    depth = WEIGHT_RING - 1 if emit_bf16 else WEIGHT_RING
    hbm = pl.BlockSpec(memory_space=pl.ANY)
    rows = pltpu.VMEM((MOE_BLOCK * rpt, LANES), F32)
    resident = pltpu.VMEM(((tc + 1) * rpt, LANES), F32)
    scratch = [resident, resident, rows, rows, rows, rows,
               pltpu.VMEM((depth, D, ff), wdt), pltpu.VMEM((depth, D, ff), wdt), pltpu.VMEM((depth, ff, D), wdt),
               pltpu.SemaphoreType.DMA(()), pltpu.SemaphoreType.DMA((3, depth))]
    out_shape = [jax.ShapeDtypeStruct(h_tiles.shape, F32)]
    if emit_bf16:
        scratch += [pltpu.VMEM((2, D, ff), BF16), pltpu.VMEM((2, D, ff), BF16), pltpu.VMEM((2, ff, D), BF16),
                    pltpu.SemaphoreType.DMA((3, 2))]
        out_shape += [jax.ShapeDtypeStruct(w.shape, BF16) for w in (w_gate, w_up, w_down)]
    grid_spec = pltpu.PrefetchScalarGridSpec(
        num_scalar_prefetch=n_prefetch, grid=(nb + 1,),
        in_specs=[hbm, hbm, hbm, hbm],
        out_specs=[hbm] * len(out_shape),
        scratch_shapes=scratch,
    )
    return pl.pallas_call(
        functools.partial(_experts_kernel, batch=batch, emit_bf16=emit_bf16), grid_spec=grid_spec,
        out_shape=out_shape,
        input_output_aliases={n_prefetch: 0},
        compiler_params=_cparams(("arbitrary",), vmem=56 * 1024 * 1024),
        name="moe_routed_experts",
    )(blk_ord, uexp, nused, row0, nvalid, nblk, skey, gflat, h_tiles, w_gate, w_up, w_down)


def _dispatch_tables(eidx, gates, counts, B, S):
    n_assign = S * TOP_K
    assert n_assign & (n_assign - 1) == 0
    e = eidx.reshape(TOP_K, B, S)
    t = lax.broadcasted_iota(I32, e.shape, 2)
    k = lax.broadcasted_iota(I32, e.shape, 0)
    keys = (e * n_assign + t * TOP_K + k).transpose(1, 0, 2).reshape(B, n_assign)
    spare = jnp.zeros((MOE_BLOCK,), I32)
    skey = jnp.stack([jnp.concatenate([lax.sort(keys[b]), spare]) for b in range(B)], axis=0)
    seg = jnp.cumsum(counts, axis=1) - counts
    nblocks_e = (counts + MOE_BLOCK - 1) // MOE_BLOCK
    bend = jnp.cumsum(nblocks_e, axis=1)
    bstart = bend - nblocks_e
    nb = n_assign // MOE_BLOCK + N_EXPERTS
    bidx = jnp.arange(nb, dtype=I32)
    blk_e = jnp.minimum(jnp.sum((bend[:, None, :] <= bidx[None, :, None]).astype(I32), axis=-1), N_EXPERTS - 1)
    onehot = (blk_e[:, :, None] == jnp.arange(N_EXPERTS, dtype=I32)[None, None, :]).astype(I32)
    pick = lambda v: jnp.sum(onehot * v[:, None, :], axis=-1)
    off = (bidx[None, :] - pick(bstart)) * MOE_BLOCK
    row0 = pick(seg) + off
    nvalid = jnp.clip(pick(counts) - off, 0, MOE_BLOCK)
    used = (counts > 0).astype(I32)
    rank = jnp.cumsum(used, axis=1) - 1
    blk_ord = pick(rank)
    slots = jnp.arange(N_EXPERTS, dtype=I32)
    uexp = jnp.sum(jnp.where((rank[:, None, :] == slots[None, :, None]) & (used[:, None, :] > 0), slots[None, None, :], 0),
                   axis=-1)
    gflat = gates.reshape(TOP_K, B, S).transpose(1, 2, 0).reshape(B, n_assign)
    gflat = jnp.concatenate([gflat, jnp.zeros((B, TOP_K), F32)], axis=1)
    return blk_ord, uexp, jnp.sum(used, axis=1, keepdims=True), row0, nvalid, bend[:, -1:], skey, gflat


def _shared_kernel(h_ref, wg_ref, wu_ref, wd_ref, routed_ref, x1_ref, gt_ref, o_ref):
    hb = h_ref[0]
    act = _silu(_dot(hb, wg_ref[...])) * _dot(hb, wu_ref[...])
    shared = _dot(act.astype(BF16), wd_ref[...])
    o_ref[0] = x1_ref[0] + gt_ref[0] * (_from_token_tiles(routed_ref, hb.shape[0]) + shared)


def _shared_and_residual(h2, wsg, wsu, wsd, routed, x1, gt2, tm):
    B, S, D = h2.shape
    row = lambda b, i: (b, i, 0)
    xs = pl.BlockSpec((1, tm, D), row)
    const2 = lambda a: pl.BlockSpec(a.shape, lambda b, i: (0, 0))
    return pl.pallas_call(
        _shared_kernel, grid=(B, S // tm),
        in_specs=[xs, const2(wsg), const2(wsu), const2(wsd), pl.BlockSpec((1, tm * (D // LANES), LANES), row), xs,
                  pl.BlockSpec((1, 1, D), lambda b, i: (b, 0, 0))],
        out_specs=xs,
        out_shape=jax.ShapeDtypeStruct((B, S, D), F32),
        compiler_params=_cparams(("parallel", "parallel")),
        name="shared_expert_residual",
    )(h2, wsg, wsu, wsd, routed, x1, gt2)


def _prep_in_weights(w_in, g_q_dil, g_k_dil, g_q_nsa, g_k_slc, g_k_win):
    D = w_in.shape[0]
    pad = jnp.zeros((D, IN_COLS_PADDED - w_in.shape[1]), w_in.dtype)
    w_all = jnp.concatenate([w_in, pad], axis=1).astype(BF16)
    one = jnp.ones((LANES,), F32)
    gcol = jnp.concatenate([
        jnp.tile(g_q_dil, NH_DIL), jnp.tile(g_k_dil, NH_DIL), jnp.ones((512,), F32), jnp.tile(g_q_nsa, NH_NSA),
        one, one, jnp.tile(g_k_slc, NKV_NSA), one, jnp.tile(g_k_win, NKV_NSA), one, one]).reshape(1, IN_COLS_PADDED)
    return w_all, gcol


def _rope_tables(S):
    inv_freq = ROPE_THETA ** (-jnp.arange(ROT_HALF, dtype=F32) / ROT_HALF)
    ang = jnp.arange(S).astype(F32)[:, None] * inv_freq[None, :]
    cos, sin = jnp.cos(ang), jnp.sin(ang)
    zeros = jnp.zeros((S, HD - ROT_DIM), F32)
    z8 = jnp.zeros((S, ROT_HALF), F32)
    cos_h = jnp.concatenate([cos, cos, jnp.ones((S, HD - ROT_DIM), F32)], axis=1)
    sa_h = jnp.concatenate([-sin, z8, zeros], axis=1)
    sb_h = jnp.concatenate([z8, sin, zeros], axis=1)
    two = lambda t: jnp.concatenate([t, t], axis=1)
    return two(cos_h), two(sa_h), two(sb_h)


def _prep_compress(pe, w1, w2):
    eye = jnp.eye(NKV_NSA, dtype=F32)
    w1r = w1.reshape(CMP_LEN, HD, CMP_HIDDEN)

    def half(w1h, peh):
        w = jnp.einsum("ldh,gk->lgdkh", w1h, eye).reshape(CMP_STRIDE * LANES, NKV_NSA * CMP_HIDDEN)
        p = jnp.broadcast_to(peh[:, None, :], (CMP_STRIDE, NKV_NSA, HD)).reshape(1, CMP_STRIDE * LANES)
        return p, w.astype(BF16)

    pa, wa = half(w1r[:CMP_STRIDE], pe[:CMP_STRIDE])
    pb, wb = half(w1r[CMP_STRIDE:], pe[CMP_STRIDE:])
    w2bd = jnp.einsum("hd,gk->ghkd", w2, eye).reshape(NKV_NSA * CMP_HIDDEN, LANES).astype(BF16)
    return pa, pb, wa, wb, w2bd


def _overlap_tables(S):
    ncb = S // CMP_STRIDE
    n_sel = S // SEL_BLOCK
    cs = np.arange(ncb) * CMP_STRIDE
    ss = np.arange(n_sel) * SEL_BLOCK
    ov = np.clip(np.minimum(cs[:, None] + CMP_LEN, ss[None, :] + SEL_BLOCK) - np.maximum(cs[:, None], ss[None, :]), 0, None)
    ov = ov.astype(np.float32) / CMP_STRIDE
    ovl = np.zeros((ncb, LANES), np.float32)
    ovr = np.zeros((ncb, LANES), np.float32)
    ovl[:, :n_sel] = ov
    ovr[:, HD:HD + n_sel] = ov
    return jnp.asarray(ovl, BF16), jnp.asarray(ovr, BF16)


def _block_ones():
    r = np.arange(LANES)
    return jnp.asarray((r[:, None] // HD == r[None, :] // HD).astype(np.float32), BF16)


def _layer(x, c, w_ada, b_ada, g_norm_mix, g_norm_ffn, w_in, g_q_dil, g_k_dil, g_q_nsa, g_k_cmp, g_k_slc, g_k_win,
           cmp_pe_k, cmp_w1_k, cmp_w2_k, cmp_pe_v, cmp_w1_v, cmp_w2_v, w_out, w_router, router_bias,
           w_gate, w_up, w_down, ws_gate, ws_up, ws_down):
    B, S, D = x.shape
    ones_bd = _block_ones()

    c_pad = jnp.zeros((8, D), F32).at[:B].set(c)
    mod = _ada_mod(c_pad, w_ada, b_ada.reshape(1, -1))[:B]
    sh1, sc1, gt1, sh2, sc2, gt2 = [m.reshape(B, 1, D) for m in jnp.split(mod, 6, axis=-1)]

    w_all, gcol = _prep_in_weights(w_in, g_q_dil, g_k_dil, g_q_nsa, g_k_slc, g_k_win)
    cos_t, sa_t, sb_t = _rope_tables(S)
    qa, ka, va, qn, qr, kc, vc, ksl, vsl, kw, vw, gate = _in_projection(
        x, g_norm_mix.reshape(1, D), sc1, sh1, w_all, gcol, cos_t, sa_t, sb_t, ones_bd, tm=512)

    o_a = _dilated_attention(qa, ka, va)

    kcmp, vcmp = _compress(kc, vc, _prep_compress(cmp_pe_k, cmp_w1_k, cmp_w2_k),
                           _prep_compress(cmp_pe_v, cmp_w1_v, cmp_w2_v), jnp.tile(g_k_cmp, NKV_NSA).reshape(1, LANES),
                           ones_bd)
    ovl, ovr = _overlap_tables(S)
    o_cmp, sel = _cmp_select(qn, kcmp, vcmp, ovl, ovr, gate, tq=128)
    o_cs = _nsa_branch(qr, ksl, vsl, sel, o_cmp, gate, tq=256, tk=512, mode="selected", out_dtype=F32)
    o_b = _nsa_branch(qr, kw, vw, None, o_cs, gate, tq=256, tk=256, mode="window", out_dtype=BF16)

    w_out_b = w_out.astype(BF16)
    x1, h2, h2_tiles = _out_projection(o_a, o_b, w_out_b[:NH_DIL * HD], w_out_b[NH_DIL * HD:], x, gt1,
                                       g_norm_ffn.reshape(1, D), sc2, sh2, tm=256)

    tm_r = 256
    eidx, gates, counts = _router(h2, w_router.T.astype(BF16),
                                  jnp.broadcast_to(router_bias.reshape(N_EXPERTS, 1), (N_EXPERTS, tm_r)), tm=tm_r)
    tables = _dispatch_tables(eidx, gates, counts[:, :, 0].astype(I32), B, S)
    weights = (w_gate, w_up, w_down)
    routed = h2_tiles
    for b in range(B):
        res = _routed_experts(routed, b, *[tbl[b] for tbl in tables], *weights, emit_bf16=(b == 0 and B > 1))
        routed = res[0]
        if len(res) > 1:
            all_used = tables[2][0, 0] == N_EXPERTS
            weights = lax.cond(all_used, lambda: tuple(res[1:]), lambda: tuple(w.astype(BF16) for w in (w_gate, w_up, w_down)))
    return _shared_and_residual(h2, ws_gate.astype(BF16), ws_up.astype(BF16), ws_down.astype(BF16), routed, x1, gt2,
                                tm=256)


def kernel(x, c, w_ada, b_ada, g_norm_mix, g_norm_ffn, w_in, g_q_dil, g_k_dil, g_q_nsa, g_k_cmp, g_k_slc, g_k_win, cmp_pe_k, cmp_w1_k, cmp_w2_k, cmp_pe_v, cmp_w1_v, cmp_w2_v, w_out, w_router, router_bias, w_gate, w_up, w_down, ws_gate, ws_up, ws_down):
    params = (w_ada, b_ada, g_norm_mix, g_norm_ffn, w_in, g_q_dil, g_k_dil, g_q_nsa, g_k_cmp, g_k_slc, g_k_win,
              cmp_pe_k, cmp_w1_k, cmp_w2_k, cmp_pe_v, cmp_w1_v, cmp_w2_v, w_out, w_router, router_bias,
              w_gate, w_up, w_down, ws_gate, ws_up, ws_down)
    for layer in range(w_ada.shape[0]):
        x = _layer(x, c, *[a[layer] for a in params])
    return x
```

```python
import functools
import math

import numpy as np
import jax
import jax.numpy as jnp
from jax import lax
from jax.experimental import pallas as pl
from jax.experimental.pallas import tpu as pltpu

F32 = jnp.float32
BF16 = jnp.bfloat16
I32 = jnp.int32

HD = 64
LANES = 128
NH_DIL = 8
NH_NSA = 8
NKV_NSA = 2
NSA_REP = NH_NSA // NKV_NSA
DIL_PAIRS = ((128, 1), (512, 4), (2048, 16))
DIL_TILE = 128
DIL_GROUP = 8
ROPE_THETA = 500000.0
ROT_DIM = HD // 4
ROT_HALF = ROT_DIM // 2
CMP_LEN = 32
CMP_STRIDE = 16
CMP_HIDDEN = 128
SEL_BLOCK = 64
SEL_SHIFT = 6
SEL_TOP = 16
NSA_WINDOW = 512
N_EXPERTS = 256
TOP_K = 8
TOP_K_SHIFT = 3
N_GROUPS = 8
GROUP_SIZE = N_EXPERTS // N_GROUPS
TOPK_GROUPS = 4
EXPERT_FF = 256
SHARED_FF = 256
ROUTED_SCALE = 2.5
EPS = 1e-6
QK_SCALE = 1.0 / math.sqrt(HD)

NEG = -1e30
BIG = 3e38

C_QA, C_KA, C_VA, C_QB = 0, 512, 1024, 1536
C_KC, C_VC, C_KSL, C_VSL, C_KW, C_VW, C_GB = 2048, 2176, 2304, 2432, 2560, 2688, 2816
IN_COLS_PADDED = 2944
TRANSPOSED_OUTS = (8, 10)

MOE_BLOCK = 128
NULL_ROWS = 8
WEIGHT_RING = 5
SCATTER_GROUP = 16
VMEM_LIMIT = 48 * 1024 * 1024


def _cparams(sem, vmem=VMEM_LIMIT):
    return pltpu.CompilerParams(dimension_semantics=sem, vmem_limit_bytes=vmem)


def _sigmoid(v):
    return 1.0 / (1.0 + jnp.exp(-v))


def _silu(v):
    return v * _sigmoid(v)


def _dot(a, b):
    return jnp.dot(a, b, preferred_element_type=F32)


def _dot_nt(a, b):
    return lax.dot_general(a, b, (((1,), (1,)), ((), ())), preferred_element_type=F32)


def _lane_lo(shape):
    return lax.broadcasted_iota(I32, shape, len(shape) - 1) < HD


def _head_sums(v, ones_bd):
    hi = v.astype(BF16)
    lo = (v - hi.astype(F32)).astype(BF16)
    return _dot(hi, ones_bd) + _dot(lo, ones_bd)


def _head_rmsnorm(y, ones_bd, gain):
    ms = _head_sums(y * y, ones_bd) * (1.0 / HD)
    return y * lax.rsqrt(ms + EPS) * gain


def _rope(y, cos, sin_a, sin_b):
    return y * cos + pltpu.roll(y, LANES - ROT_HALF, 1) * sin_a + pltpu.roll(y, ROT_HALF, 1) * sin_b


def _ada_kernel(c_ref, w_ref, b_ref, o_ref):
    a = _silu(c_ref[...]).astype(BF16)
    o_ref[...] = _dot(a, w_ref[...].astype(BF16)) + b_ref[...]


def _ada_mod(c_pad, w_ada, b_ada):
    rows, d = c_pad.shape
    n = w_ada.shape[1]
    tn = 1536 if n % 1536 == 0 else n
    return pl.pallas_call(
        _ada_kernel,
        grid=(n // tn,),
        in_specs=[pl.BlockSpec((rows, d), lambda j: (0, 0)),
                  pl.BlockSpec((d, tn), lambda j: (0, j)),
                  pl.BlockSpec((1, tn), lambda j: (0, j))],
        out_specs=pl.BlockSpec((rows, tn), lambda j: (0, j)),
        out_shape=jax.ShapeDtypeStruct((rows, n), F32),
        compiler_params=_cparams(("arbitrary",)),
        name="ada_mod",
    )(c_pad, w_ada, b_ada)


def _inproj_kernel(x_ref, g_ref, sc_ref, sh_ref, w_ref, gcol_ref, cos_ref, sa_ref, sb_ref, ones_ref,
                   qa_ref, ka_ref, va_ref, qn_ref, qr_ref, kc_ref, vc_ref, ksl_ref, vsl_ref, kw_ref, vw_ref,
                   gate_ref):
    x = x_ref[0]
    ms = jnp.mean(x * x, axis=-1, keepdims=True)
    h = (x * lax.rsqrt(ms + EPS)) * g_ref[...]
    h = h * (1.0 + sc_ref[0]) + sh_ref[0]
    hb = h.astype(BF16)
    ones_bd = ones_ref[...]
    cos, sa, sb = cos_ref[...], sa_ref[...], sb_ref[...]
    lo = _lane_lo(cos.shape)

    def proj(c0, width):
        return _dot(hb, w_ref[:, c0:c0 + width])

    def normed(tile, c0):
        return _head_rmsnorm(tile, ones_bd, gcol_ref[:, c0:c0 + LANES])

    acc = proj(C_QA, 512)
    for p in range(4):
        y = _rope(normed(acc[:, p * LANES:(p + 1) * LANES], C_QA + p * LANES), cos, sa, sb) * QK_SCALE
        qa_ref[0, :, p * LANES:(p + 1) * LANES] = y
    acc = proj(C_KA, 512)
    for p in range(4):
        y = _rope(normed(acc[:, p * LANES:(p + 1) * LANES], C_KA + p * LANES), cos, sa, sb)
        ka_ref[0, :, p * LANES:(p + 1) * LANES] = y
    va_ref[0] = proj(C_VA, 512)

    acc = proj(C_QB, 512)
    for p in range(4):
        yn = normed(acc[:, p * LANES:(p + 1) * LANES], C_QB + p * LANES)
        yr = _rope(yn, cos, sa, sb)
        for y, ref in ((yn * QK_SCALE, qn_ref), (yr * QK_SCALE, qr_ref)):
            ysw = pltpu.roll(y, HD, 1)
            for half in range(2):
                head = 2 * p + half
                grp = head // NSA_REP
                src = y if grp == half else ysw
                keep = lo if grp == 0 else jnp.logical_not(lo)
                ref[0, :, head * LANES:(head + 1) * LANES] = jnp.where(keep, src, 0.0).astype(BF16)

    acc = proj(C_KC, 512)
    kc_ref[0] = acc[:, 0:LANES]
    vc_ref[0] = acc[:, LANES:2 * LANES]
    ksl_ref[0] = _rope(normed(acc[:, 2 * LANES:3 * LANES], C_KSL), cos, sa, sb).astype(BF16)
    vsl_ref[0] = _transposed(acc[:, 3 * LANES:4 * LANES]).astype(BF16)
    acc = proj(C_KW, 384)
    kw_ref[0] = _rope(normed(acc[:, 0:LANES], C_KW), cos, sa, sb).astype(BF16)
    vw_ref[0] = _transposed(acc[:, LANES:2 * LANES]).astype(BF16)
    gate_ref[0] = _sigmoid(acc[:, 2 * LANES:3 * LANES])


def _transposed(tile):
    return jnp.concatenate([tile[r0:r0 + LANES].T for r0 in range(0, tile.shape[0], LANES)], axis=1)


def _in_projection(x, g_mix, sc1, sh1, w_all, gcol, cos_t, sa_t, sb_t, ones_bd, tm):
    B, S, D = x.shape
    nc = w_all.shape[1]
    row = lambda b, i: (b, i, 0)
    bvec = pl.BlockSpec((1, 1, D), lambda b, i: (b, 0, 0))
    tab = pl.BlockSpec((tm, LANES), lambda b, i: (i, 0))
    const2 = lambda shape: pl.BlockSpec(shape, lambda b, i: (0, 0))
    widths_dt = [(512, F32), (512, F32), (512, F32), (1024, BF16), (1024, BF16), (LANES, F32), (LANES, F32),
                 (LANES, BF16), (LANES, BF16), (LANES, BF16), (LANES, BF16), (LANES, F32)]
    return pl.pallas_call(
        _inproj_kernel,
        grid=(B, S // tm),
        in_specs=[pl.BlockSpec((1, tm, D), row), const2((1, D)), bvec, bvec, const2((D, nc)), const2((1, nc)),
                  tab, tab, tab, const2((LANES, LANES))],
        out_specs=[pl.BlockSpec((1, tm, w), row) if n not in TRANSPOSED_OUTS else pl.BlockSpec((1, w, tm), lambda b, i: (b, 0, i))
                   for n, (w, _) in enumerate(widths_dt)],
        out_shape=[jax.ShapeDtypeStruct((B, S, w) if n not in TRANSPOSED_OUTS else (B, w, S), dt)
                   for n, (w, dt) in enumerate(widths_dt)],
        compiler_params=_cparams(("parallel", "parallel")),
        name="in_projection",
    )(x, g_mix, sc1, sh1, w_all, gcol, cos_t, sa_t, sb_t, ones_bd)


def _dot_tn(a, b):
    return lax.dot_general(a, b, (((0,), (0,)), ((), ())), preferred_element_type=F32)


def _dilated_kernel(q_ref, k_ref, v_ref, o_ref, oacc, lacc, *, pairs):
    seq = q_ref.shape[1]
    t = DIL_TILE
    kk = lax.broadcasted_iota(I32, (t, t), 0)
    qq = lax.broadcasted_iota(I32, (t, t), 1)
    bias_cur = jnp.where(qq >= kk, 0.0, NEG)
    bias_prev = jnp.where(qq <= kk, 0.0, NEG)
    bias_none = jnp.full((t, t), NEG, F32)
    lane_lo = _lane_lo((t, LANES))
    row_lo = lax.broadcasted_iota(I32, (LANES, t), 0) < HD

    def blocks(d, starts, mode):
        def strided(ref, start):
            return (ref[0, pl.ds(start, t, stride=d), :] if d > 1 else ref[0, pl.ds(start, t), :]).astype(BF16)

        work, last = [], None
        for qs, first, chained in starts:
            rows_q = pl.ds(qs, t, stride=d) if d > 1 else pl.ds(qs, t)
            qt = q_ref[0, rows_q, :].astype(BF16)
            zero = jnp.zeros_like(qt)
            q2 = jnp.concatenate([jnp.where(lane_lo, qt, zero), jnp.where(lane_lo, zero, qt)], axis=0)
            cur = (strided(k_ref, qs), strided(v_ref, qs))
            if chained:
                prev, bias_p = last, bias_prev
            elif first is True:
                prev, bias_p = cur, bias_none
            else:
                prev_start = jnp.where(first, qs, qs - t * d)
                prev, bias_p = (strided(k_ref, prev_start), strided(v_ref, prev_start)), jnp.where(first, NEG, bias_prev)
            last = cur
            k2 = jnp.concatenate([prev[0], cur[0]], axis=0)
            v2 = jnp.concatenate([prev[1], cur[1]], axis=0)
            bias = jnp.concatenate([bias_p, bias_cur], axis=0)
            s = _dot_nt(k2, q2) + jnp.concatenate([bias, bias], axis=1)
            work.append([rows_q, v2, s])
        for w in work:
            s = w[2]
            m = s.max(axis=0, keepdims=True)
            p = jnp.exp(s - m)
            l = p.sum(axis=0, keepdims=True)
            w[2:] = [p.astype(BF16), l, m + jnp.log(l)]
        for w in work:
            w[1] = _dot_tn(w[1], w[2]) / w[3]
        for w in work:
            ov, lse2 = w[1], w[4]
            w[1:] = [jnp.where(row_lo, ov[:, :t], ov[:, t:]).T, jnp.where(row_lo, lse2[:, :t], lse2[:, t:]).T]
        for rows_q, o, lse in work:
            if mode == "init":
                oacc[rows_q, :] = o
                lacc[rows_q, :] = lse
                continue
            lp = lacc[rows_q, :]
            mx = jnp.maximum(lp, lse)
            wp, wn = jnp.exp(lp - mx), jnp.exp(lse - mx)
            den = wp + wn
            oacc[rows_q, :] = (oacc[rows_q, :] * wp + o * wn) / den
            if mode == "mid":
                lacc[rows_q, :] = mx + jnp.log(den)

    modes = ("init",) + ("mid",) * (len(pairs) - 2) + ("last",)
    for (window, d), mode in zip(pairs, modes):
        assert window // d == t and d & (d - 1) == 0
        n_j = seq // (t * d)
        assert n_j & (n_j - 1) == 0 and (seq // t) % DIL_GROUP == 0

        def body(i, c, d=d, mode=mode, n_j=n_j):
            starts = []
            for u in range(DIL_GROUP):
                n = i * DIL_GROUP + u
                r, j = n >> (n_j.bit_length() - 1), n & (n_j - 1)
                if n_j <= DIL_GROUP:
                    first, chained = (u % n_j == 0), (u % n_j != 0)
                else:
                    first, chained = (j == 0), (u != 0)
                starts.append((j * (t * d) + r, first, chained))
            blocks(d, starts, mode)
            return c

        lax.fori_loop(0, seq // t // DIL_GROUP, body, 0)
    o_ref[0] = oacc[...].astype(o_ref.dtype)


def _dilated_attention(qa, ka, va):
    B, S, w = qa.shape
    assert S % (DIL_TILE * max(d for _, d in DIL_PAIRS)) == 0
    kern = functools.partial(_dilated_kernel, pairs=DIL_PAIRS)
    whole = pl.BlockSpec((1, S, LANES), lambda b, p: (b, 0, p))
    acc = pltpu.VMEM((S, LANES), F32)
    return pl.pallas_call(
        kern, grid=(B, w // LANES),
        in_specs=[whole, whole, whole], out_specs=whole,
        out_shape=jax.ShapeDtypeStruct((B, S, w), BF16),
        scratch_shapes=[acc, acc],
        compiler_params=_cparams(("parallel", "parallel")),
        name="dilated_attention",
    )(qa, ka, va)


def _compress_kernel(kc_ref, vc_ref, pak_ref, pbk_ref, wak_ref, wbk_ref, w2k_ref, gk_ref, ones_ref,
                     pav_ref, pbv_ref, wav_ref, wbv_ref, w2v_ref, ko_ref, vo_ref):
    ncb = kc_ref.shape[1]

    def branch(t_ref, pa, pb, wa, wb, w2):
        t = t_ref[0]
        ua = _dot((t + pa[...]).astype(BF16), wa[...])
        ub = _dot((t + pb[...]).astype(BF16), wb[...])
        pre = ua + pltpu.roll(ub, ncb - 1, 0)
        return _dot(jax.nn.gelu(pre).astype(BF16), w2[...])

    live = lax.broadcasted_iota(I32, (ncb, LANES), 0) < ncb - 1
    kcmp = _head_rmsnorm(branch(kc_ref, pak_ref, pbk_ref, wak_ref, wbk_ref, w2k_ref), ones_ref[...], gk_ref[...])
    ko_ref[0] = jnp.where(live, kcmp, 0.0).astype(BF16)
    vo_ref[0] = jnp.where(live, branch(vc_ref, pav_ref, pbv_ref, wav_ref, wbv_ref, w2v_ref), 0.0).astype(BF16)


def _compress(kc, vc, wk, wv, gk2, ones_bd):
    B, S, _ = kc.shape
    ncb = S // CMP_STRIDE
    cw = CMP_STRIDE * LANES
    view = lambda a: a.reshape(B, ncb, cw)
    blk = pl.BlockSpec((1, ncb, cw), lambda b: (b, 0, 0))
    full = lambda a: pl.BlockSpec(a.shape, lambda b: (0,) * a.ndim)
    consts = [*wk, gk2, ones_bd, *wv]
    o_spec = pl.BlockSpec((1, ncb, LANES), lambda b: (b, 0, 0))
    return pl.pallas_call(
        _compress_kernel, grid=(B,),
        in_specs=[blk, blk] + [full(a) for a in consts],
        out_specs=[o_spec, o_spec],
        out_shape=[jax.ShapeDtypeStruct((B, ncb, LANES), BF16)] * 2,
        compiler_params=_cparams(("parallel",)),
        name="nsa_compress",
    )(view(kc), view(vc), *consts)


def _pair_tile(lo, t0, half0, t1, half1):
    a = t0 if half0 == 0 else pltpu.roll(t0, HD, 1)
    b = t1 if half1 == 1 else pltpu.roll(t1, HD, 1)
    return jnp.where(lo, a, b)


def _cmpsel_kernel(qn_ref, kc_ref, vc_ref, ovl_ref, ovr_ref, gate_ref, o_ref, sel_ref, *, tq, n_sel, top):
    ncb = kc_ref.shape[1]
    i = pl.program_id(1)
    pos0 = i * tq
    qpos = pos0 + lax.broadcasted_iota(I32, (tq, ncb), 0)
    cend = lax.broadcasted_iota(I32, (tq, ncb), 1) * CMP_STRIDE + (CMP_LEN - 1)
    valid = cend <= qpos
    lo = _lane_lo((tq, LANES))
    kc = kc_ref[0]
    vc = vc_ref[0]

    psum = [None] * NKV_NSA
    scores = [_dot_nt(qn_ref[0, :, h * LANES:(h + 1) * LANES], kc) for h in range(NH_NSA)]
    probs = []
    for h, s in enumerate(scores):
        g = h // NSA_REP
        m = jnp.where(valid, s, NEG).max(axis=1, keepdims=True)
        m = jnp.where(m > 0.5 * NEG, m, 0.0)
        p = jnp.where(valid, jnp.exp(s - m), 0.0)
        l = p.sum(axis=1, keepdims=True)
        p = p / jnp.where(l > 0.0, l, 1.0)
        probs.append(p.astype(BF16))
        psum[g] = p if psum[g] is None else psum[g] + p
    outs = [_dot(p, vc) for p in probs]

    for p_ in range(NH_NSA // 2):
        h0, h1 = 2 * p_, 2 * p_ + 1
        o_pair = _pair_tile(lo, outs[h0], h0 // NSA_REP, outs[h1], h1 // NSA_REP)
        g = jnp.where(lo, gate_ref[0, :, 3 * h0:3 * h0 + 1], gate_ref[0, :, 3 * h1:3 * h1 + 1])
        o_ref[0, :, p_ * LANES:(p_ + 1) * LANES] = g * o_pair

    def hilo_dot(a, w):
        hi = a.astype(BF16)
        return _dot(hi, w) + _dot((a - hi.astype(F32)).astype(BF16), w)

    pslc = hilo_dot(psum[0], ovl_ref[...]) + hilo_dot(psum[1], ovr_ref[...])
    pslc_t = pslc.T
    jblk = lax.broadcasted_iota(I32, (HD, tq), 0)
    cur = (pos0 + lax.broadcasted_iota(I32, (HD, tq), 1)) >> SEL_SHIFT
    forced = (jblk == 0) | (jblk == cur) | (jblk == cur - 1)
    sel_t = []
    for g in range(NKV_NSA):
        sc = jnp.where(forced, BIG, jnp.where(jblk <= cur, pslc_t[g * HD:(g + 1) * HD, :], -BIG))
        rank = jnp.zeros((HD, tq), F32)
        for k in range(n_sel):
            rk = sc[k:k + 1, :]
            rank = rank + jnp.where(jblk > k, jnp.where(rk >= sc, 1.0, 0.0), jnp.where(rk > sc, 1.0, 0.0))
        sel_t.append(jnp.where(rank < top, 1.0, 0.0))
    sel_ref[0] = jnp.concatenate(sel_t, axis=0)


def _cmp_select(qn, kcmp, vcmp, ovl, ovr, gate, tq):
    B, S, qw = qn.shape
    ncb = kcmp.shape[1]
    n_sel = S // SEL_BLOCK
    assert n_sel <= HD and tq == LANES
    row = lambda b, i: (b, i, 0)
    cb = pl.BlockSpec((1, ncb, LANES), lambda b, i: (b, 0, 0))
    ovs = pl.BlockSpec((ncb, LANES), lambda b, i: (0, 0))
    kern = functools.partial(_cmpsel_kernel, tq=tq, n_sel=n_sel, top=min(SEL_TOP, n_sel))
    return pl.pallas_call(
        kern, grid=(B, S // tq),
        in_specs=[pl.BlockSpec((1, tq, qw), row), cb, cb, ovs, ovs, pl.BlockSpec((1, tq, LANES), row)],
        out_specs=[pl.BlockSpec((1, tq, NH_NSA * HD), row), pl.BlockSpec((1, LANES, tq), lambda b, i: (b, 0, i))],
        out_shape=[jax.ShapeDtypeStruct((B, S, NH_NSA * HD), F32), jax.ShapeDtypeStruct((B, LANES, S), F32)],
        compiler_params=_cparams(("parallel", "parallel")),
        name="nsa_cmp_select",
    )(qn, kcmp, vcmp, ovl, ovr, gate)


def _nsa_kernel(*refs, tq, tk, mode):
    qi_ref, tile_ref, edge_ref = refs[:3]
    if mode == "selected":
        q_ref, k_ref, vt_ref, selt_ref, prev_ref, gate_ref, o_ref, m_ref, l_ref, acc_ref = refs[3:]
    else:
        q_ref, k_ref, vt_ref, prev_ref, gate_ref, o_ref, m_ref, l_ref, acc_ref = refs[3:]
    step = pl.program_id(1)
    i = qi_ref[step]
    tile = tile_ref[step]
    edge = edge_ref[step]

    @pl.when((edge & 1) != 0)
    def _():
        m_ref[...] = jnp.full(m_ref.shape, NEG, F32)
        l_ref[...] = jnp.zeros(l_ref.shape, F32)
        acc_ref[...] = jnp.zeros(acc_ref.shape, F32)

    def accumulate_tile():
        kpos = tile * tk + lax.broadcasted_iota(I32, (tk, tq), 0)
        qpos = i * tq + lax.broadcasted_iota(I32, (tk, tq), 1)
        k = k_ref[0]
        vt = vt_ref[0]
        if mode == "window":
            dist = qpos - kpos
            bias_w = jnp.where(dist >= 0, jnp.where(dist < NSA_WINDOW, 0.0, NEG), NEG)
        groups = range(NKV_NSA)
        scores = []
        for g in groups:
            if mode == "selected":
                nblk = tk // SEL_BLOCK
                rows = selt_ref[0, pl.ds(pl.multiple_of(g * HD + tile * nblk, nblk), nblk), :]
                chosen = jnp.broadcast_to(rows[:, None, :], (nblk, SEL_BLOCK, tq)).reshape(tk, tq) > 0.5
                bias = jnp.where(kpos <= qpos, jnp.where(chosen, 0.0, NEG), NEG)
            else:
                bias = bias_w
            heads = range(g * NSA_REP, (g + 1) * NSA_REP)
            q4 = jnp.concatenate([q_ref[0, :, h * LANES:(h + 1) * LANES] for h in heads], axis=0)
            scores.append(_dot_nt(k, q4) + jnp.concatenate([bias] * NSA_REP, axis=1))
        probs = []
        for g in groups:
            m_prev = m_ref[g]
            m_new = jnp.maximum(m_prev, scores[g].max(axis=0, keepdims=True))
            alpha = jnp.exp(m_prev - m_new)
            p = jnp.exp(scores[g] - m_new[0:1])
            l_ref[g] = alpha * l_ref[g] + p.sum(axis=0, keepdims=True)
            m_ref[g] = m_new
            probs.append((alpha[0:1], p.astype(BF16)))
        for g in groups:
            alpha, p = probs[g]
            acc_ref[g] = alpha * acc_ref[g] + _dot(vt, p)

    accumulate_tile()

    @pl.when((edge & 2) != 0)
    def _():
        lo = _lane_lo((tq, LANES))
        gate_col = 1 if mode == "selected" else 2

        def head_t(h):
            g, r = divmod(h, NSA_REP)
            cols = slice(r * tq, (r + 1) * tq)
            return acc_ref[g, g * HD:(g + 1) * HD, cols] / l_ref[g, 0:1, cols]

        for p_ in range(NH_NSA // 2):
            h0, h1 = 2 * p_, 2 * p_ + 1
            o_pair = jnp.concatenate([head_t(h0), head_t(h1)], axis=0).T
            c0, c1 = 3 * h0 + gate_col, 3 * h1 + gate_col
            gt = jnp.where(lo, gate_ref[0, :, c0:c0 + 1], gate_ref[0, :, c1:c1 + 1])
            sl = slice(p_ * LANES, (p_ + 1) * LANES)
            o_ref[0, :, sl] = (prev_ref[0, :, sl] + gt * o_pair).astype(o_ref.dtype)


def _nsa_branch(qr, k, v_t, sel_t, prev, gate, *, tq, tk, mode, out_dtype):
    B, S, qw = qr.shape
    ow = NH_NSA * HD
    steps = []
    for i in range(S // tq):
        hi = (i * tq + tq - 1) // tk
        lo = 0 if mode == "selected" else max((i * tq - (NSA_WINDOW - 1)) // tk, 0)
        steps += [(i, t, (t == lo) + 2 * (t == hi)) for t in range(lo, hi + 1)]
    tables = [jnp.asarray(np.array(col, np.int32)) for col in zip(*steps)]
    row = lambda b, s, qi, tl, ed: (b, qi[s], 0)
    in_specs = [pl.BlockSpec((1, tq, qw), row),
                pl.BlockSpec((1, tk, LANES), lambda b, s, qi, tl, ed: (b, tl[s], 0)),
                pl.BlockSpec((1, LANES, tk), lambda b, s, qi, tl, ed: (b, 0, tl[s]))]
    args = [qr, k, v_t]
    if mode == "selected":
        in_specs.append(pl.BlockSpec((1, LANES, tq), lambda b, s, qi, tl, ed: (b, 0, qi[s])))
        args.append(sel_t)
    in_specs += [pl.BlockSpec((1, tq, ow), row), pl.BlockSpec((1, tq, LANES), row)]
    args += [prev, gate]
    stat = pltpu.VMEM((NKV_NSA, 8, NSA_REP * tq), F32)
    grid_spec = pltpu.PrefetchScalarGridSpec(
        num_scalar_prefetch=3, grid=(B, len(steps)), in_specs=in_specs,
        out_specs=pl.BlockSpec((1, tq, ow), row),
        scratch_shapes=[stat, stat, pltpu.VMEM((NKV_NSA, LANES, NSA_REP * tq), F32)])
    return pl.pallas_call(
        functools.partial(_nsa_kernel, tq=tq, tk=tk, mode=mode), grid_spec=grid_spec,
        out_shape=jax.ShapeDtypeStruct((B, S, ow), out_dtype),
        compiler_params=_cparams(("parallel", "arbitrary")),
        name=f"nsa_{mode}_attention",
    )(*tables, *args)


def _to_token_tiles(ref, value):
    m, d = value.shape
    rpt = d // LANES
    for j in range(rpt):
        ref[0, pl.ds(j, m, stride=rpt), :] = value[:, j * LANES:(j + 1) * LANES]


def _from_token_tiles(ref, m):
    rpt = ref.shape[1] // m
    return jnp.concatenate([ref[0, pl.ds(j, m, stride=rpt), :] for j in range(rpt)], axis=1)


def _outproj_kernel(oa_ref, ob_ref, wt_ref, wb_ref, x_ref, gt_ref, g_ref, sc_ref, sh_ref, x1_ref, h2_ref, h2t_ref):
    mix = _dot(oa_ref[0], wt_ref[...]) + _dot(ob_ref[0], wb_ref[...])
    x1 = x_ref[0] + gt_ref[0] * mix
    x1_ref[0] = x1
    ms = jnp.mean(x1 * x1, axis=-1, keepdims=True)
    h = (x1 * lax.rsqrt(ms + EPS)) * g_ref[...]
    h2 = h * (1.0 + sc_ref[0]) + sh_ref[0]
    h2_ref[0] = h2.astype(h2_ref.dtype)
    _to_token_tiles(h2t_ref, h2)


def _out_projection(oa, ob, w_top, w_bot, x, gt1, g_ffn, sc2, sh2, tm):
    B, S, D = x.shape
    row = lambda b, i: (b, i, 0)
    bvec = pl.BlockSpec((1, 1, D), lambda b, i: (b, 0, 0))
    const2 = lambda a: pl.BlockSpec(a.shape, lambda b, i: (0, 0))
    xs = pl.BlockSpec((1, tm, D), row)
    return pl.pallas_call(
        _outproj_kernel, grid=(B, S // tm),
        in_specs=[pl.BlockSpec((1, tm, oa.shape[2]), row), pl.BlockSpec((1, tm, ob.shape[2]), row),
                  const2(w_top), const2(w_bot), xs, bvec, const2(g_ffn), bvec, bvec],
        out_specs=[xs, xs, pl.BlockSpec((1, tm * (D // LANES), LANES), row)],
        out_shape=[jax.ShapeDtypeStruct((B, S, D), F32), jax.ShapeDtypeStruct((B, S, D), BF16),
                   jax.ShapeDtypeStruct((B, S * (D // LANES), LANES), F32)],
        compiler_params=_cparams(("parallel", "parallel")),
        name="out_projection",
    )(oa, ob, w_top, w_bot, x, gt1, g_ffn, sc2, sh2)


def _router_kernel(h_ref, wr_ref, bias_ref, eidx_ref, gate_ref, cnt_ref):
    tm = h_ref.shape[1]
    logits = _dot_nt(wr_ref[...], h_ref[0])
    aff = _sigmoid(logits)
    biased = aff + bias_ref[...]
    b3 = biased.reshape(N_GROUPS, GROUP_SIZE, tm)
    jj = lax.broadcasted_iota(I32, (N_GROUPS, GROUP_SIZE, tm), 1).astype(F32)
    m1 = b3.max(axis=1, keepdims=True)
    i1 = jnp.where(b3 == m1, jj, 1e9).min(axis=1, keepdims=True)
    m2 = jnp.where(jj == i1, -jnp.inf, b3).max(axis=1, keepdims=True)
    gs = (m1 + m2).reshape(N_GROUPS, tm)
    gi = lax.broadcasted_iota(I32, (N_GROUPS, tm), 0)
    rank = jnp.zeros((N_GROUPS, tm), F32)
    for k in range(N_GROUPS):
        rk = gs[k:k + 1, :]
        rank = rank + jnp.where(gi > k, jnp.where(rk >= gs, 1.0, 0.0), jnp.where(rk > gs, 1.0, 0.0))
    gsel = (rank < TOPK_GROUPS).reshape(N_GROUPS, 1, tm)
    masked = jnp.where(gsel, b3, -jnp.inf).reshape(N_EXPERTS, tm)
    ee = lax.broadcasted_iota(I32, (N_EXPERTS, tm), 0).astype(F32)
    idx_rows, sel_rows = [], []
    chosen = jnp.zeros((N_EXPERTS, tm), F32)
    for _ in range(TOP_K):
        m = masked.max(axis=0, keepdims=True)
        idx = jnp.where(masked == m, ee, 1e9).min(axis=0, keepdims=True)
        hit = ee == idx
        idx_rows.append(idx)
        sel_rows.append(jnp.where(hit, aff, 0.0).sum(axis=0, keepdims=True))
        masked = jnp.where(hit, -jnp.inf, masked)
        chosen = chosen + jnp.where(hit, 1.0, 0.0)
    sel = jnp.concatenate(sel_rows, axis=0)
    eidx_ref[...] = jnp.concatenate(idx_rows, axis=0).astype(I32)
    gate_ref[...] = sel / sel.sum(axis=0, keepdims=True) * ROUTED_SCALE

    @pl.when(pl.program_id(1) == 0)
    def _():
        cnt_ref[...] = jnp.zeros(cnt_ref.shape, F32)

    cnt_ref[0] = cnt_ref[0] + chosen.sum(axis=1, keepdims=True)


def _router(h2, wr_t, bias_b, tm):
    B, S, D = h2.shape
    nt = S // tm
    o_spec = pl.BlockSpec((TOP_K, tm), lambda b, i: (0, b * nt + i))
    return pl.pallas_call(
        _router_kernel, grid=(B, nt),
        in_specs=[pl.BlockSpec((1, tm, D), lambda b, i: (b, i, 0)), pl.BlockSpec(wr_t.shape, lambda b, i: (0, 0)),
                  pl.BlockSpec(bias_b.shape, lambda b, i: (0, 0))],
        out_specs=[o_spec, o_spec, pl.BlockSpec((1, N_EXPERTS, LANES), lambda b, i: (b, 0, 0))],
        out_shape=[jax.ShapeDtypeStruct((TOP_K, B * S), I32), jax.ShapeDtypeStruct((TOP_K, B * S), F32),
                   jax.ShapeDtypeStruct((B, N_EXPERTS, LANES), F32)],
        compiler_params=_cparams(("parallel", "arbitrary")),
        name="moe_router",
    )(h2, wr_t, bias_b)


def _experts_kernel(ord_ref, uexp_ref, nused_ref, row0_ref, nvalid_ref, nblk_ref, skey_ref, gflat_ref,
                    h_all, wg_hbm, wu_hbm, wd_hbm, o_all, h_vmem, acc_ref, xg0, xg1, y0, y1, wg_ring, wu_ring, wd_ring,
                    sem, wsem, *, batch):
    h_hbm, o_hbm = h_all.at[batch], o_all.at[batch]
    b = pl.program_id(0)
    last_step = pl.num_programs(0) - 1
    nblk = nblk_ref[0]
    rpt = wg_hbm.shape[1] // LANES
    tc = h_vmem.shape[0] // rpt - 1
    n_assign = tc * TOP_K
    depth = wg_ring.shape[0]

    def weight_copies(j):
        e, slot = uexp_ref[j], j % depth
        return [pltpu.make_async_copy(hbm.at[e], ring.at[slot], wsem.at[n, slot])
                for n, (hbm, ring) in enumerate(((wg_hbm, wg_ring), (wu_hbm, wu_ring), (wd_hbm, wd_ring)))]

    def start_weights(j):
        @pl.when(j < nused_ref[0])
        def _():
            for cp in weight_copies(j):
                cp.start()

    def block_rows(blk):
        bc = jnp.clip(blk, 0, nblk - 1)
        return row0_ref[bc], nvalid_ref[bc]

    def assignment(rows, r, checked=True):
        row0, nv = rows
        a = skey_ref[row0 + r] & (n_assign - 1)
        return jnp.where(r < nv, a, n_assign) if checked else a

    def tile_of(a):
        off = (a & ~(TOP_K - 1)) if rpt == TOP_K else (a >> TOP_K_SHIFT) * rpt
        return pl.ds(pl.multiple_of(off, rpt), rpt)

    def gather(blk, xg, r0=0, r1=MOE_BLOCK):
        rows = block_rows(blk)
        for r in range(r0, r1):
            xg[r * rpt:(r + 1) * rpt, :] = h_vmem[tile_of(assignment(rows, r, checked=False)), :]

    def scatter(blk, y, r0=0, r1=MOE_BLOCK):
        blk_rows = block_rows(blk)
        for g0 in range(r0, r1, SCATTER_GROUP):
            updated = []
            for r in range(g0, g0 + SCATTER_GROUP):
                a = assignment(blk_rows, r)
                rows = tile_of(a)
                updated.append((rows, acc_ref[rows, :] + gflat_ref[a] * y[r * rpt:(r + 1) * rpt, :]))
            for rows, v in updated:
                acc_ref[rows, :] = v

    @pl.when(b == 0)
    def _():
        cp = pltpu.make_async_copy(h_hbm, h_vmem.at[pl.ds(0, tc * rpt)], sem)
        cp.start()
        for j in range(depth - 1):
            start_weights(j)
        h_vmem[pl.ds(tc * rpt, rpt), :] = jnp.zeros((rpt, LANES), F32)
        acc_ref[...] = jnp.zeros(acc_ref.shape, F32)
        y0[...] = jnp.zeros(y0.shape, F32)
        y1[...] = jnp.zeros(y1.shape, F32)
        cp.wait()
        gather(0, xg0)

    cur = ord_ref[jnp.minimum(b, nblk - 1)]

    @pl.when(jnp.logical_and(b < nblk, jnp.logical_or(b == 0, ord_ref[jnp.maximum(b, 1) - 1] != cur)))
    def _():
        for cp in weight_copies(cur):
            cp.wait()
        start_weights(cur + depth - 1)

    def step(xg_cur, xg_nxt, y_cur, y_prv):
        slot = cur % depth
        half = MOE_BLOCK // 2
        xb = jnp.concatenate([xg_cur[pl.ds(j, MOE_BLOCK, stride=rpt), :] for j in range(rpt)], axis=1).astype(BF16)
        gt = _dot(xb, wg_ring[slot].astype(BF16))
        up = _dot(xb, wu_ring[slot].astype(BF16))
        gather(b + 1, xg_nxt, 0, half)
        scatter(b - 1, y_prv, 0, half)
        y = _dot((_silu(gt) * up).astype(BF16), wd_ring[slot].astype(BF16))
        gather(b + 1, xg_nxt, half, MOE_BLOCK)
        scatter(b - 1, y_prv, half, MOE_BLOCK)
        for j in range(rpt):
            y_cur[pl.ds(j, MOE_BLOCK, stride=rpt), :] = y[:, j * LANES:(j + 1) * LANES]

    @pl.when(jnp.logical_and(b <= nblk, (b & 1) == 0))
    def _():
        step(xg0, xg1, y0, y1)

    @pl.when(jnp.logical_and(b <= nblk, (b & 1) == 1))
    def _():
        step(xg1, xg0, y1, y0)

    @pl.when(b == last_step)
    def _():
        cp = pltpu.make_async_copy(acc_ref.at[pl.ds(0, tc * rpt)], o_hbm, sem)
        cp.start()
        cp.wait()


def _routed_experts(h_tiles, batch, blk_ord, uexp, nused, row0, nvalid, nblk, skey, gflat, w_gate, w_up, w_down):
    D, ff = w_gate.shape[1], w_gate.shape[2]
    rpt = D // LANES
    tc = h_tiles.shape[1] // rpt
    nb = blk_ord.shape[0]
    n_prefetch = 8
    hbm = pl.BlockSpec(memory_space=pl.ANY)
    rows = pltpu.VMEM((MOE_BLOCK * rpt, LANES), F32)
    resident = pltpu.VMEM(((tc + 1) * rpt, LANES), F32)
    grid_spec = pltpu.PrefetchScalarGridSpec(
        num_scalar_prefetch=n_prefetch, grid=(nb + 1,),
        in_specs=[hbm, hbm, hbm, hbm],
        out_specs=hbm,
        scratch_shapes=[resident, resident, rows, rows, rows, rows,
                        pltpu.VMEM((WEIGHT_RING, D, ff), F32), pltpu.VMEM((WEIGHT_RING, D, ff), F32),
                        pltpu.VMEM((WEIGHT_RING, ff, D), F32),
                        pltpu.SemaphoreType.DMA(()), pltpu.SemaphoreType.DMA((3, WEIGHT_RING))],
    )
    return pl.pallas_call(
        functools.partial(_experts_kernel, batch=batch), grid_spec=grid_spec,
        out_shape=jax.ShapeDtypeStruct(h_tiles.shape, F32),
        input_output_aliases={n_prefetch: 0},
        compiler_params=_cparams(("arbitrary",), vmem=56 * 1024 * 1024),
        name="moe_routed_experts",
    )(blk_ord, uexp, nused, row0, nvalid, nblk, skey, gflat, h_tiles, w_gate, w_up, w_down)


def _dispatch_tables(eidx, gates, counts, B, S):
    n_assign = S * TOP_K
    assert n_assign & (n_assign - 1) == 0
    e = eidx.reshape(TOP_K, B, S)
    t = lax.broadcasted_iota(I32, e.shape, 2)
    k = lax.broadcasted_iota(I32, e.shape, 0)
    keys = (e * n_assign + t * TOP_K + k).transpose(1, 0, 2).reshape(B, n_assign)
    spare = jnp.zeros((MOE_BLOCK,), I32)
    skey = jnp.stack([jnp.concatenate([lax.sort(keys[b]), spare]) for b in range(B)], axis=0)
    seg = jnp.cumsum(counts, axis=1) - counts
    nblocks_e = (counts + MOE_BLOCK - 1) // MOE_BLOCK
    bend = jnp.cumsum(nblocks_e, axis=1)
    bstart = bend - nblocks_e
    nb = n_assign // MOE_BLOCK + N_EXPERTS
    bidx = jnp.arange(nb, dtype=I32)
    blk_e = jnp.minimum(jnp.sum((bend[:, None, :] <= bidx[None, :, None]).astype(I32), axis=-1), N_EXPERTS - 1)
    onehot = (blk_e[:, :, None] == jnp.arange(N_EXPERTS, dtype=I32)[None, None, :]).astype(I32)
    pick = lambda v: jnp.sum(onehot * v[:, None, :], axis=-1)
    off = (bidx[None, :] - pick(bstart)) * MOE_BLOCK
    row0 = pick(seg) + off
    nvalid = jnp.clip(pick(counts) - off, 0, MOE_BLOCK)
    used = (counts > 0).astype(I32)
    rank = jnp.cumsum(used, axis=1) - 1
    blk_ord = pick(rank)
    slots = jnp.arange(N_EXPERTS, dtype=I32)
    uexp = jnp.sum(jnp.where((rank[:, None, :] == slots[None, :, None]) & (used[:, None, :] > 0), slots[None, None, :], 0),
                   axis=-1)
    gflat = gates.reshape(TOP_K, B, S).transpose(1, 2, 0).reshape(B, n_assign)
    gflat = jnp.concatenate([gflat, jnp.zeros((B, TOP_K), F32)], axis=1)
    return blk_ord, uexp, jnp.sum(used, axis=1, keepdims=True), row0, nvalid, bend[:, -1:], skey, gflat


def _shared_kernel(h_ref, wg_ref, wu_ref, wd_ref, routed_ref, x1_ref, gt_ref, o_ref):
    hb = h_ref[0]
    act = _silu(_dot(hb, wg_ref[...])) * _dot(hb, wu_ref[...])
    shared = _dot(act.astype(BF16), wd_ref[...])
    o_ref[0] = x1_ref[0] + gt_ref[0] * (_from_token_tiles(routed_ref, hb.shape[0]) + shared)


def _shared_and_residual(h2, wsg, wsu, wsd, routed, x1, gt2, tm):
    B, S, D = h2.shape
    row = lambda b, i: (b, i, 0)
    xs = pl.BlockSpec((1, tm, D), row)
    const2 = lambda a: pl.BlockSpec(a.shape, lambda b, i: (0, 0))
    return pl.pallas_call(
        _shared_kernel, grid=(B, S // tm),
        in_specs=[xs, const2(wsg), const2(wsu), const2(wsd), pl.BlockSpec((1, tm * (D // LANES), LANES), row), xs,
                  pl.BlockSpec((1, 1, D), lambda b, i: (b, 0, 0))],
        out_specs=xs,
        out_shape=jax.ShapeDtypeStruct((B, S, D), F32),
        compiler_params=_cparams(("parallel", "parallel")),
        name="shared_expert_residual",
    )(h2, wsg, wsu, wsd, routed, x1, gt2)


def _prep_in_weights(w_in, g_q_dil, g_k_dil, g_q_nsa, g_k_slc, g_k_win):
    D = w_in.shape[0]
    pad = jnp.zeros((D, IN_COLS_PADDED - w_in.shape[1]), w_in.dtype)
    w_all = jnp.concatenate([w_in, pad], axis=1).astype(BF16)
    one = jnp.ones((LANES,), F32)
    gcol = jnp.concatenate([
        jnp.tile(g_q_dil, NH_DIL), jnp.tile(g_k_dil, NH_DIL), jnp.ones((512,), F32), jnp.tile(g_q_nsa, NH_NSA),
        one, one, jnp.tile(g_k_slc, NKV_NSA), one, jnp.tile(g_k_win, NKV_NSA), one, one]).reshape(1, IN_COLS_PADDED)
    return w_all, gcol


def _rope_tables(S):
    inv_freq = ROPE_THETA ** (-jnp.arange(ROT_HALF, dtype=F32) / ROT_HALF)
    ang = jnp.arange(S).astype(F32)[:, None] * inv_freq[None, :]
    cos, sin = jnp.cos(ang), jnp.sin(ang)
    zeros = jnp.zeros((S, HD - ROT_DIM), F32)
    z8 = jnp.zeros((S, ROT_HALF), F32)
    cos_h = jnp.concatenate([cos, cos, jnp.ones((S, HD - ROT_DIM), F32)], axis=1)
    sa_h = jnp.concatenate([-sin, z8, zeros], axis=1)
    sb_h = jnp.concatenate([z8, sin, zeros], axis=1)
    two = lambda t: jnp.concatenate([t, t], axis=1)
    return two(cos_h), two(sa_h), two(sb_h)


def _prep_compress(pe, w1, w2):
    eye = jnp.eye(NKV_NSA, dtype=F32)
    w1r = w1.reshape(CMP_LEN, HD, CMP_HIDDEN)

    def half(w1h, peh):
        w = jnp.einsum("ldh,gk->lgdkh", w1h, eye).reshape(CMP_STRIDE * LANES, NKV_NSA * CMP_HIDDEN)
        p = jnp.broadcast_to(peh[:, None, :], (CMP_STRIDE, NKV_NSA, HD)).reshape(1, CMP_STRIDE * LANES)
        return p, w.astype(BF16)

    pa, wa = half(w1r[:CMP_STRIDE], pe[:CMP_STRIDE])
    pb, wb = half(w1r[CMP_STRIDE:], pe[CMP_STRIDE:])
    w2bd = jnp.einsum("hd,gk->ghkd", w2, eye).reshape(NKV_NSA * CMP_HIDDEN, LANES).astype(BF16)
    return pa, pb, wa, wb, w2bd


def _overlap_tables(S):
    ncb = S // CMP_STRIDE
    n_sel = S // SEL_BLOCK
    cs = np.arange(ncb) * CMP_STRIDE
    ss = np.arange(n_sel) * SEL_BLOCK
    ov = np.clip(np.minimum(cs[:, None] + CMP_LEN, ss[None, :] + SEL_BLOCK) - np.maximum(cs[:, None], ss[None, :]), 0, None)
    ov = ov.astype(np.float32) / CMP_STRIDE
    ovl = np.zeros((ncb, LANES), np.float32)
    ovr = np.zeros((ncb, LANES), np.float32)
    ovl[:, :n_sel] = ov
    ovr[:, HD:HD + n_sel] = ov
    return jnp.asarray(ovl, BF16), jnp.asarray(ovr, BF16)


def _block_ones():
    r = np.arange(LANES)
    return jnp.asarray((r[:, None] // HD == r[None, :] // HD).astype(np.float32), BF16)


def _layer(x, c, w_ada, b_ada, g_norm_mix, g_norm_ffn, w_in, g_q_dil, g_k_dil, g_q_nsa, g_k_cmp, g_k_slc, g_k_win,
           cmp_pe_k, cmp_w1_k, cmp_w2_k, cmp_pe_v, cmp_w1_v, cmp_w2_v, w_out, w_router, router_bias,
           w_gate, w_up, w_down, ws_gate, ws_up, ws_down):
    B, S, D = x.shape
    ones_bd = _block_ones()

    c_pad = jnp.zeros((8, D), F32).at[:B].set(c)
    mod = _ada_mod(c_pad, w_ada, b_ada.reshape(1, -1))[:B]
    sh1, sc1, gt1, sh2, sc2, gt2 = [m.reshape(B, 1, D) for m in jnp.split(mod, 6, axis=-1)]

    w_all, gcol = _prep_in_weights(w_in, g_q_dil, g_k_dil, g_q_nsa, g_k_slc, g_k_win)
    cos_t, sa_t, sb_t = _rope_tables(S)
    qa, ka, va, qn, qr, kc, vc, ksl, vsl, kw, vw, gate = _in_projection(
        x, g_norm_mix.reshape(1, D), sc1, sh1, w_all, gcol, cos_t, sa_t, sb_t, ones_bd, tm=512)

    o_a = _dilated_attention(qa, ka, va)

    kcmp, vcmp = _compress(kc, vc, _prep_compress(cmp_pe_k, cmp_w1_k, cmp_w2_k),
                           _prep_compress(cmp_pe_v, cmp_w1_v, cmp_w2_v), jnp.tile(g_k_cmp, NKV_NSA).reshape(1, LANES),
                           ones_bd)
    ovl, ovr = _overlap_tables(S)
    o_cmp, sel = _cmp_select(qn, kcmp, vcmp, ovl, ovr, gate, tq=128)
    o_cs = _nsa_branch(qr, ksl, vsl, sel, o_cmp, gate, tq=256, tk=512, mode="selected", out_dtype=F32)
    o_b = _nsa_branch(qr, kw, vw, None, o_cs, gate, tq=256, tk=256, mode="window", out_dtype=BF16)

    w_out_b = w_out.astype(BF16)
    x1, h2, h2_tiles = _out_projection(o_a, o_b, w_out_b[:NH_DIL * HD], w_out_b[NH_DIL * HD:], x, gt1,
                                       g_norm_ffn.reshape(1, D), sc2, sh2, tm=256)

    tm_r = 256
    eidx, gates, counts = _router(h2, w_router.T.astype(BF16),
                                  jnp.broadcast_to(router_bias.reshape(N_EXPERTS, 1), (N_EXPERTS, tm_r)), tm=tm_r)
    tables = _dispatch_tables(eidx, gates, counts[:, :, 0].astype(I32), B, S)
    routed = h2_tiles
    for b in range(B):
        routed = _routed_experts(routed, b, *[tbl[b] for tbl in tables], w_gate, w_up, w_down)
    return _shared_and_residual(h2, ws_gate.astype(BF16), ws_up.astype(BF16), ws_down.astype(BF16), routed, x1, gt2,
                                tm=256)


def kernel(x, c, w_ada, b_ada, g_norm_mix, g_norm_ffn, w_in, g_q_dil, g_k_dil, g_q_nsa, g_k_cmp, g_k_slc, g_k_win, cmp_pe_k, cmp_w1_k, cmp_w2_k, cmp_pe_v, cmp_w1_v, cmp_w2_v, w_out, w_router, router_bias, w_gate, w_up, w_down, ws_gate, ws_up, ws_down):
    params = (w_ada, b_ada, g_norm_mix, g_norm_ffn, w_in, g_q_dil, g_k_dil, g_q_nsa, g_k_cmp, g_k_slc, g_k_win,
              cmp_pe_k, cmp_w1_k, cmp_w2_k, cmp_pe_v, cmp_w1_v, cmp_w2_v, w_out, w_router, router_bias,
              w_gate, w_up, w_down, ws_gate, ws_up, ws_down)
    for layer in range(w_ada.shape[0]):
        x = _layer(x, c, *[a[layer] for a in params])
    return x
```

```python
import functools
import math

import numpy as np
import jax
import jax.numpy as jnp
from jax import lax
from jax.experimental import pallas as pl
from jax.experimental.pallas import tpu as pltpu

F32 = jnp.float32
BF16 = jnp.bfloat16
I32 = jnp.int32

HD = 64
LANES = 128
NH_DIL = 8
NH_NSA = 8
NKV_NSA = 2
NSA_REP = NH_NSA // NKV_NSA
DIL_PAIRS = ((128, 1), (512, 4), (2048, 16))
DIL_TILE = 128
DIL_GROUP = 8
ROPE_THETA = 500000.0
ROT_DIM = HD // 4
ROT_HALF = ROT_DIM // 2
CMP_LEN = 32
CMP_STRIDE = 16
CMP_HIDDEN = 128
SEL_BLOCK = 64
SEL_SHIFT = 6
SEL_TOP = 16
NSA_WINDOW = 512
N_EXPERTS = 256
TOP_K = 8
TOP_K_SHIFT = 3
N_GROUPS = 8
GROUP_SIZE = N_EXPERTS // N_GROUPS
TOPK_GROUPS = 4
EXPERT_FF = 256
SHARED_FF = 256
ROUTED_SCALE = 2.5
EPS = 1e-6
QK_SCALE = 1.0 / math.sqrt(HD)

NEG = -1e30
BIG = 3e38

C_QA, C_KA, C_VA, C_QB = 0, 512, 1024, 1536
C_KC, C_VC, C_KSL, C_VSL, C_KW, C_VW, C_GB = 2048, 2176, 2304, 2432, 2560, 2688, 2816
IN_COLS_PADDED = 2944
TRANSPOSED_OUTS = (8, 10)

MOE_BLOCK = 128
NULL_ROWS = 8
WEIGHT_RING = 5
SCATTER_GROUP = 16
VMEM_LIMIT = 48 * 1024 * 1024


def _cparams(sem, vmem=VMEM_LIMIT):
    return pltpu.CompilerParams(dimension_semantics=sem, vmem_limit_bytes=vmem)


def _sigmoid(v):
    return 1.0 / (1.0 + jnp.exp(-v))


def _silu(v):
    return v * _sigmoid(v)


def _dot(a, b):
    return jnp.dot(a, b, preferred_element_type=F32)


def _dot_nt(a, b):
    return lax.dot_general(a, b, (((1,), (1,)), ((), ())), preferred_element_type=F32)


def _lane_lo(shape):
    return lax.broadcasted_iota(I32, shape, len(shape) - 1) < HD


def _head_sums(v, ones_bd):
    hi = v.astype(BF16)
    lo = (v - hi.astype(F32)).astype(BF16)
    return _dot(hi, ones_bd) + _dot(lo, ones_bd)


def _head_rmsnorm(y, ones_bd, gain):
    ms = _head_sums(y * y, ones_bd) * (1.0 / HD)
    return y * lax.rsqrt(ms + EPS) * gain


def _rope(y, cos, sin_a, sin_b):
    return y * cos + pltpu.roll(y, LANES - ROT_HALF, 1) * sin_a + pltpu.roll(y, ROT_HALF, 1) * sin_b


def _ada_kernel(c_ref, w_ref, b_ref, o_ref):
    a = _silu(c_ref[...]).astype(BF16)
    o_ref[...] = _dot(a, w_ref[...].astype(BF16)) + b_ref[...]


def _ada_mod(c_pad, w_ada, b_ada):
    rows, d = c_pad.shape
    n = w_ada.shape[1]
    tn = 1536 if n % 1536 == 0 else n
    return pl.pallas_call(
        _ada_kernel,
        grid=(n // tn,),
        in_specs=[pl.BlockSpec((rows, d), lambda j: (0, 0)),
                  pl.BlockSpec((d, tn), lambda j: (0, j)),
                  pl.BlockSpec((1, tn), lambda j: (0, j))],
        out_specs=pl.BlockSpec((rows, tn), lambda j: (0, j)),
        out_shape=jax.ShapeDtypeStruct((rows, n), F32),
        compiler_params=_cparams(("arbitrary",)),
        name="ada_mod",
    )(c_pad, w_ada, b_ada)


def _inproj_kernel(x_ref, g_ref, sc_ref, sh_ref, w_ref, gcol_ref, cos_ref, sa_ref, sb_ref, ones_ref,
                   qa_ref, ka_ref, va_ref, qn_ref, qr_ref, kc_ref, vc_ref, ksl_ref, vsl_ref, kw_ref, vw_ref,
                   gate_ref):
    x = x_ref[0]
    ms = jnp.mean(x * x, axis=-1, keepdims=True)
    h = (x * lax.rsqrt(ms + EPS)) * g_ref[...]
    h = h * (1.0 + sc_ref[0]) + sh_ref[0]
    hb = h.astype(BF16)
    ones_bd = ones_ref[...]
    cos, sa, sb = cos_ref[...], sa_ref[...], sb_ref[...]
    lo = _lane_lo(cos.shape)

    def proj(c0, width):
        return _dot(hb, w_ref[:, c0:c0 + width])

    def normed(tile, c0):
        return _head_rmsnorm(tile, ones_bd, gcol_ref[:, c0:c0 + LANES])

    acc = proj(C_QA, 512)
    for p in range(4):
        y = _rope(normed(acc[:, p * LANES:(p + 1) * LANES], C_QA + p * LANES), cos, sa, sb) * QK_SCALE
        qa_ref[0, :, p * LANES:(p + 1) * LANES] = y
    acc = proj(C_KA, 512)
    for p in range(4):
        y = _rope(normed(acc[:, p * LANES:(p + 1) * LANES], C_KA + p * LANES), cos, sa, sb)
        ka_ref[0, :, p * LANES:(p + 1) * LANES] = y
    va_ref[0] = proj(C_VA, 512)

    acc = proj(C_QB, 512)
    for p in range(4):
        yn = normed(acc[:, p * LANES:(p + 1) * LANES], C_QB + p * LANES)
        yr = _rope(yn, cos, sa, sb)
        for y, ref in ((yn * QK_SCALE, qn_ref), (yr * QK_SCALE, qr_ref)):
            ysw = pltpu.roll(y, HD, 1)
            for half in range(2):
                head = 2 * p + half
                grp = head // NSA_REP
                src = y if grp == half else ysw
                keep = lo if grp == 0 else jnp.logical_not(lo)
                ref[0, :, head * LANES:(head + 1) * LANES] = jnp.where(keep, src, 0.0).astype(BF16)

    acc = proj(C_KC, 512)
    kc_ref[0] = acc[:, 0:LANES]
    vc_ref[0] = acc[:, LANES:2 * LANES]
    ksl_ref[0] = _rope(normed(acc[:, 2 * LANES:3 * LANES], C_KSL), cos, sa, sb).astype(BF16)
    vsl_ref[0] = _transposed(acc[:, 3 * LANES:4 * LANES]).astype(BF16)
    acc = proj(C_KW, 384)
    kw_ref[0] = _rope(normed(acc[:, 0:LANES], C_KW), cos, sa, sb).astype(BF16)
    vw_ref[0] = _transposed(acc[:, LANES:2 * LANES]).astype(BF16)
    gate_ref[0] = _sigmoid(acc[:, 2 * LANES:3 * LANES])


def _transposed(tile):
    return jnp.concatenate([tile[r0:r0 + LANES].T for r0 in range(0, tile.shape[0], LANES)], axis=1)


def _in_projection(x, g_mix, sc1, sh1, w_all, gcol, cos_t, sa_t, sb_t, ones_bd, tm):
    B, S, D = x.shape
    nc = w_all.shape[1]
    row = lambda b, i: (b, i, 0)
    bvec = pl.BlockSpec((1, 1, D), lambda b, i: (b, 0, 0))
    tab = pl.BlockSpec((tm, LANES), lambda b, i: (i, 0))
    const2 = lambda shape: pl.BlockSpec(shape, lambda b, i: (0, 0))
    widths_dt = [(512, F32), (512, F32), (512, F32), (1024, BF16), (1024, BF16), (LANES, F32), (LANES, F32),
                 (LANES, BF16), (LANES, BF16), (LANES, BF16), (LANES, BF16), (LANES, F32)]
    return pl.pallas_call(
        _inproj_kernel,
        grid=(B, S // tm),
        in_specs=[pl.BlockSpec((1, tm, D), row), const2((1, D)), bvec, bvec, const2((D, nc)), const2((1, nc)),
                  tab, tab, tab, const2((LANES, LANES))],
        out_specs=[pl.BlockSpec((1, tm, w), row) if n not in TRANSPOSED_OUTS else pl.BlockSpec((1, w, tm), lambda b, i: (b, 0, i))
                   for n, (w, _) in enumerate(widths_dt)],
        out_shape=[jax.ShapeDtypeStruct((B, S, w) if n not in TRANSPOSED_OUTS else (B, w, S), dt)
                   for n, (w, dt) in enumerate(widths_dt)],
        compiler_params=_cparams(("parallel", "parallel")),
        name="in_projection",
    )(x, g_mix, sc1, sh1, w_all, gcol, cos_t, sa_t, sb_t, ones_bd)


def _dot_tn(a, b):
    return lax.dot_general(a, b, (((0,), (0,)), ((), ())), preferred_element_type=F32)


def _dilated_kernel(q_ref, k_ref, v_ref, o_ref, oacc, lacc, *, pairs):
    seq = q_ref.shape[1]
    t = DIL_TILE
    kk = lax.broadcasted_iota(I32, (t, t), 0)
    qq = lax.broadcasted_iota(I32, (t, t), 1)
    bias_cur = jnp.where(qq >= kk, 0.0, NEG)
    bias_prev = jnp.where(qq <= kk, 0.0, NEG)
    bias_none = jnp.full((t, t), NEG, F32)
    lane_lo = _lane_lo((t, LANES))
    row_lo = lax.broadcasted_iota(I32, (LANES, t), 0) < HD

    def blocks(d, starts, mode):
        def strided(ref, start):
            return (ref[0, pl.ds(start, t, stride=d), :] if d > 1 else ref[0, pl.ds(start, t), :]).astype(BF16)

        work, last = [], None
        for qs, first, chained in starts:
            rows_q = pl.ds(qs, t, stride=d) if d > 1 else pl.ds(qs, t)
            qt = q_ref[0, rows_q, :].astype(BF16)
            zero = jnp.zeros_like(qt)
            q2 = jnp.concatenate([jnp.where(lane_lo, qt, zero), jnp.where(lane_lo, zero, qt)], axis=0)
            cur = (strided(k_ref, qs), strided(v_ref, qs))
            if chained:
                prev, bias_p = last, bias_prev
            elif first is True:
                prev, bias_p = cur, bias_none
            else:
                prev_start = jnp.where(first, qs, qs - t * d)
                prev, bias_p = (strided(k_ref, prev_start), strided(v_ref, prev_start)), jnp.where(first, NEG, bias_prev)
            last = cur
            k2 = jnp.concatenate([prev[0], cur[0]], axis=0)
            v2 = jnp.concatenate([prev[1], cur[1]], axis=0)
            bias = jnp.concatenate([bias_p, bias_cur], axis=0)
            s = _dot_nt(k2, q2) + jnp.concatenate([bias, bias], axis=1)
            work.append([rows_q, v2, s])
        for w in work:
            s = w[2]
            m = s.max(axis=0, keepdims=True)
            p = jnp.exp(s - m)
            l = p.sum(axis=0, keepdims=True)
            w[2:] = [p.astype(BF16), l, m + jnp.log(l)]
        for w in work:
            w[1] = _dot_tn(w[1], w[2]) / w[3]
        for w in work:
            ov, lse2 = w[1], w[4]
            w[1:] = [jnp.where(row_lo, ov[:, :t], ov[:, t:]).T, jnp.where(row_lo, lse2[:, :t], lse2[:, t:]).T]
        for rows_q, o, lse in work:
            if mode == "init":
                oacc[rows_q, :] = o
                lacc[rows_q, :] = lse
                continue
            lp = lacc[rows_q, :]
            mx = jnp.maximum(lp, lse)
            wp, wn = jnp.exp(lp - mx), jnp.exp(lse - mx)
            den = wp + wn
            oacc[rows_q, :] = (oacc[rows_q, :] * wp + o * wn) / den
            if mode == "mid":
                lacc[rows_q, :] = mx + jnp.log(den)

    modes = ("init",) + ("mid",) * (len(pairs) - 2) + ("last",)
    for (window, d), mode in zip(pairs, modes):
        assert window // d == t and d & (d - 1) == 0
        n_j = seq // (t * d)
        assert n_j & (n_j - 1) == 0 and (seq // t) % DIL_GROUP == 0

        def body(i, c, d=d, mode=mode, n_j=n_j):
            starts = []
            for u in range(DIL_GROUP):
                n = i * DIL_GROUP + u
                r, j = n >> (n_j.bit_length() - 1), n & (n_j - 1)
                if n_j <= DIL_GROUP:
                    first, chained = (u % n_j == 0), (u % n_j != 0)
                else:
                    first, chained = (j == 0), (u != 0)
                starts.append((j * (t * d) + r, first, chained))
            blocks(d, starts, mode)
            return c

        lax.fori_loop(0, seq // t // DIL_GROUP, body, 0)
    o_ref[0] = oacc[...].astype(o_ref.dtype)


def _dilated_attention(qa, ka, va):
    B, S, w = qa.shape
    assert S % (DIL_TILE * max(d for _, d in DIL_PAIRS)) == 0
    kern = functools.partial(_dilated_kernel, pairs=DIL_PAIRS)
    whole = pl.BlockSpec((1, S, LANES), lambda b, p: (b, 0, p))
    acc = pltpu.VMEM((S, LANES), F32)
    return pl.pallas_call(
        kern, grid=(B, w // LANES),
        in_specs=[whole, whole, whole], out_specs=whole,
        out_shape=jax.ShapeDtypeStruct((B, S, w), BF16),
        scratch_shapes=[acc, acc],
        compiler_params=_cparams(("parallel", "parallel")),
        name="dilated_attention",
    )(qa, ka, va)


def _compress_kernel(kc_ref, vc_ref, pak_ref, pbk_ref, wak_ref, wbk_ref, w2k_ref, gk_ref, ones_ref,
                     pav_ref, pbv_ref, wav_ref, wbv_ref, w2v_ref, ko_ref, vo_ref):
    ncb = kc_ref.shape[1]

    def branch(t_ref, pa, pb, wa, wb, w2):
        t = t_ref[0]
        ua = _dot((t + pa[...]).astype(BF16), wa[...])
        ub = _dot((t + pb[...]).astype(BF16), wb[...])
        pre = ua + pltpu.roll(ub, ncb - 1, 0)
        return _dot(jax.nn.gelu(pre).astype(BF16), w2[...])

    live = lax.broadcasted_iota(I32, (ncb, LANES), 0) < ncb - 1
    kcmp = _head_rmsnorm(branch(kc_ref, pak_ref, pbk_ref, wak_ref, wbk_ref, w2k_ref), ones_ref[...], gk_ref[...])
    ko_ref[0] = jnp.where(live, kcmp, 0.0).astype(BF16)
    vo_ref[0] = jnp.where(live, branch(vc_ref, pav_ref, pbv_ref, wav_ref, wbv_ref, w2v_ref), 0.0).astype(BF16)


def _compress(kc, vc, wk, wv, gk2, ones_bd):
    B, S, _ = kc.shape
    ncb = S // CMP_STRIDE
    cw = CMP_STRIDE * LANES
    view = lambda a: a.reshape(B, ncb, cw)
    blk = pl.BlockSpec((1, ncb, cw), lambda b: (b, 0, 0))
    full = lambda a: pl.BlockSpec(a.shape, lambda b: (0,) * a.ndim)
    consts = [*wk, gk2, ones_bd, *wv]
    o_spec = pl.BlockSpec((1, ncb, LANES), lambda b: (b, 0, 0))
    return pl.pallas_call(
        _compress_kernel, grid=(B,),
        in_specs=[blk, blk] + [full(a) for a in consts],
        out_specs=[o_spec, o_spec],
        out_shape=[jax.ShapeDtypeStruct((B, ncb, LANES), BF16)] * 2,
        compiler_params=_cparams(("parallel",)),
        name="nsa_compress",
    )(view(kc), view(vc), *consts)


def _pair_tile(lo, t0, half0, t1, half1):
    a = t0 if half0 == 0 else pltpu.roll(t0, HD, 1)
    b = t1 if half1 == 1 else pltpu.roll(t1, HD, 1)
    return jnp.where(lo, a, b)


def _cmpsel_kernel(qn_ref, kc_ref, vc_ref, ovl_ref, ovr_ref, gate_ref, o_ref, sel_ref, *, tq, n_sel, top):
    ncb = kc_ref.shape[1]
    i = pl.program_id(1)
    pos0 = i * tq
    qpos = pos0 + lax.broadcasted_iota(I32, (tq, ncb), 0)
    cend = lax.broadcasted_iota(I32, (tq, ncb), 1) * CMP_STRIDE + (CMP_LEN - 1)
    valid = cend <= qpos
    lo = _lane_lo((tq, LANES))
    kc = kc_ref[0]
    vc = vc_ref[0]

    psum = [None] * NKV_NSA
    scores = [_dot_nt(qn_ref[0, :, h * LANES:(h + 1) * LANES], kc) for h in range(NH_NSA)]
    probs = []
    for h, s in enumerate(scores):
        g = h // NSA_REP
        m = jnp.where(valid, s, NEG).max(axis=1, keepdims=True)
        m = jnp.where(m > 0.5 * NEG, m, 0.0)
        p = jnp.where(valid, jnp.exp(s - m), 0.0)
        l = p.sum(axis=1, keepdims=True)
        p = p / jnp.where(l > 0.0, l, 1.0)
        probs.append(p.astype(BF16))
        psum[g] = p if psum[g] is None else psum[g] + p
    outs = [_dot(p, vc) for p in probs]

    for p_ in range(NH_NSA // 2):
        h0, h1 = 2 * p_, 2 * p_ + 1
        o_pair = _pair_tile(lo, outs[h0], h0 // NSA_REP, outs[h1], h1 // NSA_REP)
        g = jnp.where(lo, gate_ref[0, :, 3 * h0:3 * h0 + 1], gate_ref[0, :, 3 * h1:3 * h1 + 1])
        o_ref[0, :, p_ * LANES:(p_ + 1) * LANES] = g * o_pair

    def hilo_dot(a, w):
        hi = a.astype(BF16)
        return _dot(hi, w) + _dot((a - hi.astype(F32)).astype(BF16), w)

    pslc = hilo_dot(psum[0], ovl_ref[...]) + hilo_dot(psum[1], ovr_ref[...])
    pslc_t = pslc.T
    jblk = lax.broadcasted_iota(I32, (HD, tq), 0)
    cur = (pos0 + lax.broadcasted_iota(I32, (HD, tq), 1)) >> SEL_SHIFT
    forced = (jblk == 0) | (jblk == cur) | (jblk == cur - 1)
    sel_t = []
    for g in range(NKV_NSA):
        sc = jnp.where(forced, BIG, jnp.where(jblk <= cur, pslc_t[g * HD:(g + 1) * HD, :], -BIG))
        rank = jnp.zeros((HD, tq), F32)
        for k in range(n_sel):
            rk = sc[k:k + 1, :]
            rank = rank + jnp.where(jblk > k, jnp.where(rk >= sc, 1.0, 0.0), jnp.where(rk > sc, 1.0, 0.0))
        sel_t.append(jnp.where(rank < top, 1.0, 0.0))
    sel_ref[0] = jnp.concatenate(sel_t, axis=0)


def _cmp_select(qn, kcmp, vcmp, ovl, ovr, gate, tq):
    B, S, qw = qn.shape
    ncb = kcmp.shape[1]
    n_sel = S // SEL_BLOCK
    assert n_sel <= HD and tq == LANES
    row = lambda b, i: (b, i, 0)
    cb = pl.BlockSpec((1, ncb, LANES), lambda b, i: (b, 0, 0))
    ovs = pl.BlockSpec((ncb, LANES), lambda b, i: (0, 0))
    kern = functools.partial(_cmpsel_kernel, tq=tq, n_sel=n_sel, top=min(SEL_TOP, n_sel))
    return pl.pallas_call(
        kern, grid=(B, S // tq),
        in_specs=[pl.BlockSpec((1, tq, qw), row), cb, cb, ovs, ovs, pl.BlockSpec((1, tq, LANES), row)],
        out_specs=[pl.BlockSpec((1, tq, NH_NSA * HD), row), pl.BlockSpec((1, LANES, tq), lambda b, i: (b, 0, i))],
        out_shape=[jax.ShapeDtypeStruct((B, S, NH_NSA * HD), F32), jax.ShapeDtypeStruct((B, LANES, S), F32)],
        compiler_params=_cparams(("parallel", "parallel")),
        name="nsa_cmp_select",
    )(qn, kcmp, vcmp, ovl, ovr, gate)


def _nsa_kernel(*refs, tq, tk, mode):
    qi_ref, tile_ref, edge_ref = refs[:3]
    if mode == "selected":
        q_ref, k_ref, vt_ref, selt_ref, prev_ref, gate_ref, o_ref, m_ref, l_ref, acc_ref = refs[3:]
    else:
        q_ref, k_ref, vt_ref, prev_ref, gate_ref, o_ref, m_ref, l_ref, acc_ref = refs[3:]
    step = pl.program_id(1)
    i = qi_ref[step]
    tile = tile_ref[step]
    edge = edge_ref[step]

    @pl.when((edge & 1) != 0)
    def _():
        m_ref[...] = jnp.full(m_ref.shape, NEG, F32)
        l_ref[...] = jnp.zeros(l_ref.shape, F32)
        acc_ref[...] = jnp.zeros(acc_ref.shape, F32)

    def accumulate_tile():
        kpos = tile * tk + lax.broadcasted_iota(I32, (tk, tq), 0)
        qpos = i * tq + lax.broadcasted_iota(I32, (tk, tq), 1)
        k = k_ref[0]
        vt = vt_ref[0]
        if mode == "window":
            dist = qpos - kpos
            bias_w = jnp.where(dist >= 0, jnp.where(dist < NSA_WINDOW, 0.0, NEG), NEG)
        groups = range(NKV_NSA)
        scores = []
        for g in groups:
            if mode == "selected":
                nblk = tk // SEL_BLOCK
                rows = selt_ref[0, pl.ds(pl.multiple_of(g * HD + tile * nblk, nblk), nblk), :]
                chosen = jnp.broadcast_to(rows[:, None, :], (nblk, SEL_BLOCK, tq)).reshape(tk, tq) > 0.5
                bias = jnp.where(kpos <= qpos, jnp.where(chosen, 0.0, NEG), NEG)
            else:
                bias = bias_w
            heads = range(g * NSA_REP, (g + 1) * NSA_REP)
            q4 = jnp.concatenate([q_ref[0, :, h * LANES:(h + 1) * LANES] for h in heads], axis=0)
            scores.append(_dot_nt(k, q4) + jnp.concatenate([bias] * NSA_REP, axis=1))
        probs = []
        for g in groups:
            m_prev = m_ref[g]
            m_new = jnp.maximum(m_prev, scores[g].max(axis=0, keepdims=True))
            alpha = jnp.exp(m_prev - m_new)
            p = jnp.exp(scores[g] - m_new[0:1])
            l_ref[g] = alpha * l_ref[g] + p.sum(axis=0, keepdims=True)
            m_ref[g] = m_new
            probs.append((alpha[0:1], p.astype(BF16)))
        for g in groups:
            alpha, p = probs[g]
            acc_ref[g] = alpha * acc_ref[g] + _dot(vt, p)

    accumulate_tile()

    @pl.when((edge & 2) != 0)
    def _():
        lo = _lane_lo((tq, LANES))
        gate_col = 1 if mode == "selected" else 2

        def head_t(h):
            g, r = divmod(h, NSA_REP)
            cols = slice(r * tq, (r + 1) * tq)
            return acc_ref[g, g * HD:(g + 1) * HD, cols] / l_ref[g, 0:1, cols]

        for p_ in range(NH_NSA // 2):
            h0, h1 = 2 * p_, 2 * p_ + 1
            o_pair = jnp.concatenate([head_t(h0), head_t(h1)], axis=0).T
            c0, c1 = 3 * h0 + gate_col, 3 * h1 + gate_col
            gt = jnp.where(lo, gate_ref[0, :, c0:c0 + 1], gate_ref[0, :, c1:c1 + 1])
            sl = slice(p_ * LANES, (p_ + 1) * LANES)
            o_ref[0, :, sl] = (prev_ref[0, :, sl] + gt * o_pair).astype(o_ref.dtype)


def _nsa_branch(qr, k, v_t, sel_t, prev, gate, *, tq, tk, mode, out_dtype):
    B, S, qw = qr.shape
    ow = NH_NSA * HD
    steps = []
    for i in range(S // tq):
        hi = (i * tq + tq - 1) // tk
        lo = 0 if mode == "selected" else max((i * tq - (NSA_WINDOW - 1)) // tk, 0)
        steps += [(i, t, (t == lo) + 2 * (t == hi)) for t in range(lo, hi + 1)]
    tables = [jnp.asarray(np.array(col, np.int32)) for col in zip(*steps)]
    row = lambda b, s, qi, tl, ed: (b, qi[s], 0)
    in_specs = [pl.BlockSpec((1, tq, qw), row),
                pl.BlockSpec((1, tk, LANES), lambda b, s, qi, tl, ed: (b, tl[s], 0)),
                pl.BlockSpec((1, LANES, tk), lambda b, s, qi, tl, ed: (b, 0, tl[s]))]
    args = [qr, k, v_t]
    if mode == "selected":
        in_specs.append(pl.BlockSpec((1, LANES, tq), lambda b, s, qi, tl, ed: (b, 0, qi[s])))
        args.append(sel_t)
    in_specs += [pl.BlockSpec((1, tq, ow), row), pl.BlockSpec((1, tq, LANES), row)]
    args += [prev, gate]
    stat = pltpu.VMEM((NKV_NSA, 8, NSA_REP * tq), F32)
    grid_spec = pltpu.PrefetchScalarGridSpec(
        num_scalar_prefetch=3, grid=(B, len(steps)), in_specs=in_specs,
        out_specs=pl.BlockSpec((1, tq, ow), row),
        scratch_shapes=[stat, stat, pltpu.VMEM((NKV_NSA, LANES, NSA_REP * tq), F32)])
    return pl.pallas_call(
        functools.partial(_nsa_kernel, tq=tq, tk=tk, mode=mode), grid_spec=grid_spec,
        out_shape=jax.ShapeDtypeStruct((B, S, ow), out_dtype),
        compiler_params=_cparams(("parallel", "arbitrary")),
        name=f"nsa_{mode}_attention",
    )(*tables, *args)


def _to_token_tiles(ref, value):
    m, d = value.shape
    rpt = d // LANES
    for j in range(rpt):
        ref[0, pl.ds(j, m, stride=rpt), :] = value[:, j * LANES:(j + 1) * LANES]


def _from_token_tiles(ref, m):
    rpt = ref.shape[1] // m
    return jnp.concatenate([ref[0, pl.ds(j, m, stride=rpt), :] for j in range(rpt)], axis=1)


def _outproj_kernel(oa_ref, ob_ref, wt_ref, wb_ref, x_ref, gt_ref, g_ref, sc_ref, sh_ref, x1_ref, h2_ref, h2t_ref):
    mix = _dot(oa_ref[0], wt_ref[...]) + _dot(ob_ref[0], wb_ref[...])
    x1 = x_ref[0] + gt_ref[0] * mix
    x1_ref[0] = x1
    ms = jnp.mean(x1 * x1, axis=-1, keepdims=True)
    h = (x1 * lax.rsqrt(ms + EPS)) * g_ref[...]
    h2 = h * (1.0 + sc_ref[0]) + sh_ref[0]
    h2_ref[0] = h2.astype(h2_ref.dtype)
    _to_token_tiles(h2t_ref, h2)


def _out_projection(oa, ob, w_top, w_bot, x, gt1, g_ffn, sc2, sh2, tm):
    B, S, D = x.shape
    row = lambda b, i: (b, i, 0)
    bvec = pl.BlockSpec((1, 1, D), lambda b, i: (b, 0, 0))
    const2 = lambda a: pl.BlockSpec(a.shape, lambda b, i: (0, 0))
    xs = pl.BlockSpec((1, tm, D), row)
    return pl.pallas_call(
        _outproj_kernel, grid=(B, S // tm),
        in_specs=[pl.BlockSpec((1, tm, oa.shape[2]), row), pl.BlockSpec((1, tm, ob.shape[2]), row),
                  const2(w_top), const2(w_bot), xs, bvec, const2(g_ffn), bvec, bvec],
        out_specs=[xs, xs, pl.BlockSpec((1, tm * (D // LANES), LANES), row)],
        out_shape=[jax.ShapeDtypeStruct((B, S, D), F32), jax.ShapeDtypeStruct((B, S, D), BF16),
                   jax.ShapeDtypeStruct((B, S * (D // LANES), LANES), F32)],
        compiler_params=_cparams(("parallel", "parallel")),
        name="out_projection",
    )(oa, ob, w_top, w_bot, x, gt1, g_ffn, sc2, sh2)


def _router_kernel(h_ref, wr_ref, bias_ref, eidx_ref, gate_ref, cnt_ref):
    tm = h_ref.shape[1]
    logits = _dot_nt(wr_ref[...], h_ref[0])
    aff = _sigmoid(logits)
    biased = aff + bias_ref[...]
    b3 = biased.reshape(N_GROUPS, GROUP_SIZE, tm)
    jj = lax.broadcasted_iota(I32, (N_GROUPS, GROUP_SIZE, tm), 1).astype(F32)
    m1 = b3.max(axis=1, keepdims=True)
    i1 = jnp.where(b3 == m1, jj, 1e9).min(axis=1, keepdims=True)
    m2 = jnp.where(jj == i1, -jnp.inf, b3).max(axis=1, keepdims=True)
    gs = (m1 + m2).reshape(N_GROUPS, tm)
    gi = lax.broadcasted_iota(I32, (N_GROUPS, tm), 0)
    rank = jnp.zeros((N_GROUPS, tm), F32)
    for k in range(N_GROUPS):
        rk = gs[k:k + 1, :]
        rank = rank + jnp.where(gi > k, jnp.where(rk >= gs, 1.0, 0.0), jnp.where(rk > gs, 1.0, 0.0))
    gsel = (rank < TOPK_GROUPS).reshape(N_GROUPS, 1, tm)
    masked = jnp.where(gsel, b3, -jnp.inf).reshape(N_EXPERTS, tm)
    ee = lax.broadcasted_iota(I32, (N_EXPERTS, tm), 0).astype(F32)
    idx_rows, sel_rows = [], []
    chosen = jnp.zeros((N_EXPERTS, tm), F32)
    for _ in range(TOP_K):
        m = masked.max(axis=0, keepdims=True)
        idx = jnp.where(masked == m, ee, 1e9).min(axis=0, keepdims=True)
        hit = ee == idx
        idx_rows.append(idx)
        sel_rows.append(jnp.where(hit, aff, 0.0).sum(axis=0, keepdims=True))
        masked = jnp.where(hit, -jnp.inf, masked)
        chosen = chosen + jnp.where(hit, 1.0, 0.0)
    sel = jnp.concatenate(sel_rows, axis=0)
    eidx_ref[...] = jnp.concatenate(idx_rows, axis=0).astype(I32)
    gate_ref[...] = sel / sel.sum(axis=0, keepdims=True) * ROUTED_SCALE

    @pl.when(pl.program_id(1) == 0)
    def _():
        cnt_ref[...] = jnp.zeros(cnt_ref.shape, F32)

    cnt_ref[0] = cnt_ref[0] + chosen.sum(axis=1, keepdims=True)


def _router(h2, wr_t, bias_b, tm):
    B, S, D = h2.shape
    nt = S // tm
    o_spec = pl.BlockSpec((TOP_K, tm), lambda b, i: (0, b * nt + i))
    return pl.pallas_call(
        _router_kernel, grid=(B, nt),
        in_specs=[pl.BlockSpec((1, tm, D), lambda b, i: (b, i, 0)), pl.BlockSpec(wr_t.shape, lambda b, i: (0, 0)),
                  pl.BlockSpec(bias_b.shape, lambda b, i: (0, 0))],
        out_specs=[o_spec, o_spec, pl.BlockSpec((1, N_EXPERTS, LANES), lambda b, i: (b, 0, 0))],
        out_shape=[jax.ShapeDtypeStruct((TOP_K, B * S), I32), jax.ShapeDtypeStruct((TOP_K, B * S), F32),
                   jax.ShapeDtypeStruct((B, N_EXPERTS, LANES), F32)],
        compiler_params=_cparams(("parallel", "arbitrary")),
        name="moe_router",
    )(h2, wr_t, bias_b)


def _experts_kernel(ord_ref, uexp_ref, nused_ref, row0_ref, nvalid_ref, nblk_ref, skey_ref, gflat_ref,
                    h_all, wg_hbm, wu_hbm, wd_hbm, o_all, h_vmem, acc_ref, xg0, xg1, y0, y1, wg_ring, wu_ring, wd_ring,
                    sem, wsem, *, batch):
    h_hbm, o_hbm = h_all.at[batch], o_all.at[batch]
    grid_step = pl.program_id(0)
    last_step = pl.num_programs(0) - 1
    nblk = nblk_ref[0]
    rpt = wg_hbm.shape[1] // LANES
    tc = h_vmem.shape[0] // rpt - 1
    n_assign = tc * TOP_K
    depth = wg_ring.shape[0]

    def weight_copies(j):
        e, slot = uexp_ref[j], j % depth
        return [pltpu.make_async_copy(hbm.at[e], ring.at[slot], wsem.at[n, slot])
                for n, (hbm, ring) in enumerate(((wg_hbm, wg_ring), (wu_hbm, wu_ring), (wd_hbm, wd_ring)))]

    def start_weights(j):
        @pl.when(j < nused_ref[0])
        def _():
            for cp in weight_copies(j):
                cp.start()

    def block_rows(blk):
        bc = jnp.clip(blk, 0, nblk - 1)
        return row0_ref[bc], nvalid_ref[bc]

    def assignment(rows, r, checked=True):
        row0, nv = rows
        a = skey_ref[row0 + r] & (n_assign - 1)
        return jnp.where(r < nv, a, n_assign) if checked else a

    def tile_of(a):
        off = (a & ~(TOP_K - 1)) if rpt == TOP_K else (a >> TOP_K_SHIFT) * rpt
        return pl.ds(pl.multiple_of(off, rpt), rpt)

    def gather(blk, xg, r0=0, r1=MOE_BLOCK):
        rows = block_rows(blk)
        for r in range(r0, r1):
            xg[r * rpt:(r + 1) * rpt, :] = h_vmem[tile_of(assignment(rows, r, checked=False)), :]

    def scatter(blk, y, r0=0, r1=MOE_BLOCK):
        blk_rows = block_rows(blk)
        for g0 in range(r0, r1, SCATTER_GROUP):
            updated = []
            for r in range(g0, g0 + SCATTER_GROUP):
                a = assignment(blk_rows, r)
                rows = tile_of(a)
                updated.append((rows, acc_ref[rows, :] + gflat_ref[a] * y[r * rpt:(r + 1) * rpt, :]))
            for rows, v in updated:
                acc_ref[rows, :] = v

    @pl.when(grid_step == 0)
    def _():
        cp = pltpu.make_async_copy(h_hbm, h_vmem.at[pl.ds(0, tc * rpt)], sem)
        cp.start()
        for j in range(depth - 1):
            start_weights(j)
        h_vmem[pl.ds(tc * rpt, rpt), :] = jnp.zeros((rpt, LANES), F32)
        acc_ref[...] = jnp.zeros(acc_ref.shape, F32)
        y0[...] = jnp.zeros(y0.shape, F32)
        y1[...] = jnp.zeros(y1.shape, F32)
        cp.wait()
        gather(0, xg0)

    def step(b, cur, xg_cur, xg_nxt, y_cur, y_prv):
        slot = cur % depth
        half = MOE_BLOCK // 2
        xb = jnp.concatenate([xg_cur[pl.ds(j, MOE_BLOCK, stride=rpt), :] for j in range(rpt)], axis=1).astype(BF16)
        gt = _dot(xb, wg_ring[slot].astype(BF16))
        up = _dot(xb, wu_ring[slot].astype(BF16))
        gather(b + 1, xg_nxt, 0, half)
        scatter(b - 1, y_prv, 0, half)
        y = _dot((_silu(gt) * up).astype(BF16), wd_ring[slot].astype(BF16))
        gather(b + 1, xg_nxt, half, MOE_BLOCK)
        scatter(b - 1, y_prv, half, MOE_BLOCK)
        for j in range(rpt):
            y_cur[pl.ds(j, MOE_BLOCK, stride=rpt), :] = y[:, j * LANES:(j + 1) * LANES]

    def run_block(b, buffers):
        cur = ord_ref[jnp.minimum(b, nblk - 1)]

        @pl.when(jnp.logical_and(b < nblk, jnp.logical_or(b == 0, ord_ref[jnp.clip(b - 1, 0, ord_ref.shape[0] - 1)] != cur)))
        def _():
            for cp in weight_copies(cur):
                cp.wait()
            start_weights(cur + depth - 1)

        @pl.when(b <= nblk)
        def _():
            step(b, cur, *buffers)

    run_block(2 * grid_step, (xg0, xg1, y0, y1))
    run_block(2 * grid_step + 1, (xg1, xg0, y1, y0))

    @pl.when(grid_step == last_step)
    def _():
        cp = pltpu.make_async_copy(acc_ref.at[pl.ds(0, tc * rpt)], o_hbm, sem)
        cp.start()
        cp.wait()


def _routed_experts(h_tiles, batch, blk_ord, uexp, nused, row0, nvalid, nblk, skey, gflat, w_gate, w_up, w_down):
    D, ff = w_gate.shape[1], w_gate.shape[2]
    rpt = D // LANES
    tc = h_tiles.shape[1] // rpt
    nb = blk_ord.shape[0]
    n_prefetch = 8
    hbm = pl.BlockSpec(memory_space=pl.ANY)
    rows = pltpu.VMEM((MOE_BLOCK * rpt, LANES), F32)
    resident = pltpu.VMEM(((tc + 1) * rpt, LANES), F32)
    grid_spec = pltpu.PrefetchScalarGridSpec(
        num_scalar_prefetch=n_prefetch, grid=(nb // 2 + 1,),
        in_specs=[hbm, hbm, hbm, hbm],
        out_specs=hbm,
        scratch_shapes=[resident, resident, rows, rows, rows, rows,
                        pltpu.VMEM((WEIGHT_RING, D, ff), F32), pltpu.VMEM((WEIGHT_RING, D, ff), F32),
                        pltpu.VMEM((WEIGHT_RING, ff, D), F32),
                        pltpu.SemaphoreType.DMA(()), pltpu.SemaphoreType.DMA((3, WEIGHT_RING))],
    )
    return pl.pallas_call(
        functools.partial(_experts_kernel, batch=batch), grid_spec=grid_spec,
        out_shape=jax.ShapeDtypeStruct(h_tiles.shape, F32),
        input_output_aliases={n_prefetch: 0},
        compiler_params=_cparams(("arbitrary",), vmem=56 * 1024 * 1024),
        name="moe_routed_experts",
    )(blk_ord, uexp, nused, row0, nvalid, nblk, skey, gflat, h_tiles, w_gate, w_up, w_down)


def _dispatch_tables(eidx, gates, counts, B, S):
    n_assign = S * TOP_K
    assert n_assign & (n_assign - 1) == 0
    e = eidx.reshape(TOP_K, B, S)
    t = lax.broadcasted_iota(I32, e.shape, 2)
    k = lax.broadcasted_iota(I32, e.shape, 0)
    keys = (e * n_assign + t * TOP_K + k).transpose(1, 0, 2).reshape(B, n_assign)
    spare = jnp.zeros((MOE_BLOCK,), I32)
    skey = jnp.stack([jnp.concatenate([lax.sort(keys[b]), spare]) for b in range(B)], axis=0)
    seg = jnp.cumsum(counts, axis=1) - counts
    nblocks_e = (counts + MOE_BLOCK - 1) // MOE_BLOCK
    bend = jnp.cumsum(nblocks_e, axis=1)
    bstart = bend - nblocks_e
    nb = n_assign // MOE_BLOCK + N_EXPERTS
    bidx = jnp.arange(nb, dtype=I32)
    blk_e = jnp.minimum(jnp.sum((bend[:, None, :] <= bidx[None, :, None]).astype(I32), axis=-1), N_EXPERTS - 1)
    onehot = (blk_e[:, :, None] == jnp.arange(N_EXPERTS, dtype=I32)[None, None, :]).astype(I32)
    pick = lambda v: jnp.sum(onehot * v[:, None, :], axis=-1)
    off = (bidx[None, :] - pick(bstart)) * MOE_BLOCK
    row0 = pick(seg) + off
    nvalid = jnp.clip(pick(counts) - off, 0, MOE_BLOCK)
    used = (counts > 0).astype(I32)
    rank = jnp.cumsum(used, axis=1) - 1
    blk_ord = pick(rank)
    slots = jnp.arange(N_EXPERTS, dtype=I32)
    uexp = jnp.sum(jnp.where((rank[:, None, :] == slots[None, :, None]) & (used[:, None, :] > 0), slots[None, None, :], 0),
                   axis=-1)
    gflat = gates.reshape(TOP_K, B, S).transpose(1, 2, 0).reshape(B, n_assign)
    gflat = jnp.concatenate([gflat, jnp.zeros((B, TOP_K), F32)], axis=1)
    return blk_ord, uexp, jnp.sum(used, axis=1, keepdims=True), row0, nvalid, bend[:, -1:], skey, gflat


def _shared_kernel(h_ref, wg_ref, wu_ref, wd_ref, routed_ref, x1_ref, gt_ref, o_ref):
    hb = h_ref[0]
    act = _silu(_dot(hb, wg_ref[...])) * _dot(hb, wu_ref[...])
    shared = _dot(act.astype(BF16), wd_ref[...])
    o_ref[0] = x1_ref[0] + gt_ref[0] * (_from_token_tiles(routed_ref, hb.shape[0]) + shared)


def _shared_and_residual(h2, wsg, wsu, wsd, routed, x1, gt2, tm):
    B, S, D = h2.shape
    row = lambda b, i: (b, i, 0)
    xs = pl.BlockSpec((1, tm, D), row)
    const2 = lambda a: pl.BlockSpec(a.shape, lambda b, i: (0, 0))
    return pl.pallas_call(
        _shared_kernel, grid=(B, S // tm),
        in_specs=[xs, const2(wsg), const2(wsu), const2(wsd), pl.BlockSpec((1, tm * (D // LANES), LANES), row), xs,
                  pl.BlockSpec((1, 1, D), lambda b, i: (b, 0, 0))],
        out_specs=xs,
        out_shape=jax.ShapeDtypeStruct((B, S, D), F32),
        compiler_params=_cparams(("parallel", "parallel")),
        name="shared_expert_residual",
    )(h2, wsg, wsu, wsd, routed, x1, gt2)


def _prep_in_weights(w_in, g_q_dil, g_k_dil, g_q_nsa, g_k_slc, g_k_win):
    D = w_in.shape[0]
    pad = jnp.zeros((D, IN_COLS_PADDED - w_in.shape[1]), w_in.dtype)
    w_all = jnp.concatenate([w_in, pad], axis=1).astype(BF16)
    one = jnp.ones((LANES,), F32)
    gcol = jnp.concatenate([
        jnp.tile(g_q_dil, NH_DIL), jnp.tile(g_k_dil, NH_DIL), jnp.ones((512,), F32), jnp.tile(g_q_nsa, NH_NSA),
        one, one, jnp.tile(g_k_slc, NKV_NSA), one, jnp.tile(g_k_win, NKV_NSA), one, one]).reshape(1, IN_COLS_PADDED)
    return w_all, gcol


def _rope_tables(S):
    inv_freq = ROPE_THETA ** (-jnp.arange(ROT_HALF, dtype=F32) / ROT_HALF)
    ang = jnp.arange(S).astype(F32)[:, None] * inv_freq[None, :]
    cos, sin = jnp.cos(ang), jnp.sin(ang)
    zeros = jnp.zeros((S, HD - ROT_DIM), F32)
    z8 = jnp.zeros((S, ROT_HALF), F32)
    cos_h = jnp.concatenate([cos, cos, jnp.ones((S, HD - ROT_DIM), F32)], axis=1)
    sa_h = jnp.concatenate([-sin, z8, zeros], axis=1)
    sb_h = jnp.concatenate([z8, sin, zeros], axis=1)
    two = lambda t: jnp.concatenate([t, t], axis=1)
    return two(cos_h), two(sa_h), two(sb_h)


def _prep_compress(pe, w1, w2):
    eye = jnp.eye(NKV_NSA, dtype=F32)
    w1r = w1.reshape(CMP_LEN, HD, CMP_HIDDEN)

    def half(w1h, peh):
        w = jnp.einsum("ldh,gk->lgdkh", w1h, eye).reshape(CMP_STRIDE * LANES, NKV_NSA * CMP_HIDDEN)
        p = jnp.broadcast_to(peh[:, None, :], (CMP_STRIDE, NKV_NSA, HD)).reshape(1, CMP_STRIDE * LANES)
        return p, w.astype(BF16)

    pa, wa = half(w1r[:CMP_STRIDE], pe[:CMP_STRIDE])
    pb, wb = half(w1r[CMP_STRIDE:], pe[CMP_STRIDE:])
    w2bd = jnp.einsum("hd,gk->ghkd", w2, eye).reshape(NKV_NSA * CMP_HIDDEN, LANES).astype(BF16)
    return pa, pb, wa, wb, w2bd


def _overlap_tables(S):
    ncb = S // CMP_STRIDE
    n_sel = S // SEL_BLOCK
    cs = np.arange(ncb) * CMP_STRIDE
    ss = np.arange(n_sel) * SEL_BLOCK
    ov = np.clip(np.minimum(cs[:, None] + CMP_LEN, ss[None, :] + SEL_BLOCK) - np.maximum(cs[:, None], ss[None, :]), 0, None)
    ov = ov.astype(np.float32) / CMP_STRIDE
    ovl = np.zeros((ncb, LANES), np.float32)
    ovr = np.zeros((ncb, LANES), np.float32)
    ovl[:, :n_sel] = ov
    ovr[:, HD:HD + n_sel] = ov
    return jnp.asarray(ovl, BF16), jnp.asarray(ovr, BF16)


def _block_ones():
    r = np.arange(LANES)
    return jnp.asarray((r[:, None] // HD == r[None, :] // HD).astype(np.float32), BF16)


def _layer(x, c, w_ada, b_ada, g_norm_mix, g_norm_ffn, w_in, g_q_dil, g_k_dil, g_q_nsa, g_k_cmp, g_k_slc, g_k_win,
           cmp_pe_k, cmp_w1_k, cmp_w2_k, cmp_pe_v, cmp_w1_v, cmp_w2_v, w_out, w_router, router_bias,
           w_gate, w_up, w_down, ws_gate, ws_up, ws_down):
    B, S, D = x.shape
    ones_bd = _block_ones()

    c_pad = jnp.zeros((8, D), F32).at[:B].set(c)
    mod = _ada_mod(c_pad, w_ada, b_ada.reshape(1, -1))[:B]
    sh1, sc1, gt1, sh2, sc2, gt2 = [m.reshape(B, 1, D) for m in jnp.split(mod, 6, axis=-1)]

    w_all, gcol = _prep_in_weights(w_in, g_q_dil, g_k_dil, g_q_nsa, g_k_slc, g_k_win)
    cos_t, sa_t, sb_t = _rope_tables(S)
    qa, ka, va, qn, qr, kc, vc, ksl, vsl, kw, vw, gate = _in_projection(
        x, g_norm_mix.reshape(1, D), sc1, sh1, w_all, gcol, cos_t, sa_t, sb_t, ones_bd, tm=512)

    o_a = _dilated_attention(qa, ka, va)

    kcmp, vcmp = _compress(kc, vc, _prep_compress(cmp_pe_k, cmp_w1_k, cmp_w2_k),
                           _prep_compress(cmp_pe_v, cmp_w1_v, cmp_w2_v), jnp.tile(g_k_cmp, NKV_NSA).reshape(1, LANES),
                           ones_bd)
    ovl, ovr = _overlap_tables(S)
    o_cmp, sel = _cmp_select(qn, kcmp, vcmp, ovl, ovr, gate, tq=128)
    o_cs = _nsa_branch(qr, ksl, vsl, sel, o_cmp, gate, tq=256, tk=512, mode="selected", out_dtype=F32)
    o_b = _nsa_branch(qr, kw, vw, None, o_cs, gate, tq=256, tk=256, mode="window", out_dtype=BF16)

    w_out_b = w_out.astype(BF16)
    x1, h2, h2_tiles = _out_projection(o_a, o_b, w_out_b[:NH_DIL * HD], w_out_b[NH_DIL * HD:], x, gt1,
                                       g_norm_ffn.reshape(1, D), sc2, sh2, tm=256)

    tm_r = 256
    eidx, gates, counts = _router(h2, w_router.T.astype(BF16),
                                  jnp.broadcast_to(router_bias.reshape(N_EXPERTS, 1), (N_EXPERTS, tm_r)), tm=tm_r)
    tables = _dispatch_tables(eidx, gates, counts[:, :, 0].astype(I32), B, S)
    routed = h2_tiles
    for b in range(B):
        routed = _routed_experts(routed, b, *[tbl[b] for tbl in tables], w_gate, w_up, w_down)
    return _shared_and_residual(h2, ws_gate.astype(BF16), ws_up.astype(BF16), ws_down.astype(BF16), routed, x1, gt2,
                                tm=256)


def kernel(x, c, w_ada, b_ada, g_norm_mix, g_norm_ffn, w_in, g_q_dil, g_k_dil, g_q_nsa, g_k_cmp, g_k_slc, g_k_win, cmp_pe_k, cmp_w1_k, cmp_w2_k, cmp_pe_v, cmp_w1_v, cmp_w2_v, w_out, w_router, router_bias, w_gate, w_up, w_down, ws_gate, ws_up, ws_down):
    params = (w_ada, b_ada, g_norm_mix, g_norm_ffn, w_in, g_q_dil, g_k_dil, g_q_nsa, g_k_cmp, g_k_slc, g_k_win,
              cmp_pe_k, cmp_w1_k, cmp_w2_k, cmp_pe_v, cmp_w1_v, cmp_w2_v, w_out, w_router, router_bias,
              w_gate, w_up, w_down, ws_gate, ws_up, ws_down)
    for layer in range(w_ada.shape[0]):
        x = _layer(x, c, *[a[layer] for a in params])
    return x
```
